```python
import jax
import jax.numpy as jnp
from jax import lax
import numpy as np

D_MODEL = 2048
BATCH = 8
SEQ = 8192
DEPTH = 4

GRID_W = 64
CTX_LEN = 256
EPS = 1e-6
LOG_FLOOR = 1e-30
N_EVEN = (DEPTH + 1) // 2
N_ODD = DEPTH // 2
N_MOD = 6

A_HEADS = 8
A_DK = 128
A_DV = D_MODEL // 16
A_KWIDTH = A_HEADS * A_DK
A_WIDTH = A_HEADS * A_DV
SCAN_CHUNK = 64

B_GROUPS = 8
B_CH = D_MODEL // 16
B_WIDTH = B_GROUPS * B_CH
B_CHUNK = 128

Q_OFF = 0
FF_OFF = Q_OFF + A_KWIDTH
FB_OFF = FF_OFF + A_KWIDTH
I_OFF = FB_OFF + A_KWIDTH
G_OFF = I_OFF + A_WIDTH
U_OFF = G_OFF + A_WIDTH
V_OFF = U_OFF + B_WIDTH
IN_COLS = V_OFF + B_WIDTH
MIX_WIDTH = A_WIDTH + B_WIDTH

C_GROUPS = 4
C_CH = D_MODEL // C_GROUPS
POOL_WINDOWS = (2, 4, 8, 16)

D_FF = 4 * D_MODEL

kernel_name = 'hybrid_hgrn2_chunkmlp_pool_dit'


def rms_norm(x, g):
    xf = x.astype(jnp.float32)
    y = xf * lax.rsqrt(jnp.mean(xf * xf, axis=-1, keepdims=True) + EPS)
    return (y * g.astype(jnp.float32)).astype(x.dtype)


def modulate(h, shift, scale):
    return h * (1.0 + scale[:, None, :]) + shift[:, None, :]


def ada_mods(cvec, w, b):
    m = jax.nn.silu(cvec) @ w + b
    return jnp.split(m, N_MOD, axis=-1)


def split_heads(t, dh):
    return t.reshape(t.shape[:-1] + (t.shape[-1] // dh, dh))


def flip_seq(t):
    return jnp.flip(t, axis=1)


def lower_bounds(lb_logits):
    p = jax.nn.softmax(lb_logits.astype(jnp.float32), axis=1)
    return jnp.cumsum(p, axis=1) - p[:, :1]


def hgrn_gates(f_pre, lb):
    f = f_pre.astype(jnp.float32)
    fg = lb + (1.0 - lb) * jax.nn.sigmoid(f)
    log_f = jnp.log(jnp.maximum(fg, LOG_FLOOR))
    k = (1.0 - lb) * jax.nn.sigmoid(-f)
    return split_heads(k, A_DK), split_heads(log_f, A_DK)


def gla_chunk_scan(q, k, v, log_f, s0):
    bsz, seq_len, n_heads, _ = q.shape
    dv = v.shape[-1]
    n_chunks = seq_len // SCAN_CHUNK

    def to_chunks(t):
        return t.reshape(bsz, n_chunks, SCAN_CHUNK, n_heads, t.shape[-1]).transpose(1, 0, 3, 2, 4)

    lower = jnp.tril(jnp.ones((SCAN_CHUNK, SCAN_CHUNK), dtype=bool))[None, None, :, :, None]

    def step(state, blk):
        qb, kb, vb, gb = blk
        b = jnp.cumsum(gb, axis=2)
        o_inter = jnp.einsum('bhtk,bhkv->bhtv', qb * jnp.exp(b), state)
        diff = b[:, :, :, None, :] - b[:, :, None, :, :]
        decay = jnp.where(lower, jnp.exp(jnp.minimum(diff, 0.0)), 0.0)
        scores = jnp.einsum('bhtk,bhsk,bhtsk->bhts', qb, kb, decay)
        o_intra = jnp.einsum('bhts,bhsv->bhtv', scores, vb)
        b_last = b[:, :, -1:, :]
        new_state = (jnp.exp(b_last[:, :, 0, :])[..., None] * state
                     + jnp.einsum('bhsk,bhsv->bhkv', kb * jnp.exp(b_last - b), vb))
        return new_state, o_inter + o_intra

    s_fin, o = lax.scan(step, s0, (to_chunks(q), to_chunks(k), to_chunks(v), to_chunks(log_f)))
    o = o.transpose(1, 0, 3, 2, 4).reshape(bsz, seq_len, n_heads, dv)
    return o, s_fin


def gla_final_state(k, v, log_f):
    b = jnp.cumsum(log_f, axis=1)
    w = jnp.exp(b[:, -1:] - b)
    return jnp.einsum('blhk,blhv->bhkv', k * w, v)


def hgrn_stream(q, f_fwd, f_bwd, i, lb_f, lb_b, s0_f, s0_b):
    kf, gf = hgrn_gates(f_fwd, lb_f)
    kb, gb = hgrn_gates(f_bwd, lb_b)
    qh = split_heads(q.astype(jnp.float32), A_DK)
    vh = split_heads(i.astype(jnp.float32), A_DV)
    o_f, s_f = gla_chunk_scan(qh, kf, vh, gf, s0_f)
    o_b, s_b = gla_chunk_scan(flip_seq(qh), flip_seq(kb), flip_seq(vh), flip_seq(gb), s0_b)
    return o_f + flip_seq(o_b), s_f, s_b


def hgrn_final_states(f_fwd, f_bwd, i, lb_f, lb_b):
    kf, gf = hgrn_gates(f_fwd, lb_f)
    kb, gb = hgrn_gates(f_bwd, lb_b)
    vh = split_heads(i.astype(jnp.float32), A_DV)
    s_f = gla_final_state(kf, vh, gf)
    s_b = gla_final_state(flip_seq(kb), flip_seq(vh), flip_seq(gb))
    return s_f, s_b


def hgrn_readout(o, g, norm_g):
    o = o * lax.rsqrt(jnp.mean(o * o, axis=-1, keepdims=True) + EPS)
    o = o.reshape(o.shape[:2] + (A_WIDTH,)) * norm_g.astype(jnp.float32)
    return (o * jax.nn.silu(g.astype(jnp.float32))).astype(g.dtype)


def chunk_token_mlp(u, v, ws, bias, vnorm_g):
    bsz, seq_len, _ = u.shape
    u32 = jax.nn.gelu(u.astype(jnp.float32), approximate=False)
    vg = split_heads(jax.nn.gelu(v.astype(jnp.float32), approximate=False), B_CH)
    mu = jnp.mean(vg, axis=-1, keepdims=True)
    var = jnp.mean(jnp.square(vg - mu), axis=-1, keepdims=True)
    vn = (vg - mu) * lax.rsqrt(var + EPS) * split_heads(vnorm_g.astype(jnp.float32), B_CH)
    vc = vn.reshape(bsz, seq_len // B_CHUNK, B_CHUNK, B_GROUPS, B_CH)
    mixed = (jnp.einsum('gts,bnsgc->bntgc', ws.astype(jnp.float32), vc)
             + bias.astype(jnp.float32).T[:, :, None])
    return (u32 * mixed.reshape(bsz, seq_len, B_WIDTH)).astype(v.dtype)


def even_mixer(h_lat, h_ctx, w_in, w_out, lb_f, lb_b, a_norm_g, b_ws, b_bias, b_vnorm_g, ctx_out_needed):
    bsz = h_ctx.shape[0]
    if ctx_out_needed:
        zero = jnp.zeros((bsz, A_HEADS, A_DK, A_DV), jnp.float32)
        pc = h_ctx @ w_in
        oc, s_f, s_b = hgrn_stream(pc[..., Q_OFF:FF_OFF], pc[..., FF_OFF:FB_OFF], pc[..., FB_OFF:I_OFF],
                                   pc[..., I_OFF:G_OFF], lb_f, lb_b, zero, zero)
        yc = jnp.concatenate([hgrn_readout(oc, pc[..., G_OFF:U_OFF], a_norm_g),
                              chunk_token_mlp(pc[..., U_OFF:V_OFF], pc[..., V_OFF:IN_COLS], b_ws, b_bias, b_vnorm_g)],
                             axis=-1) @ w_out
    else:
        pc = h_ctx @ w_in[:, FF_OFF:G_OFF]
        s_f, s_b = hgrn_final_states(pc[..., :A_KWIDTH], pc[..., A_KWIDTH:2 * A_KWIDTH],
                                     pc[..., 2 * A_KWIDTH:], lb_f, lb_b)
        yc = None
    pl = h_lat @ w_in
    ol, _, _ = hgrn_stream(pl[..., Q_OFF:FF_OFF], pl[..., FF_OFF:FB_OFF], pl[..., FB_OFF:I_OFF],
                           pl[..., I_OFF:G_OFF], lb_f, lb_b, s_f, s_b)
    yl = jnp.concatenate([hgrn_readout(ol, pl[..., G_OFF:U_OFF], a_norm_g),
                          chunk_token_mlp(pl[..., U_OFF:V_OFF], pl[..., V_OFF:IN_COLS], b_ws, b_bias, b_vnorm_g)],
                         axis=-1) @ w_out
    return yl, yc


def window_bounds(n, k):
    t = np.arange(n)
    return np.clip(t - k // 2, 0, n), np.clip(t - k // 2 + k, 0, n)


def pool_minus_self_2d(x, k):
    bsz, seq_len, ch = x.shape
    rows = seq_len // GRID_W
    g = x.reshape(bsz, rows, GRID_W, ch)
    sat = jnp.pad(jnp.cumsum(jnp.cumsum(g, axis=1), axis=2), ((0, 0), (1, 0), (1, 0), (0, 0)))
    rlo, rhi = window_bounds(rows, k)
    clo, chi = window_bounds(GRID_W, k)

    def corner(r, cidx):
        return sat[:, r][:, :, cidx]

    s = corner(rhi, chi) - corner(rlo, chi) - corner(rhi, clo) + corner(rlo, clo)
    cnt = ((rhi - rlo)[:, None] * (chi - clo)[None, :]).astype(np.float32)
    return (s / cnt[None, :, :, None] - g).reshape(bsz, seq_len, ch)


def pool_minus_self_1d(x, k):
    seq_len = x.shape[1]
    cs = jnp.pad(jnp.cumsum(x, axis=1), ((0, 0), (1, 0), (0, 0)))
    lo, hi = window_bounds(seq_len, k)
    return (cs[:, hi] - cs[:, lo]) / (hi - lo).astype(np.float32)[None, :, None] - x


def pool_mixer(h, w_in, w_grp, b_grp, scale, on_grid):
    p = (h @ w_in).astype(jnp.float32)
    pool = pool_minus_self_2d if on_grid else pool_minus_self_1d
    z = jnp.stack([pool(p[..., gi * C_CH:(gi + 1) * C_CH], k) for gi, k in enumerate(POOL_WINDOWS)], axis=2)
    y = jnp.einsum('blgc,gcd->blgd', z, w_grp.astype(jnp.float32)) + b_grp.astype(jnp.float32)
    return (y.reshape(h.shape[:2] + (D_MODEL,)) * scale.astype(jnp.float32)).astype(h.dtype)


def sq_relu_mlp(h, w1, w2):
    return jnp.square(jax.nn.relu(h @ w1)) @ w2


def _fwd_setup_inputs(seed: int = 0) -> dict:
    key = jax.random.key(seed)
    ks = jax.random.split(key, 22)

    def nrm(k, shape, s):
        return jax.random.normal(k, shape, jnp.float32) * s

    d = D_MODEL
    return {
        'x': nrm(ks[0], (BATCH, SEQ, d), 1.0),
        'c': nrm(ks[1], (BATCH, d), 1.0),
        'ctx': nrm(ks[2], (BATCH, CTX_LEN, d), 1.0),
        'c_ctx': nrm(ks[3], (d,), 1.0),
        'w_ada': nrm(ks[4], (DEPTH, d, N_MOD * d), 0.5 * d ** -0.5),
        'b_ada': nrm(ks[5], (DEPTH, N_MOD * d), 0.01),
        'g_norm_mix': 1.0 + nrm(ks[6], (DEPTH, d), 0.05),
        'g_norm_ffn': 1.0 + nrm(ks[7], (DEPTH, d), 0.05),
        'w_in_even': nrm(ks[8], (N_EVEN, d, IN_COLS), d ** -0.5),
        'w_out_even': nrm(ks[9], (N_EVEN, MIX_WIDTH, d), MIX_WIDTH ** -0.5),
        'lb_logits': nrm(ks[10], (2, N_EVEN, A_KWIDTH), 0.5),
        'g_hgrn_out': 1.0 + nrm(ks[11], (N_EVEN, A_WIDTH), 0.05),
        'w_spatial': nrm(ks[12], (N_EVEN, B_GROUPS, B_CHUNK, B_CHUNK), B_CHUNK ** -0.5),
        'b_spatial': 1.0 + nrm(ks[13], (N_EVEN, B_GROUPS, B_CHUNK), 0.02),
        'g_spatial_v': 1.0 + nrm(ks[14], (N_EVEN, B_WIDTH), 0.05),
        'w_in_pool': nrm(ks[15], (N_ODD, d, d), d ** -0.5),
        'w_grp_pool': nrm(ks[16], (N_ODD, C_GROUPS, C_CH, C_CH), C_CH ** -0.5),
        'b_grp_pool': nrm(ks[17], (N_ODD, C_GROUPS, C_CH), 0.01),
        'scale_pool': 1.0 + nrm(ks[18], (N_ODD, d), 0.05),
        'w_ffn_up': nrm(ks[19], (DEPTH, d, D_FF), d ** -0.5),
        'w_ffn_down': nrm(ks[20], (DEPTH, D_FF, d), D_FF ** -0.5),
        'g_norm_final': 1.0 + nrm(ks[21], (d,), 0.05),
    }


def _fwd_reference(x, c, ctx, c_ctx, w_ada, b_ada, g_norm_mix, g_norm_ffn, w_in_even, w_out_even, lb_logits,
              g_hgrn_out, w_spatial, b_spatial, g_spatial_v, w_in_pool, w_grp_pool, b_grp_pool, scale_pool,
              w_ffn_up, w_ffn_down, g_norm_final):
    lbs = lower_bounds(lb_logits)
    last_even = DEPTH - 1 if (DEPTH - 1) % 2 == 0 else DEPTH - 2
    c_ctx_row = c_ctx[None, :]
    ctx_s = ctx
    for layer in range(DEPTH):
        ctx_out_needed = layer < last_even
        ctx_read = layer <= last_even
        sh1, sc1, gt1, sh2, sc2, gt2 = ada_mods(c, w_ada[layer], b_ada[layer])
        hl = modulate(rms_norm(x, g_norm_mix[layer]), sh1, sc1)
        hc = None
        if ctx_read:
            csh1, csc1, cgt1, csh2, csc2, cgt2 = ada_mods(c_ctx_row, w_ada[layer], b_ada[layer])
            hc = modulate(rms_norm(ctx_s, g_norm_mix[layer]), csh1, csc1)
        if layer % 2 == 0:
            e = layer // 2
            yl, yc = even_mixer(hl, hc, w_in_even[e], w_out_even[e], lbs[0, e], lbs[1, e], g_hgrn_out[e],
                                w_spatial[e], b_spatial[e], g_spatial_v[e], ctx_out_needed)
        else:
            o = layer // 2
            yl = pool_mixer(hl, w_in_pool[o], w_grp_pool[o], b_grp_pool[o], scale_pool[o], True)
            yc = pool_mixer(hc, w_in_pool[o], w_grp_pool[o], b_grp_pool[o], scale_pool[o], False) if ctx_out_needed else None
        x = x + gt1[:, None, :] * yl
        x = x + gt2[:, None, :] * sq_relu_mlp(modulate(rms_norm(x, g_norm_ffn[layer]), sh2, sc2),
                                              w_ffn_up[layer], w_ffn_down[layer])
        if ctx_out_needed:
            ctx_s = ctx_s + cgt1[:, None, :] * yc
            ctx_s = ctx_s + cgt2[:, None, :] * sq_relu_mlp(modulate(rms_norm(ctx_s, g_norm_ffn[layer]), csh2, csc2),
                                                          w_ffn_up[layer], w_ffn_down[layer])
    return rms_norm(x, g_norm_final)


import jax as _jax
import jax.numpy as _jnp

TWIN_FORMAT = 'train_step'
FWD_PARAMS = ['x', 'c', 'ctx', 'c_ctx', 'w_ada', 'b_ada', 'g_norm_mix', 'g_norm_ffn', 'w_in_even', 'w_out_even', 'lb_logits', 'g_hgrn_out', 'w_spatial', 'b_spatial', 'g_spatial_v', 'w_in_pool', 'w_grp_pool', 'b_grp_pool', 'scale_pool', 'w_ffn_up', 'w_ffn_down', 'g_norm_final']
TWIN_WEIGHTS = ['c_ctx', 'w_ada', 'b_ada', 'g_norm_mix', 'g_norm_ffn', 'w_in_even', 'w_out_even', 'lb_logits', 'g_hgrn_out', 'w_spatial', 'b_spatial', 'g_spatial_v', 'w_in_pool', 'w_grp_pool', 'b_grp_pool', 'scale_pool', 'w_ffn_up', 'w_ffn_down', 'g_norm_final']
TWIN_DIFF_INPUT = 'x'
TWIN_INPUTS = ['x', 'c', 'ctx', 'c_ctx', 'w_ada', 'b_ada', 'g_norm_mix', 'g_norm_ffn', 'w_in_even', 'w_out_even', 'lb_logits', 'g_hgrn_out', 'w_spatial', 'b_spatial', 'g_spatial_v', 'w_in_pool', 'w_grp_pool', 'b_grp_pool', 'scale_pool', 'w_ffn_up', 'w_ffn_down', 'g_norm_final', 'loss_target', 'm_c_ctx', 'm_w_ada', 'm_b_ada', 'm_g_norm_mix', 'm_g_norm_ffn', 'm_w_in_even', 'm_w_out_even', 'm_lb_logits', 'm_g_hgrn_out', 'm_w_spatial', 'm_b_spatial', 'm_g_spatial_v', 'm_w_in_pool', 'm_w_grp_pool', 'm_b_grp_pool', 'm_scale_pool', 'm_w_ffn_up', 'm_w_ffn_down', 'm_g_norm_final', 'v_c_ctx', 'v_w_ada', 'v_b_ada', 'v_g_norm_mix', 'v_g_norm_ffn', 'v_w_in_even', 'v_w_out_even', 'v_lb_logits', 'v_g_hgrn_out', 'v_w_spatial', 'v_b_spatial', 'v_g_spatial_v', 'v_w_in_pool', 'v_w_grp_pool', 'v_b_grp_pool', 'v_scale_pool', 'v_w_ffn_up', 'v_w_ffn_down', 'v_g_norm_final']
TWIN_OUTPUTS = ['loss', 'grad_x', 'grad_c_ctx', 'grad_w_ada', 'grad_b_ada', 'grad_g_norm_mix', 'grad_g_norm_ffn', 'grad_w_in_even', 'grad_w_out_even', 'grad_lb_logits', 'grad_g_hgrn_out', 'grad_w_spatial', 'grad_b_spatial', 'grad_g_spatial_v', 'grad_w_in_pool', 'grad_w_grp_pool', 'grad_b_grp_pool', 'grad_scale_pool', 'grad_w_ffn_up', 'grad_w_ffn_down', 'grad_g_norm_final', 'delta_c_ctx', 'delta_w_ada', 'delta_b_ada', 'delta_g_norm_mix', 'delta_g_norm_ffn', 'delta_w_in_even', 'delta_w_out_even', 'delta_lb_logits', 'delta_g_hgrn_out', 'delta_w_spatial', 'delta_b_spatial', 'delta_g_spatial_v', 'delta_w_in_pool', 'delta_w_grp_pool', 'delta_b_grp_pool', 'delta_scale_pool', 'delta_w_ffn_up', 'delta_w_ffn_down', 'delta_g_norm_final', 'new_m_c_ctx', 'new_m_w_ada', 'new_m_b_ada', 'new_m_g_norm_mix', 'new_m_g_norm_ffn', 'new_m_w_in_even', 'new_m_w_out_even', 'new_m_lb_logits', 'new_m_g_hgrn_out', 'new_m_w_spatial', 'new_m_b_spatial', 'new_m_g_spatial_v', 'new_m_w_in_pool', 'new_m_w_grp_pool', 'new_m_b_grp_pool', 'new_m_scale_pool', 'new_m_w_ffn_up', 'new_m_w_ffn_down', 'new_m_g_norm_final', 'new_v_c_ctx', 'new_v_w_ada', 'new_v_b_ada', 'new_v_g_norm_mix', 'new_v_g_norm_ffn', 'new_v_w_in_even', 'new_v_w_out_even', 'new_v_lb_logits', 'new_v_g_hgrn_out', 'new_v_w_spatial', 'new_v_b_spatial', 'new_v_g_spatial_v', 'new_v_w_in_pool', 'new_v_w_grp_pool', 'new_v_b_grp_pool', 'new_v_scale_pool', 'new_v_w_ffn_up', 'new_v_w_ffn_down', 'new_v_g_norm_final']
TWIN_LEAF_KINDS = {'loss': 'loss', 'grad_x': 'grad_x', 'grad_c_ctx': 'grad_w', 'grad_w_ada': 'grad_w', 'grad_b_ada': 'grad_w', 'grad_g_norm_mix': 'grad_w', 'grad_g_norm_ffn': 'grad_w', 'grad_w_in_even': 'grad_w', 'grad_w_out_even': 'grad_w', 'grad_lb_logits': 'grad_w', 'grad_g_hgrn_out': 'grad_w', 'grad_w_spatial': 'grad_w', 'grad_b_spatial': 'grad_w', 'grad_g_spatial_v': 'grad_w', 'grad_w_in_pool': 'grad_w', 'grad_w_grp_pool': 'grad_w', 'grad_b_grp_pool': 'grad_w', 'grad_scale_pool': 'grad_w', 'grad_w_ffn_up': 'grad_w', 'grad_w_ffn_down': 'grad_w', 'grad_g_norm_final': 'grad_w', 'delta_c_ctx': 'delta_w', 'delta_w_ada': 'delta_w', 'delta_b_ada': 'delta_w', 'delta_g_norm_mix': 'delta_w', 'delta_g_norm_ffn': 'delta_w', 'delta_w_in_even': 'delta_w', 'delta_w_out_even': 'delta_w', 'delta_lb_logits': 'delta_w', 'delta_g_hgrn_out': 'delta_w', 'delta_w_spatial': 'delta_w', 'delta_b_spatial': 'delta_w', 'delta_g_spatial_v': 'delta_w', 'delta_w_in_pool': 'delta_w', 'delta_w_grp_pool': 'delta_w', 'delta_b_grp_pool': 'delta_w', 'delta_scale_pool': 'delta_w', 'delta_w_ffn_up': 'delta_w', 'delta_w_ffn_down': 'delta_w', 'delta_g_norm_final': 'delta_w', 'new_m_c_ctx': 'new_m', 'new_m_w_ada': 'new_m', 'new_m_b_ada': 'new_m', 'new_m_g_norm_mix': 'new_m', 'new_m_g_norm_ffn': 'new_m', 'new_m_w_in_even': 'new_m', 'new_m_w_out_even': 'new_m', 'new_m_lb_logits': 'new_m', 'new_m_g_hgrn_out': 'new_m', 'new_m_w_spatial': 'new_m', 'new_m_b_spatial': 'new_m', 'new_m_g_spatial_v': 'new_m', 'new_m_w_in_pool': 'new_m', 'new_m_w_grp_pool': 'new_m', 'new_m_b_grp_pool': 'new_m', 'new_m_scale_pool': 'new_m', 'new_m_w_ffn_up': 'new_m', 'new_m_w_ffn_down': 'new_m', 'new_m_g_norm_final': 'new_m', 'new_v_c_ctx': 'new_v', 'new_v_w_ada': 'new_v', 'new_v_b_ada': 'new_v', 'new_v_g_norm_mix': 'new_v', 'new_v_g_norm_ffn': 'new_v', 'new_v_w_in_even': 'new_v', 'new_v_w_out_even': 'new_v', 'new_v_lb_logits': 'new_v', 'new_v_g_hgrn_out': 'new_v', 'new_v_w_spatial': 'new_v', 'new_v_b_spatial': 'new_v', 'new_v_g_spatial_v': 'new_v', 'new_v_w_in_pool': 'new_v', 'new_v_w_grp_pool': 'new_v', 'new_v_b_grp_pool': 'new_v', 'new_v_scale_pool': 'new_v', 'new_v_w_ffn_up': 'new_v', 'new_v_w_ffn_down': 'new_v', 'new_v_g_norm_final': 'new_v'}


def _forward(args):
    return _fwd_reference(*[args[k] for k in FWD_PARAMS])


def _output_shape():
    def fwd():
        inp = _fwd_setup_inputs(0)
        return _fwd_reference(*[inp[k] for k in FWD_PARAMS])
    out = _jax.eval_shape(fwd)
    return out.shape, out.dtype

N_MICROBATCH = 1
ADAM_LR = 0.001
ADAM_B1 = 0.9
ADAM_B2 = 0.999
ADAM_EPS = 1e-08
ADAM_WD = 0.01
ADAM_STEP = 10
PER_EXAMPLE_BATCH_AXIS = {'x': 0, 'c': 0, 'ctx': 0, 'loss_target': 0}
SHARED_INPUTS = []
_WEIGHT_DTYPES = {'c_ctx': _jnp.float32, 'w_ada': _jnp.float32, 'b_ada': _jnp.float32, 'g_norm_mix': _jnp.float32, 'g_norm_ffn': _jnp.float32, 'w_in_even': _jnp.float32, 'w_out_even': _jnp.float32, 'lb_logits': _jnp.float32, 'g_hgrn_out': _jnp.float32, 'w_spatial': _jnp.float32, 'b_spatial': _jnp.float32, 'g_spatial_v': _jnp.float32, 'w_in_pool': _jnp.float32, 'w_grp_pool': _jnp.float32, 'b_grp_pool': _jnp.float32, 'scale_pool': _jnp.float32, 'w_ffn_up': _jnp.float32, 'w_ffn_down': _jnp.float32, 'g_norm_final': _jnp.float32}
MOMENT_SCALE = {'c_ctx': 1.235583e-03, 'w_ada': 8.103192e-02, 'b_ada': 1.667760e-01, 'g_norm_mix': 4.065893e-02, 'g_norm_ffn': 5.057234e-02, 'w_in_even': 2.727188e-02, 'w_out_even': 3.247219e-02, 'lb_logits': 6.338931e-03, 'g_hgrn_out': 2.641057e-02, 'w_spatial': 2.530316e-02, 'b_spatial': 2.651066e-02, 'g_spatial_v': 2.573772e-02, 'w_in_pool': 3.173918e-02, 'w_grp_pool': 3.193906e-02, 'b_grp_pool': 5.877719e-02, 'scale_pool': 1.281950e-01, 'w_ffn_up': 2.728366e-02, 'w_ffn_down': 6.307077e-02, 'g_norm_final': 3.232522e+01}


def _to_microbatches(a, axis):
    t = _jnp.moveaxis(a, axis, 0)
    t = t.reshape((N_MICROBATCH, t.shape[0] // N_MICROBATCH) + t.shape[1:])
    return _jnp.moveaxis(t, 1, axis + 1)


def setup_inputs(seed: int = 0) -> dict:
    inp = _fwd_setup_inputs(seed)
    key = _jax.random.fold_in(_jax.random.key(seed), 7919)
    shape, _ = _output_shape()
    out = dict(inp)
    out["loss_target"] = _jax.random.normal(_jax.random.fold_in(key, 0), shape, _jnp.float32)
    for i, name in enumerate(TWIN_WEIGHTS):
        w = inp[name].astype(_jnp.float32)
        if MOMENT_SCALE is None:
            s = _jnp.sqrt(_jnp.mean(_jnp.square(w)) + 1e-30)
        else:
            s = MOMENT_SCALE[name]
        km, kv = _jax.random.split(_jax.random.fold_in(key, i + 1))
        out[name] = w
        out["m_" + name] = s * _jax.random.normal(km, w.shape, _jnp.float32)
        out["v_" + name] = (s * s) * _jax.random.uniform(kv, w.shape, _jnp.float32, 0.5, 1.5)
    if N_MICROBATCH > 1:
        for name, axis in PER_EXAMPLE_BATCH_AXIS.items():
            out[name] = _to_microbatches(out[name], axis)
    return {'x': out['x'], 'c': out['c'], 'ctx': out['ctx'], 'c_ctx': out['c_ctx'], 'w_ada': out['w_ada'], 'b_ada': out['b_ada'], 'g_norm_mix': out['g_norm_mix'], 'g_norm_ffn': out['g_norm_ffn'], 'w_in_even': out['w_in_even'], 'w_out_even': out['w_out_even'], 'lb_logits': out['lb_logits'], 'g_hgrn_out': out['g_hgrn_out'], 'w_spatial': out['w_spatial'], 'b_spatial': out['b_spatial'], 'g_spatial_v': out['g_spatial_v'], 'w_in_pool': out['w_in_pool'], 'w_grp_pool': out['w_grp_pool'], 'b_grp_pool': out['b_grp_pool'], 'scale_pool': out['scale_pool'], 'w_ffn_up': out['w_ffn_up'], 'w_ffn_down': out['w_ffn_down'], 'g_norm_final': out['g_norm_final'], 'loss_target': out['loss_target'], 'm_c_ctx': out['m_c_ctx'], 'm_w_ada': out['m_w_ada'], 'm_b_ada': out['m_b_ada'], 'm_g_norm_mix': out['m_g_norm_mix'], 'm_g_norm_ffn': out['m_g_norm_ffn'], 'm_w_in_even': out['m_w_in_even'], 'm_w_out_even': out['m_w_out_even'], 'm_lb_logits': out['m_lb_logits'], 'm_g_hgrn_out': out['m_g_hgrn_out'], 'm_w_spatial': out['m_w_spatial'], 'm_b_spatial': out['m_b_spatial'], 'm_g_spatial_v': out['m_g_spatial_v'], 'm_w_in_pool': out['m_w_in_pool'], 'm_w_grp_pool': out['m_w_grp_pool'], 'm_b_grp_pool': out['m_b_grp_pool'], 'm_scale_pool': out['m_scale_pool'], 'm_w_ffn_up': out['m_w_ffn_up'], 'm_w_ffn_down': out['m_w_ffn_down'], 'm_g_norm_final': out['m_g_norm_final'], 'v_c_ctx': out['v_c_ctx'], 'v_w_ada': out['v_w_ada'], 'v_b_ada': out['v_b_ada'], 'v_g_norm_mix': out['v_g_norm_mix'], 'v_g_norm_ffn': out['v_g_norm_ffn'], 'v_w_in_even': out['v_w_in_even'], 'v_w_out_even': out['v_w_out_even'], 'v_lb_logits': out['v_lb_logits'], 'v_g_hgrn_out': out['v_g_hgrn_out'], 'v_w_spatial': out['v_w_spatial'], 'v_b_spatial': out['v_b_spatial'], 'v_g_spatial_v': out['v_g_spatial_v'], 'v_w_in_pool': out['v_w_in_pool'], 'v_w_grp_pool': out['v_w_grp_pool'], 'v_b_grp_pool': out['v_b_grp_pool'], 'v_scale_pool': out['v_scale_pool'], 'v_w_ffn_up': out['v_w_ffn_up'], 'v_w_ffn_down': out['v_w_ffn_down'], 'v_g_norm_final': out['v_g_norm_final']}


def _loss(weights, diff, rest, loss_target):
    with _jax.named_scope("forward"):
        args = {**rest, TWIN_DIFF_INPUT: diff, **{k: w.astype(_WEIGHT_DTYPES[k]) for k, w in weights.items()}}
        y = _forward(args)
    with _jax.named_scope("loss_head"):
        err = _jnp.square(y.astype(_jnp.float32) - loss_target)
        return 0.5 * _jnp.sum(_jnp.mean(err, axis=-1)) if err.ndim else 0.5 * err


def _adamw(w, g, m, v):
    m = ADAM_B1 * m + (1.0 - ADAM_B1) * g
    v = ADAM_B2 * v + (1.0 - ADAM_B2) * _jnp.square(g)
    m_hat = m / (1.0 - ADAM_B1 ** ADAM_STEP)
    v_hat = v / (1.0 - ADAM_B2 ** ADAM_STEP)
    delta = -ADAM_LR * (m_hat / (_jnp.sqrt(v_hat) + ADAM_EPS) + ADAM_WD * w)
    return delta, m, v


def reference(x, c, ctx, c_ctx, w_ada, b_ada, g_norm_mix, g_norm_ffn, w_in_even, w_out_even, lb_logits, g_hgrn_out, w_spatial, b_spatial, g_spatial_v, w_in_pool, w_grp_pool, b_grp_pool, scale_pool, w_ffn_up, w_ffn_down, g_norm_final, loss_target, m_c_ctx, m_w_ada, m_b_ada, m_g_norm_mix, m_g_norm_ffn, m_w_in_even, m_w_out_even, m_lb_logits, m_g_hgrn_out, m_w_spatial, m_b_spatial, m_g_spatial_v, m_w_in_pool, m_w_grp_pool, m_b_grp_pool, m_scale_pool, m_w_ffn_up, m_w_ffn_down, m_g_norm_final, v_c_ctx, v_w_ada, v_b_ada, v_g_norm_mix, v_g_norm_ffn, v_w_in_even, v_w_out_even, v_lb_logits, v_g_hgrn_out, v_w_spatial, v_b_spatial, v_g_spatial_v, v_w_in_pool, v_w_grp_pool, v_b_grp_pool, v_scale_pool, v_w_ffn_up, v_w_ffn_down, v_g_norm_final):
    given = dict(x=x, c=c, ctx=ctx, c_ctx=c_ctx, w_ada=w_ada, b_ada=b_ada, g_norm_mix=g_norm_mix, g_norm_ffn=g_norm_ffn, w_in_even=w_in_even, w_out_even=w_out_even, lb_logits=lb_logits, g_hgrn_out=g_hgrn_out, w_spatial=w_spatial, b_spatial=b_spatial, g_spatial_v=g_spatial_v, w_in_pool=w_in_pool, w_grp_pool=w_grp_pool, b_grp_pool=b_grp_pool, scale_pool=scale_pool, w_ffn_up=w_ffn_up, w_ffn_down=w_ffn_down, g_norm_final=g_norm_final, loss_target=loss_target, m_c_ctx=m_c_ctx, m_w_ada=m_w_ada, m_b_ada=m_b_ada, m_g_norm_mix=m_g_norm_mix, m_g_norm_ffn=m_g_norm_ffn, m_w_in_even=m_w_in_even, m_w_out_even=m_w_out_even, m_lb_logits=m_lb_logits, m_g_hgrn_out=m_g_hgrn_out, m_w_spatial=m_w_spatial, m_b_spatial=m_b_spatial, m_g_spatial_v=m_g_spatial_v, m_w_in_pool=m_w_in_pool, m_w_grp_pool=m_w_grp_pool, m_b_grp_pool=m_b_grp_pool, m_scale_pool=m_scale_pool, m_w_ffn_up=m_w_ffn_up, m_w_ffn_down=m_w_ffn_down, m_g_norm_final=m_g_norm_final, v_c_ctx=v_c_ctx, v_w_ada=v_w_ada, v_b_ada=v_b_ada, v_g_norm_mix=v_g_norm_mix, v_g_norm_ffn=v_g_norm_ffn, v_w_in_even=v_w_in_even, v_w_out_even=v_w_out_even, v_lb_logits=v_lb_logits, v_g_hgrn_out=v_g_hgrn_out, v_w_spatial=v_w_spatial, v_b_spatial=v_b_spatial, v_g_spatial_v=v_g_spatial_v, v_w_in_pool=v_w_in_pool, v_w_grp_pool=v_w_grp_pool, v_b_grp_pool=v_b_grp_pool, v_scale_pool=v_scale_pool, v_w_ffn_up=v_w_ffn_up, v_w_ffn_down=v_w_ffn_down, v_g_norm_final=v_g_norm_final)
    weights = {n: given[n] for n in TWIN_WEIGHTS}
    shared = {n: given[n] for n in SHARED_INPUTS}
    per_example = {n: given[n] for n in ['x', 'c', 'ctx']}
    grad_fn = _jax.value_and_grad(_loss, argnums=(0, 1))

    def one_microbatch(ex, loss_target):
        ex = dict(ex)
        diff = ex.pop(TWIN_DIFF_INPUT)
        return grad_fn(weights, diff, {**shared, **ex}, loss_target)

    if N_MICROBATCH == 1:
        loss, (grad_w, grad_x) = one_microbatch(per_example, given["loss_target"])
    else:
        def body(carry, xs):
            loss_sum, grad_sum = carry
            l_k, (gw_k, gx_k) = one_microbatch(xs[0], xs[1])
            with _jax.named_scope("update"):
                return (loss_sum + l_k, _jax.tree.map(_jnp.add, grad_sum, gw_k)), gx_k

        init = (_jnp.zeros((), _jnp.float32), _jax.tree.map(_jnp.zeros_like, weights))
        (loss, grad_w), grad_x = _jax.lax.scan(body, init, (per_example, given["loss_target"]))
    with _jax.named_scope("update"):
        delta_w, new_m, new_v = {}, {}, {}
        for n in TWIN_WEIGHTS:
            delta_w[n], new_m[n], new_v[n] = _adamw(weights[n], grad_w[n], given["m_" + n], given["v_" + n])
    return (loss, grad_x, *[grad_w[n] for n in TWIN_WEIGHTS], *[delta_w[n] for n in TWIN_WEIGHTS],
            *[new_m[n] for n in TWIN_WEIGHTS], *[new_v[n] for n in TWIN_WEIGHTS])
```

```python
import functools

import jax
import jax.numpy as jnp
from jax import lax
from jax.experimental import pallas as pl
from jax.experimental.pallas import tpu as pltpu

F32 = jnp.float32
BF16 = jnp.bfloat16
MESH = pl.DeviceIdType.MESH
ANY = pl.BlockSpec(memory_space=pl.ANY)
HIGHEST = lax.Precision.HIGHEST

D = 2048
DFF = 8192
CTX = 256
TR = 256
GRID_W = 64
EPS = 1e-6
LOG_FLOOR = 1e-30
NH = 8
DK = 128
SUB = 16
B_CHUNK = 128
AW = NH * DK
POOL_WINDOWS = (2, 4, 8, 16)
NSH = 4
ADAM_LR, ADAM_B1, ADAM_B2, ADAM_EPS, ADAM_WD, ADAM_STEP = 0.001, 0.9, 0.999, 1e-08, 0.01, 10
MIB = 1024 * 1024


def _cp(vmem_mib):
    return pltpu.CompilerParams(vmem_limit_bytes=vmem_mib * MIB)


def _bf(v):
    return v.astype(BF16)


def _nn(a, b):
    return lax.dot_general(a, b, (((1,), (0,)), ((), ())), preferred_element_type=F32)


def _nt(a, b):
    return lax.dot_general(a, b, (((1,), (1,)), ((), ())), preferred_element_type=F32)


def _tn(a, b):
    return lax.dot_general(a, b, (((0,), (0,)), ((), ())), preferred_element_type=F32)


def _mesh_pos():
    return lax.axis_index("x"), lax.axis_index("y"), lax.axis_index("c")


def gather_chips(arrs, name):
    n = len(arrs)

    def body(*refs):
        ins, outs = refs[:n], refs[n:2 * n]
        send_sems, recv_sems, local_sems = refs[2 * n:]
        x, y, c = _mesh_pos()
        chips = [(1 - x, y), (x, 1 - y), (1 - x, 1 - y)]
        me = 2 * x + y
        local = []
        for t in range(n):
            loc = pltpu.make_async_copy(ins[t], outs[t].at[me], local_sems.at[t])
            loc.start()
            local.append(loc)
            for j, (px, py) in enumerate(chips):
                pltpu.make_async_remote_copy(
                    src_ref=ins[t], dst_ref=outs[t].at[me], send_sem=send_sems.at[3 * t + j],
                    recv_sem=recv_sems.at[3 * t + j], device_id=(px, py, c), device_id_type=MESH).start()
        for t in range(n):
            for j, (px, py) in enumerate(chips):
                cp = pltpu.make_async_remote_copy(
                    src_ref=ins[t], dst_ref=outs[t].at[2 * px + py], send_sem=send_sems.at[3 * t + j],
                    recv_sem=recv_sems.at[3 * t + j], device_id=(px, py, c), device_id_type=MESH)
                cp.wait_recv()
                cp.wait_send()
        for loc in local:
            loc.wait()

    return pl.pallas_call(
        body, name=name,
        out_shape=[jax.ShapeDtypeStruct((NSH,) + a.shape, a.dtype) for a in arrs],
        in_specs=[ANY] * n, out_specs=[ANY] * n,
        scratch_shapes=[pltpu.SemaphoreType.DMA((3 * n,)), pltpu.SemaphoreType.DMA((3 * n,)),
                        pltpu.SemaphoreType.DMA((n,))],
    )(*arrs)


def exchange_grads(bufs, name):
    n = len(bufs)

    def body(*refs):
        ins, outs = refs[:n], refs[n:2 * n]
        send_sems, recv_sems, local_sems = refs[2 * n:]
        x, y, c = _mesh_pos()
        sibling = (x, y, 1 - c)
        chips = [(1 - x, y), (x, 1 - y), (1 - x, 1 - y)]
        me = 2 * x + y

        def copy(t, k, src, slot, to):
            return pltpu.make_async_remote_copy(
                src_ref=src, dst_ref=outs[t].at[slot], send_sem=send_sems.at[7 * t + k],
                recv_sem=recv_sems.at[7 * t + k], device_id=to, device_id_type=MESH)

        local, sends = [], []
        for t in range(n):
            loc = pltpu.make_async_copy(ins[t].at[me], outs[t].at[2 * me + c], local_sems.at[t])
            loc.start()
            local.append(loc)
            first = [copy(t, 0, ins[t].at[me], 2 * me + c, sibling)]
            first += [copy(t, 1 + j, ins[t].at[2 * px + py], 2 * me + c, (px, py, c))
                      for j, (px, py) in enumerate(chips)]
            for cp in first:
                cp.start()
            sends += first
        for t in range(n):
            for j, (px, py) in enumerate(chips):
                slot = 2 * (2 * px + py) + c
                copy(t, 1 + j, ins[t].at[me], slot, (px, py, c)).wait_recv()
                fwd = copy(t, 4 + j, outs[t].at[slot], slot, sibling)
                fwd.start()
                sends.append(fwd)
        for t in range(n):
            copy(t, 0, ins[t].at[me], 2 * me + 1 - c, sibling).wait_recv()
            for j, (px, py) in enumerate(chips):
                copy(t, 4 + j, ins[t].at[me], 2 * (2 * px + py) + 1 - c, sibling).wait_recv()
        for cp in sends:
            cp.wait_send()
        for loc in local:
            loc.wait()

    return pl.pallas_call(
        body, name=name,
        out_shape=[jax.ShapeDtypeStruct((8,) + b.shape[1:], b.dtype) for b in bufs],
        in_specs=[ANY] * n, out_specs=[ANY] * n,
        scratch_shapes=[pltpu.SemaphoreType.DMA((7 * n,)), pltpu.SemaphoreType.DMA((7 * n,)),
                        pltpu.SemaphoreType.DMA((n,))],
    )(*bufs)


def allgather8(v, name):
    m, n = v.shape

    def body(x_ref, out_ref, send_sems, recv_sems, local_sem):
        x, y, c = _mesh_pos()
        me, sibling = (x, y, c), (x, y, 1 - c)
        chips = [(1 - x, y), (x, 1 - y), (1 - x, 1 - y)]

        def rows(px, py, pc):
            return out_ref.at[4 * px + 2 * py + pc]

        def copy(k, block, to, src=None):
            return pltpu.make_async_remote_copy(
                src_ref=rows(*block) if src is None else src, dst_ref=rows(*block),
                send_sem=send_sems.at[k], recv_sem=recv_sems.at[k], device_id=to, device_id_type=MESH)

        mine = pltpu.make_async_copy(x_ref, rows(*me), local_sem)
        mine.start()
        first = [copy(0, me, sibling, src=x_ref)]
        first += [copy(1 + j, me, (*chip, c), src=x_ref) for j, chip in enumerate(chips)]
        for cp in first:
            cp.start()
        passed = [copy(4 + j, (*chip, c), sibling) for j, chip in enumerate(chips)]
        for j, chip in enumerate(chips):
            copy(1 + j, (*chip, c), me).wait_recv()
            passed[j].start()
        copy(0, sibling, me).wait_recv()
        for j, chip in enumerate(chips):
            copy(4 + j, (*chip, 1 - c), me).wait_recv()
        for cp in first + passed:
            cp.wait_send()
        mine.wait()

    return pl.pallas_call(
        body, name=name,
        out_shape=jax.ShapeDtypeStruct((8, m, n), v.dtype),
        in_specs=[pl.BlockSpec(memory_space=pltpu.VMEM)],
        out_specs=pl.BlockSpec(memory_space=pltpu.VMEM),
        scratch_shapes=[pltpu.SemaphoreType.DMA((7,)), pltpu.SemaphoreType.DMA((7,)), pltpu.SemaphoreType.DMA],
        compiler_params=_cp(40),
    )(v)


def _row_spec(width=D, off=0):
    return pl.BlockSpec((TR, width), lambda i, off=off: (i, off))


def _vec_spec(width=D):
    return pl.BlockSpec((1, width), lambda i: (0, 0))


def _mod_spec():
    return pl.BlockSpec((None, 6, D), lambda i: (jnp.minimum(i, 1), 0, 0))


def _pair_spec(width=D):
    return pl.BlockSpec((None, 1, width), lambda i: (jnp.minimum(i, 1), 0, 0))


def _accum(ref, val, first):
    @pl.when(first)
    def _():
        ref[...] = val

    @pl.when(jnp.logical_not(first))
    def _():
        ref[...] += val


def norm_mod(xs, g, mod, si, name):
    r = xs.shape[0]

    def body(x_ref, g_ref, m_ref, o_ref):
        x = x_ref[...]
        rstd = lax.rsqrt(jnp.mean(x * x, axis=-1, keepdims=True) + EPS)
        n = x * rstd * g_ref[...]
        o_ref[...] = (n * (1.0 + m_ref[si + 1:si + 2, :]) + m_ref[si:si + 1, :]).astype(BF16)

    return pl.pallas_call(
        body, name=name, grid=(r // TR,),
        in_specs=[_row_spec(), _vec_spec(), _mod_spec()], out_specs=_row_spec(),
        out_shape=jax.ShapeDtypeStruct((r, D), BF16), compiler_params=_cp(32),
    )(xs, g, mod)


def gate_in(dx, f, mod, gi, name):
    r = dx.shape[0]

    def body(dx_ref, f_ref, m_ref, o_ref, dg_ref):
        i = pl.program_id(0)
        dxv = dx_ref[...]
        o_ref[...] = (dxv * m_ref[gi:gi + 1, :]).astype(BF16)
        _accum(dg_ref, jnp.sum(dxv * f_ref[...].astype(F32), axis=0, keepdims=True), i <= 1)

    return pl.pallas_call(
        body, name=name, grid=(r // TR,),
        in_specs=[_row_spec(), _row_spec(), _mod_spec()], out_specs=[_row_spec(), _pair_spec()],
        out_shape=[jax.ShapeDtypeStruct((r, D), BF16), jax.ShapeDtypeStruct((2, 1, D), F32)],
        compiler_params=_cp(32),
    )(dx, f, mod)


def gate_in_pool(dx, ypre, mod, scale, gi, name):
    r = dx.shape[0]

    def body(dx_ref, y_ref, m_ref, s_ref, o_ref, dg_ref, ds_ref, db_ref):
        i = pl.program_id(0)
        dxv = dx_ref[...]
        yp = y_ref[...].astype(F32)
        sc = s_ref[...]
        dy = dxv * m_ref[gi:gi + 1, :]
        dyp = dy * sc
        o_ref[...] = dyp.astype(BF16)
        _accum(dg_ref, jnp.sum(dxv * (yp * sc), axis=0, keepdims=True), i <= 1)
        _accum(ds_ref, jnp.sum(dy * yp, axis=0, keepdims=True), i == 0)
        _accum(db_ref, jnp.sum(dyp, axis=0, keepdims=True), i == 0)

    return pl.pallas_call(
        body, name=name, grid=(r // TR,),
        in_specs=[_row_spec(), _row_spec(), _mod_spec(), _vec_spec()],
        out_specs=[_row_spec(), _pair_spec(), _vec_spec(), _vec_spec()],
        out_shape=[jax.ShapeDtypeStruct((r, D), BF16), jax.ShapeDtypeStruct((2, 1, D), F32),
                   jax.ShapeDtypeStruct((1, D), F32), jax.ShapeDtypeStruct((1, D), F32)],
        compiler_params=_cp(32),
    )(dx, ypre, mod, scale)


def normmod_bwd(dh, x, dxo, g, mod, si, name):
    r = x.shape[0]

    def body(dh_ref, x_ref, dxo_ref, g_ref, m_ref, dx_ref, dsh_ref, dsc_ref, dg_ref):
        i = pl.program_id(0)
        xv = x_ref[...]
        dhv = dh_ref[...]
        gv = g_ref[...]
        rstd = lax.rsqrt(jnp.mean(xv * xv, axis=-1, keepdims=True) + EPS)
        xhat = xv * rstd
        dn = dhv * (1.0 + m_ref[si + 1:si + 2, :])
        dxh = dn * gv
        dx_ref[...] = rstd * (dxh - xhat * jnp.mean(dxh * xhat, axis=-1, keepdims=True)) + dxo_ref[...]
        _accum(dsh_ref, jnp.sum(dhv, axis=0, keepdims=True), i <= 1)
        _accum(dsc_ref, jnp.sum(dhv * (xhat * gv), axis=0, keepdims=True), i <= 1)
        _accum(dg_ref, jnp.sum(dn * xhat, axis=0, keepdims=True), i == 0)

    return pl.pallas_call(
        body, name=name, grid=(r // TR,),
        in_specs=[_row_spec(), _row_spec(), _row_spec(), _vec_spec(), _mod_spec()],
        out_specs=[_row_spec(), _pair_spec(), _pair_spec(), _vec_spec()],
        out_shape=[jax.ShapeDtypeStruct((r, D), F32), jax.ShapeDtypeStruct((2, 1, D), F32),
                   jax.ShapeDtypeStruct((2, 1, D), F32), jax.ShapeDtypeStruct((1, D), F32)],
        compiler_params=_cp(48),
    )(dh, x, dxo, g, mod)


def final_loss(xs, g, target, name):
    r = xs.shape[0]

    def body(x_ref, g_ref, t_ref, dx_ref, loss_ref, dg_ref):
        i = pl.program_id(0)

        @pl.when(i == 0)
        def _():
            dx_ref[...] = jnp.zeros_like(dx_ref)
            loss_ref[...] = jnp.zeros_like(loss_ref)
            dg_ref[...] = jnp.zeros_like(dg_ref)

        @pl.when(i > 0)
        def _():
            xv = x_ref[...]
            gv = g_ref[...]
            rstd = lax.rsqrt(jnp.mean(xv * xv, axis=-1, keepdims=True) + EPS)
            xhat = xv * rstd
            err = xhat * gv - t_ref[...]
            part = 0.5 * jnp.sum(jnp.mean(err * err, axis=-1, keepdims=True), axis=0, keepdims=True)
            lane = lax.broadcasted_iota(jnp.int32, (1, 128), 1)
            loss_ref[...] += jnp.where(lane == 0, part, 0.0)
            dy = err * (1.0 / D)
            dg_ref[...] += jnp.sum(dy * xhat, axis=0, keepdims=True)
            dxh = dy * gv
            dx_ref[...] = rstd * (dxh - xhat * jnp.mean(dxh * xhat, axis=-1, keepdims=True))

    return pl.pallas_call(
        body, name=name, grid=(r // TR,),
        in_specs=[_row_spec(), _vec_spec(), pl.BlockSpec((TR, D), lambda i: (jnp.maximum(i - 1, 0), 0))],
        out_specs=[_row_spec(), pl.BlockSpec((1, 128), lambda i: (0, 0)), _vec_spec()],
        out_shape=[jax.ShapeDtypeStruct((r, D), F32), jax.ShapeDtypeStruct((1, 128), F32),
                   jax.ShapeDtypeStruct((1, D), F32)],
        compiler_params=_cp(32),
    )(xs, g, target)


def _mm(name, mode, a, b, grid, a_spec, b_spec, out_shapes, out_specs, epi, kaxis=None, acc=None,
        extras=(), dst=None, vmem=48):
    ne, no = len(extras), len(out_shapes)
    dot = {"nn": _nn, "nt": _nt, "tn": _tn}[mode]
    nk = grid[kaxis] if kaxis is not None else 1
    nd = 0 if dst is None else 1

    def body(*refs):
        a_ref, b_ref = refs[0], refs[1]
        ex = refs[2:2 + ne]
        outs = refs[2 + ne + nd:2 + ne + nd + no]
        ids = [pl.program_id(ax) for ax in range(len(grid))]
        part = dot(a_ref[...], b_ref[...])
        if kaxis is None:
            epi(part, ids, ex, outs)
        else:
            acc_ref = refs[-1]
            k = ids[kaxis]

            @pl.when(k == 0)
            def _():
                acc_ref[...] = part

            @pl.when(k > 0)
            def _():
                acc_ref[...] += part

            @pl.when(k == nk - 1)
            def _():
                epi(acc_ref[...], ids, ex, outs)

    operands = [a, b] + [e[0] for e in extras]
    in_specs = [a_spec, b_spec] + [e[1] for e in extras]
    aliases = {}
    if dst is not None:
        operands.append(dst)
        in_specs.append(ANY)
        aliases = {len(operands) - 1: 0}
    return pl.pallas_call(
        body, name=name, grid=grid, in_specs=in_specs, out_specs=out_specs, out_shape=out_shapes,
        scratch_shapes=[] if kaxis is None else [pltpu.VMEM(acc, F32)],
        input_output_aliases=aliases, compiler_params=_cp(vmem),
    )(*operands)


def _epi_store(acc, ids, ex, outs):
    outs[0][...] = acc.astype(outs[0].dtype)


def _epi_relu2(acc, ids, ex, outs):
    rl = jnp.maximum(acc, 0.0)
    outs[0][...] = (rl * rl).astype(BF16)


def _epi_2sqrt(acc, ids, ex, outs):
    outs[0][...] = (acc * (2.0 * jnp.sqrt(ex[0][...].astype(F32)))).astype(BF16)


def _gate_rows(ids, tm, shape, mod_ref, gi):
    rid = ids[0] * tm + lax.broadcasted_iota(jnp.int32, shape, 0)
    return jnp.where(rid < CTX, mod_ref[0, gi:gi + 1, :], mod_ref[1, gi:gi + 1, :])


def _epi_res(gi, tm):
    def epi(acc, ids, ex, outs):
        outs[0][...] = ex[0][...] + _gate_rows(ids, tm, acc.shape, ex[1], gi) * acc
        outs[1][...] = acc.astype(BF16)
    return epi


def _epi_pool(gi, tm):
    def epi(acc, ids, ex, outs):
        ypre = acc + ex[2][...]
        outs[0][...] = ex[0][...] + _gate_rows(ids, tm, acc.shape, ex[1], gi) * (ypre * ex[3][...])
        outs[1][...] = ypre.astype(BF16)
    return epi


def _tm(r):
    return 768 if r % 768 == 0 else TR


def mm_cols(a, w, l, name, epi=_epi_store, out_dtype=F32, per=2):
    r, k = a.shape
    c = w.shape[3]
    tm, tn = _tm(r), c // per
    return _mm(name, "nn", a, w, (r // tm, NSH * per),
               pl.BlockSpec((tm, k), lambda i, j: (i, 0)),
               pl.BlockSpec((None, None, k, tn), lambda i, j: (j // per, l, 0, j % per)),
               [jax.ShapeDtypeStruct((r, NSH * c), out_dtype)], [pl.BlockSpec((tm, tn), lambda i, j: (i, j))],
               epi, vmem=56)[0]


def mm_rows_res(a, w, l, res, mod, gi, name):
    r = a.shape[0]
    kc = w.shape[2]
    tm, tn = _tm(r), 1024
    return _mm(name, "nn", a, w, (r // tm, D // tn, NSH),
               pl.BlockSpec((tm, kc), lambda i, j, k: (i, k)),
               pl.BlockSpec((None, None, kc, tn), lambda i, j, k: (k, l, 0, j)),
               [jax.ShapeDtypeStruct((r, D), F32), jax.ShapeDtypeStruct((r, D), BF16)],
               [pl.BlockSpec((tm, tn), lambda i, j, k: (i, j))] * 2,
               _epi_res(gi, tm), kaxis=2, acc=(tm, tn),
               extras=[(res, pl.BlockSpec((tm, tn), lambda i, j, k: (i, j))),
                       (mod, pl.BlockSpec((2, 6, tn), lambda i, j, k: (0, 0, j)))], vmem=56)


def mm_rows(a, w, l, name):
    r = a.shape[0]
    kc = w.shape[2]
    tm, tn = _tm(r), 1024
    return _mm(name, "nn", a, w, (r // tm, D // tn, NSH),
               pl.BlockSpec((tm, kc), lambda i, j, k: (i, k)),
               pl.BlockSpec((None, None, kc, tn), lambda i, j, k: (k, l, 0, j)),
               [jax.ShapeDtypeStruct((r, D), F32)], [pl.BlockSpec((tm, tn), lambda i, j, k: (i, j))],
               _epi_store, kaxis=2, acc=(tm, tn), vmem=56)[0]


def mm_grp_res(z, w, o, bias, scale, res, mod, gi, name):
    r = z.shape[0]
    tm = _tm(r)
    return _mm(name, "nn", z, w, (r // tm, 4, NSH),
               pl.BlockSpec((tm, 128), lambda i, g, k: (i, 4 * g + k)),
               pl.BlockSpec((None, None, None, 128, 512), lambda i, g, k: (k, o, g, 0, 0)),
               [jax.ShapeDtypeStruct((r, D), F32), jax.ShapeDtypeStruct((r, D), BF16)],
               [pl.BlockSpec((tm, 512), lambda i, g, k: (i, g))] * 2,
               _epi_pool(gi, tm), kaxis=2, acc=(tm, 512),
               extras=[(res, pl.BlockSpec((tm, 512), lambda i, g, k: (i, g))),
                       (mod, pl.BlockSpec((2, 6, 512), lambda i, g, k: (0, 0, g))),
                       (bias, pl.BlockSpec((1, 512), lambda i, g, k: (0, g))),
                       (scale, pl.BlockSpec((1, 512), lambda i, g, k: (0, g)))], vmem=48)


def mm_t_rows(a, w, l, name, epi=_epi_store, out_dtype=F32, extras_of=None, per=1):
    r, n = a.shape
    kc = w.shape[2]
    tm, tn = _tm(r), kc // per
    extras = []
    if extras_of is not None:
        extras = [(extras_of, pl.BlockSpec((tm, tn), lambda i, j: (i, j)))]
    return _mm(name, "nt", a, w, (r // tm, NSH * per),
               pl.BlockSpec((tm, n), lambda i, j: (i, 0)),
               pl.BlockSpec((None, None, tn, n), lambda i, j: (j // per, l, j % per, 0)),
               [jax.ShapeDtypeStruct((r, NSH * kc), out_dtype)], [pl.BlockSpec((tm, tn), lambda i, j: (i, j))],
               epi, extras=extras, vmem=56)[0]


def mm_t_cols(a, w, l, name):
    r = a.shape[0]
    k, c = w.shape[2], w.shape[3]
    tm, tn = _tm(r), 1024
    return _mm(name, "nt", a, w, (r // tm, k // tn, NSH),
               pl.BlockSpec((tm, c), lambda i, j, s: (i, s)),
               pl.BlockSpec((None, None, tn, c), lambda i, j, s: (s, l, j, 0)),
               [jax.ShapeDtypeStruct((r, k), F32)], [pl.BlockSpec((tm, tn), lambda i, j, s: (i, j))],
               _epi_store, kaxis=2, acc=(tm, tn), vmem=56)[0]


def mm_t_grp(dy, w, o, name):
    r = dy.shape[0]
    tm = _tm(r)
    return _mm(name, "nt", dy, w, (r // tm, 4, NSH),
               pl.BlockSpec((tm, 512), lambda i, g, s: (i, g)),
               pl.BlockSpec((None, None, None, 128, 512), lambda i, g, s: (s, o, g, 0, 0)),
               [jax.ShapeDtypeStruct((r, D), F32)], [pl.BlockSpec((tm, 128), lambda i, g, s: (i, 4 * g + s))],
               _epi_store, vmem=40)[0]


def grad_cols(a, b, dst, l, name, ta=1024, per=2):
    r, k = a.shape
    c = dst.shape[3]
    tk, tn = _tm(r), c // per
    return _mm(name, "tn", a, b, (NSH, k // ta, per, r // tk),
               pl.BlockSpec((tk, ta), lambda s, i, j, t: (t, i)),
               pl.BlockSpec((tk, tn), lambda s, i, j, t: (t, s * per + j)),
               [jax.ShapeDtypeStruct(dst.shape, BF16)],
               [pl.BlockSpec((None, None, ta, tn), lambda s, i, j, t: (s, l, i, j))],
               _epi_store, kaxis=3, acc=(ta, tn), dst=dst, vmem=56)[0]


def grad_rows(a, b, dst, l, name, per=1, tn=1024):
    r = a.shape[0]
    kc, n = dst.shape[2], dst.shape[3]
    tk, ta = _tm(r), kc // per
    return _mm(name, "tn", a, b, (NSH, per, n // tn, r // tk),
               pl.BlockSpec((tk, ta), lambda s, i, j, t: (t, s * per + i)),
               pl.BlockSpec((tk, tn), lambda s, i, j, t: (t, j)),
               [jax.ShapeDtypeStruct(dst.shape, BF16)],
               [pl.BlockSpec((None, None, ta, tn), lambda s, i, j, t: (s, l, i, j))],
               _epi_store, kaxis=3, acc=(ta, tn), dst=dst, vmem=56)[0]


def grad_grp(z, dy, dst, o, name):
    r = z.shape[0]
    tk = _tm(r)
    return _mm(name, "tn", z, dy, (NSH, 4, r // tk),
               pl.BlockSpec((tk, 128), lambda s, g, t: (t, 4 * g + s)),
               pl.BlockSpec((tk, 512), lambda s, g, t: (t, g)),
               [jax.ShapeDtypeStruct(dst.shape, BF16)],
               [pl.BlockSpec((None, None, None, 128, 512), lambda s, g, t: (s, o, g, 0, 0))],
               _epi_store, kaxis=2, acc=(128, 512), dst=dst, vmem=40)[0]


def _scan_tile(reverse, nt):
    if reverse:
        return lambda p: jnp.where(p == 0, 0, nt - p)
    return lambda p: p


def _gates(f, lbv):
    sg = jax.nn.sigmoid(f)
    fg = lbv + (1.0 - lbv) * sg
    g = jnp.log(jnp.maximum(fg, LOG_FLOOR))
    kk = (1.0 - lbv) * jax.nn.sigmoid(-f)
    return sg, fg, g, kk


def _chunk_cumsum(g, reverse):
    n = g.shape[0]
    rr = lax.broadcasted_iota(jnp.int32, (n, n), 0)
    cc = lax.broadcasted_iota(jnp.int32, (n, n), 1)
    inside = (rr // SUB) == (cc // SUB)
    tri = jnp.where(inside & ((cc >= rr) if reverse else (cc <= rr)), 1.0, 0.0).astype(F32)
    return jnp.dot(tri, g, precision=HIGHEST, preferred_element_type=F32)


def _decay(b, s, rows, reverse):
    dec = jnp.exp(jnp.minimum(b - b[s:s + 1], 0.0))
    return jnp.where((rows <= s) if reverse else (rows >= s), dec, 0.0)


def hgrn_fwd(p, lb, reverse, name):
    r = p.shape[0]
    nt = r // TR
    nsub = TR // SUB
    fcol = 2 if reverse else 1
    tile = _scan_tile(reverse, nt)

    def body(q_ref, f_ref, v_ref, lb_ref, o_ref, sin_ref, st, k_s, b_s):
        i = pl.program_id(0)

        @pl.when(i == 0)
        def _():
            st[...] = jnp.zeros_like(st)

        sin_ref[...] = st[...]
        _, _, g, kk = _gates(f_ref[...], lb_ref[...])
        k_s[...] = kk
        b_s[...] = _chunk_cumsum(g, reverse)
        rows = lax.broadcasted_iota(jnp.int32, (SUB, DK), 0)

        def sub(jj, carry):
            j = (nsub - 1 - jj) if reverse else jj
            rs = pl.ds(pl.multiple_of(j * SUB, SUB), SUB)
            for h in range(NH):
                sl = slice(h * DK, (h + 1) * DK)
                q, k, b, v = q_ref[rs, sl], k_s[rs, sl], b_s[rs, sl], v_ref[rs, sl]
                btot = b[0:1] if reverse else b[SUB - 1:SUB]
                o = _nt(_bf(q * jnp.exp(b)), _bf(st[h]))
                for s in range(SUB):
                    col = jnp.sum(q * k[s:s + 1] * _decay(b, s, rows, reverse), axis=-1, keepdims=True)
                    o = o + col * v[s:s + 1]
                o_ref[rs, sl] = o
                st[h] = st[h] * jnp.exp(btot) + _tn(_bf(v), _bf(k * jnp.exp(btot - b)))
            return carry

        lax.fori_loop(0, nsub, sub, 0)

    seg = lambda col: pl.BlockSpec((TR, AW), lambda i, col=col: (tile(i), col))
    return pl.pallas_call(
        body, name=name, grid=(nt,),
        in_specs=[seg(0), seg(fcol), seg(3), _vec_spec(AW)],
        out_specs=[pl.BlockSpec((TR, AW), lambda i: (tile(i), 0)),
                   pl.BlockSpec((None, NH, DK, DK), lambda i: (tile(i), 0, 0, 0))],
        out_shape=[jax.ShapeDtypeStruct((r, AW), F32), jax.ShapeDtypeStruct((nt, NH, DK, DK), F32)],
        scratch_shapes=[pltpu.VMEM((NH, DK, DK), F32), pltpu.VMEM((TR, AW), F32), pltpu.VMEM((TR, AW), F32)],
        compiler_params=_cp(40),
    )(p, p, p, lb)


def hgrn_bwd(p, do, sin, lb, reverse, name, add=None):
    r = p.shape[0]
    nt = r // TR
    nsub = TR // SUB
    fcol = 2 if reverse else 1
    tile0 = _scan_tile(reverse, nt)
    tile = lambda i: tile0(nt - 1 - i)
    nadd = 0 if add is None else 2
    out_dt = F32 if add is None else BF16

    def body(*refs):
        q_ref, f_ref, v_ref, do_ref, sin_ref, lb_ref = refs[:6]
        adds = refs[6:6 + nadd]
        dq_ref, dv_ref, df_ref, dlb_ref = refs[6 + nadd:10 + nadd]
        dst, srun, ssub, k_s, b_s, sg_s, fg_s = refs[10 + nadd:]
        i = pl.program_id(0)

        @pl.when(i == 0)
        def _():
            dst[...] = jnp.zeros_like(dst)
            dlb_ref[...] = jnp.zeros_like(dlb_ref)

        lbv = lb_ref[...]
        sg, fg, g, kk = _gates(f_ref[...], lbv)
        k_s[...] = kk
        sg_s[...] = sg
        fg_s[...] = fg
        b_s[...] = _chunk_cumsum(g, reverse)
        srun[...] = sin_ref[...]
        rows = lax.broadcasted_iota(jnp.int32, (SUB, DK), 0)
        r16 = lax.broadcasted_iota(jnp.int32, (SUB, SUB), 0)
        c16 = lax.broadcasted_iota(jnp.int32, (SUB, SUB), 1)
        later = jnp.where((c16 <= r16) if reverse else (c16 >= r16), 1.0, 0.0).astype(F32)

        def recompute(jj, c):
            j = (nsub - 1 - jj) if reverse else jj
            rs = pl.ds(pl.multiple_of(j * SUB, SUB), SUB)
            for h in range(NH):
                sl = slice(h * DK, (h + 1) * DK)
                k, b, v = k_s[rs, sl], b_s[rs, sl], v_ref[rs, sl]
                btot = b[0:1] if reverse else b[SUB - 1:SUB]
                ssub[jj, h] = srun[h]
                srun[h] = srun[h] * jnp.exp(btot) + _tn(_bf(v), _bf(k * jnp.exp(btot - b)))
            return c

        lax.fori_loop(0, nsub, recompute, 0)
        for h in range(NH):
            ssub[nsub, h] = srun[h]

        def back(jj, c):
            pos = nsub - 1 - jj
            j = jj if reverse else pos
            rs = pl.ds(pl.multiple_of(j * SUB, SUB), SUB)
            for h in range(NH):
                sl = slice(h * DK, (h + 1) * DK)
                q, k, b, v, dov = q_ref[rs, sl], k_s[rs, sl], b_s[rs, sl], v_ref[rs, sl], do_ref[rs, sl]
                btot = b[0:1] if reverse else b[SUB - 1:SUB]
                s0 = ssub[pos, h]
                ds = dst[h]
                dg_next = jnp.sum(ds * ssub[pos + 1, h], axis=0, keepdims=True)
                eb = jnp.exp(b)
                ebt = jnp.exp(btot - b)
                ke = k * ebt
                dq = _nn(_bf(dov), _bf(s0)) * eb
                dk = _nn(_bf(v), _bf(ds)) * ebt
                dv = _nt(_bf(ke), _bf(ds))
                for s in range(SUB):
                    dec = _decay(b, s, rows, reverse)
                    dsc = jnp.sum(dov * v[s:s + 1], axis=-1, keepdims=True)
                    qd = q * dec
                    dq = dq + (dsc * dec) * k[s:s + 1]
                    dk_row = jnp.sum(dsc * qd, axis=0, keepdims=True)
                    sc = jnp.sum(qd * k[s:s + 1], axis=-1, keepdims=True)
                    dv_row = jnp.sum(sc * dov, axis=0, keepdims=True)
                    dk = dk + jnp.where(rows == s, dk_row, 0.0)
                    dv = dv + jnp.where(rows == s, dv_row, 0.0)
                dst[h] = ds * jnp.exp(btot) + _tn(_bf(dov), _bf(q * eb))
                dg = jnp.dot(later, q * dq - k * dk, precision=HIGHEST, preferred_element_type=F32) + dg_next
                sgv, fgv, lbh = sg_s[rs, sl], fg_s[rs, sl], lbv[:, sl]
                dfg = jnp.where(fgv > LOG_FLOOR, dg / fgv, 0.0)
                df_ref[rs, sl] = ((1.0 - lbh) * sgv * (1.0 - sgv) * (dfg - dk)).astype(BF16)
                dlb_ref[:, sl] += jnp.sum((dfg - dk) * (1.0 - sgv), axis=0, keepdims=True)
                if add is None:
                    dq_ref[rs, sl] = dq
                    dv_ref[rs, sl] = dv
                else:
                    dq_ref[rs, sl] = (dq + adds[0][rs, sl]).astype(BF16)
                    dv_ref[rs, sl] = (dv + adds[1][rs, sl]).astype(BF16)
            return c

        lax.fori_loop(0, nsub, back, 0)

    seg = lambda col: pl.BlockSpec((TR, AW), lambda i, col=col: (tile(i), col))
    plain = pl.BlockSpec((TR, AW), lambda i: (tile(i), 0))
    operands = [p, p, p, do, sin, lb]
    in_specs = [seg(0), seg(fcol), seg(3), plain,
                pl.BlockSpec((None, NH, DK, DK), lambda i: (tile(i), 0, 0, 0)), _vec_spec(AW)]
    if add is not None:
        operands += list(add)
        in_specs += [plain, plain]
    return pl.pallas_call(
        body, name=name, grid=(nt,), in_specs=in_specs,
        out_specs=[plain, plain, plain, _vec_spec(AW)],
        out_shape=[jax.ShapeDtypeStruct((r, AW), out_dt), jax.ShapeDtypeStruct((r, AW), out_dt),
                   jax.ShapeDtypeStruct((r, AW), BF16), jax.ShapeDtypeStruct((1, AW), F32)],
        scratch_shapes=[pltpu.VMEM((NH, DK, DK), F32), pltpu.VMEM((NH, DK, DK), F32),
                        pltpu.VMEM((nsub + 1, NH, DK, DK), F32)]
        + [pltpu.VMEM((TR, AW), F32)] * 4,
        compiler_params=_cp(56),
    )(*operands)


def _silu(v):
    return v * jax.nn.sigmoid(v)


def readout_fwd(of, ob, p, ng, name):
    r = of.shape[0]

    def body(of_ref, ob_ref, g_ref, ng_ref, y_ref):
        for h in range(NH):
            sl = slice(h * DK, (h + 1) * DK)
            o = of_ref[:, sl] + ob_ref[:, sl]
            on = o * lax.rsqrt(jnp.mean(o * o, axis=-1, keepdims=True) + EPS) * ng_ref[:, sl]
            y_ref[:, sl] = (on * _silu(g_ref[:, sl])).astype(BF16)

    return pl.pallas_call(
        body, name=name, grid=(r // TR,),
        in_specs=[_row_spec(AW), _row_spec(AW), _row_spec(AW, 4), _vec_spec(AW)], out_specs=_row_spec(AW),
        out_shape=jax.ShapeDtypeStruct((r, AW), BF16), compiler_params=_cp(32),
    )(of, ob, p, ng)


def readout_bwd(dy, of, ob, p, ng, name):
    r = of.shape[0]

    def body(dy_ref, of_ref, ob_ref, g_ref, ng_ref, do_ref, dg_ref, dn_ref):
        i = pl.program_id(0)

        @pl.when(i == 0)
        def _():
            dn_ref[...] = jnp.zeros_like(dn_ref)

        for h in range(NH):
            sl = slice(h * DK, (h + 1) * DK)
            o = of_ref[:, sl] + ob_ref[:, sl]
            rstd = lax.rsqrt(jnp.mean(o * o, axis=-1, keepdims=True) + EPS)
            oh = o * rstd
            gv = g_ref[:, sl]
            sig = jax.nn.sigmoid(gv)
            dyv = dy_ref[:, sl]
            don = dyv * (gv * sig)
            dg_ref[:, sl] = (dyv * (oh * ng_ref[:, sl]) * (sig * (1.0 + gv * (1.0 - sig)))).astype(BF16)
            dn_ref[:, sl] += jnp.sum(don * oh, axis=0, keepdims=True)
            doh = don * ng_ref[:, sl]
            do_ref[:, sl] = rstd * (doh - oh * jnp.mean(doh * oh, axis=-1, keepdims=True))

    return pl.pallas_call(
        body, name=name, grid=(r // TR,),
        in_specs=[_row_spec(AW), _row_spec(AW), _row_spec(AW), _row_spec(AW, 4), _vec_spec(AW)],
        out_specs=[_row_spec(AW), _row_spec(AW), _vec_spec(AW)],
        out_shape=[jax.ShapeDtypeStruct((r, AW), F32), jax.ShapeDtypeStruct((r, AW), BF16),
                   jax.ShapeDtypeStruct((1, AW), F32)],
        compiler_params=_cp(32),
    )(dy, of, ob, p, ng)


def _gelu(v):
    return 0.5 * v * (1.0 + lax.erf(v * 0.7071067811865476))


def _gelu_grad(v):
    return 0.5 * (1.0 + lax.erf(v * 0.7071067811865476)) + v * (0.3989422804014327 * jnp.exp(-0.5 * v * v))


def _cmlp_norm(vv, gn):
    vg = _gelu(vv)
    mu = jnp.mean(vg, axis=-1, keepdims=True)
    cen = vg - mu
    rstd = lax.rsqrt(jnp.mean(cen * cen, axis=-1, keepdims=True) + EPS)
    xhat = cen * rstd
    return xhat, rstd, xhat * gn


def chunkmlp_fwd(p, ws, bias, gn, name):
    r = p.shape[0]

    def body(u_ref, v_ref, ws_ref, b_ref, gn_ref, y_ref):
        for ci in range(TR // B_CHUNK):
            rs = slice(ci * B_CHUNK, (ci + 1) * B_CHUNK)
            for gidx in range(NH):
                sl = slice(gidx * DK, (gidx + 1) * DK)
                _, _, vn = _cmlp_norm(v_ref[rs, sl], gn_ref[:, sl])
                mixed = _nn(_bf(ws_ref[gidx]), _bf(vn)) + b_ref[gidx]
                y_ref[rs, sl] = (_gelu(u_ref[rs, sl]) * mixed).astype(BF16)

    return pl.pallas_call(
        body, name=name, grid=(r // TR,),
        in_specs=[_row_spec(AW, 5), _row_spec(AW, 6), pl.BlockSpec((NH, B_CHUNK, B_CHUNK), lambda i: (0, 0, 0)),
                  pl.BlockSpec((NH, B_CHUNK, 1), lambda i: (0, 0, 0)), _vec_spec(AW)],
        out_specs=_row_spec(AW), out_shape=jax.ShapeDtypeStruct((r, AW), BF16), compiler_params=_cp(32),
    )(p, p, ws, bias, gn)


def chunkmlp_bwd(dy, p, ws, bias, gn, name):
    r = p.shape[0]

    def body(dy_ref, u_ref, v_ref, ws_ref, b_ref, gn_ref, du_ref, dv_ref, dws_ref, db_ref, dgn_ref):
        i = pl.program_id(0)

        @pl.when(i == 0)
        def _():
            dws_ref[...] = jnp.zeros_like(dws_ref)
            db_ref[...] = jnp.zeros_like(db_ref)
            dgn_ref[...] = jnp.zeros_like(dgn_ref)

        for ci in range(TR // B_CHUNK):
            rs = slice(ci * B_CHUNK, (ci + 1) * B_CHUNK)
            for gidx in range(NH):
                sl = slice(gidx * DK, (gidx + 1) * DK)
                vv, uv, dyv, gnv = v_ref[rs, sl], u_ref[rs, sl], dy_ref[rs, sl], gn_ref[:, sl]
                xhat, rstd, vn = _cmlp_norm(vv, gnv)
                wg = _bf(ws_ref[gidx])
                mixed = _nn(wg, _bf(vn)) + b_ref[gidx]
                dmixed = dyv * _gelu(uv)
                du_ref[rs, sl] = (dyv * mixed * _gelu_grad(uv)).astype(BF16)
                dws_ref[gidx] += _nt(_bf(dmixed), _bf(vn))
                db_ref[gidx] += jnp.sum(dmixed, axis=-1, keepdims=True)
                dvn = _tn(wg, _bf(dmixed))
                dgn_ref[:, sl] += jnp.sum(dvn * xhat, axis=0, keepdims=True)
                dxh = dvn * gnv
                dvg = rstd * (dxh - jnp.mean(dxh, axis=-1, keepdims=True)
                              - xhat * jnp.mean(dxh * xhat, axis=-1, keepdims=True))
                dv_ref[rs, sl] = (dvg * _gelu_grad(vv)).astype(BF16)

    return pl.pallas_call(
        body, name=name, grid=(r // TR,),
        in_specs=[_row_spec(AW, 1), _row_spec(AW, 5), _row_spec(AW, 6),
                  pl.BlockSpec((NH, B_CHUNK, B_CHUNK), lambda i: (0, 0, 0)),
                  pl.BlockSpec((NH, B_CHUNK, 1), lambda i: (0, 0, 0)), _vec_spec(AW)],
        out_specs=[_row_spec(AW), _row_spec(AW), pl.BlockSpec((NH, B_CHUNK, B_CHUNK), lambda i: (0, 0, 0)),
                   pl.BlockSpec((NH, B_CHUNK, 1), lambda i: (0, 0, 0)), _vec_spec(AW)],
        out_shape=[jax.ShapeDtypeStruct((r, AW), BF16), jax.ShapeDtypeStruct((r, AW), BF16),
                   jax.ShapeDtypeStruct((NH, B_CHUNK, B_CHUNK), F32), jax.ShapeDtypeStruct((NH, B_CHUNK, 1), F32),
                   jax.ShapeDtypeStruct((1, AW), F32)],
        compiler_params=_cp(32),
    )(dy, p, p, ws, bias, gn)


def _win_count(pos, k, n):
    lo = jnp.maximum(pos - k // 2, 0)
    hi = jnp.minimum(pos - k // 2 + k, n)
    return (hi - lo).astype(F32)


def _box(src, pos, n_pos, stride, offs):
    n = src.shape[0]
    acc = jnp.zeros_like(src)
    for d in offs:
        sh = src if d == 0 else pltpu.roll(src, (-d * stride) % n, 0)
        acc = acc + jnp.where((pos + d >= 0) & (pos + d < n_pos), sh, 0.0)
    return acc


def pool_op(p, transpose, name):
    r = p.shape[0]
    seq = r - CTX
    grows = seq // GRID_W

    def body(x_ref, o_ref):
        j = pl.program_id(0)
        for gi, k in enumerate(POOL_WINDOWS):
            @pl.when(j // 4 == gi)
            def _(k=k):
                offs = range(-(k // 2) + 1, k // 2 + 1) if transpose else range(-(k // 2), k // 2)
                lat = x_ref[CTX:, :]
                t = lax.broadcasted_iota(jnp.int32, (seq, 128), 0)
                col, row = t & (GRID_W - 1), t >> 6
                cnt = _win_count(col, k, GRID_W) * _win_count(row, k, grows)
                src = lat / cnt if transpose else lat
                acc = _box(_box(src, col, GRID_W, 1, offs), row, grows, GRID_W, offs)
                o_ref[CTX:, :] = ((acc if transpose else acc / cnt) - lat).astype(BF16)
                cx = x_ref[:CTX, :]
                tc = lax.broadcasted_iota(jnp.int32, (CTX, 128), 0)
                cntc = _win_count(tc, k, CTX)
                srcc = cx / cntc if transpose else cx
                accc = _box(srcc, tc, CTX, 1, offs)
                o_ref[:CTX, :] = ((accc if transpose else accc / cntc) - cx).astype(BF16)

    return pl.pallas_call(
        body, name=name, grid=(D // 128,),
        in_specs=[pl.BlockSpec((r, 128), lambda j: (0, j))], out_specs=pl.BlockSpec((r, 128), lambda j: (0, j)),
        out_shape=jax.ShapeDtypeStruct((r, D), BF16), compiler_params=_cp(56),
    )(p)


def ada_mods(cs, w_ada, b_loc, name):
    nl, _, cl = w_ada.shape
    tn = 1024

    def body(c_ref, w_ref, b_ref, o_ref):
        o_ref[...] = _nn(_bf(_silu(c_ref[...])), _bf(w_ref[...])) + b_ref[...]

    return pl.pallas_call(
        body, name=name, grid=(nl, cl // tn),
        in_specs=[pl.BlockSpec((16, D), lambda l, j: (0, 0)), pl.BlockSpec((None, D, tn), lambda l, j: (l, 0, j)),
                  pl.BlockSpec((None, 1, tn), lambda l, j: (l, 0, j))],
        out_specs=pl.BlockSpec((None, 16, tn), lambda l, j: (l, 0, j)),
        out_shape=jax.ShapeDtypeStruct((nl, 16, cl), F32), compiler_params=_cp(40),
    )(cs, w_ada, b_loc)


def _adamw(w, g, m, v):
    m = ADAM_B1 * m + (1.0 - ADAM_B1) * g
    v = ADAM_B2 * v + (1.0 - ADAM_B2) * (g * g)
    m_hat = m / (1.0 - ADAM_B1 ** ADAM_STEP)
    v_hat = v / (1.0 - ADAM_B2 ** ADAM_STEP)
    delta = -ADAM_LR * (m_hat / (jnp.sqrt(v_hat) + ADAM_EPS) + ADAM_WD * w)
    return delta, m, v


def ada_update(cs, dm, w, m, v, name):
    nl, _, cl = w.shape
    ta, tn = 256, 1024

    def body(c_ref, dm_ref, w_ref, m_ref, v_ref, g_ref, d_ref, nm_ref, nv_ref, dc_ref):
        l, j = pl.program_id(1), pl.program_id(2)
        a = _bf(_silu(c_ref[...]))
        bmat = _bf(dm_ref[...])
        wv = w_ref[...]
        g = _tn(a, bmat)
        g_ref[...] = g
        d_ref[...], nm_ref[...], nv_ref[...] = _adamw(wv, g, m_ref[...], v_ref[...])
        _accum(dc_ref, _nt(bmat, _bf(wv)), (l == 0) & (j == 0))

    wspec = pl.BlockSpec((None, ta, tn), lambda i, l, j: (l, i, j))
    shp = jax.ShapeDtypeStruct(w.shape, F32)
    return pl.pallas_call(
        body, name=name, grid=(D // ta, nl, cl // tn),
        in_specs=[pl.BlockSpec((16, ta), lambda i, l, j: (0, i)),
                  pl.BlockSpec((None, 16, tn), lambda i, l, j: (l, 0, j)), wspec, wspec, wspec],
        out_specs=[wspec, wspec, wspec, wspec, pl.BlockSpec((16, ta), lambda i, l, j: (0, i))],
        out_shape=[shp, shp, shp, shp, jax.ShapeDtypeStruct((16, D), F32)], compiler_params=_cp(40),
    )(cs, dm, w, m, v)


def adamw_big(rb, w, m, v, name):
    shape = w.shape
    cols = shape[-1]
    rows = w.size // cols
    tr = 128
    w2, m2, v2 = (t.reshape(rows, cols) for t in (w, m, v))
    rb2 = rb.reshape(8, rows, cols)

    def body(rb_ref, w_ref, m_ref, v_ref, g_ref, d_ref, nm_ref, nv_ref):
        g = rb_ref[0].astype(F32)
        for dev in range(1, 8):
            g = g + rb_ref[dev].astype(F32)
        g_ref[...] = g
        d_ref[...], nm_ref[...], nv_ref[...] = _adamw(w_ref[...], g, m_ref[...], v_ref[...])

    spec = pl.BlockSpec((tr, cols), lambda i: (i, 0))
    shp = jax.ShapeDtypeStruct((rows, cols), F32)
    outs = pl.pallas_call(
        body, name=name, grid=(rows // tr,),
        in_specs=[pl.BlockSpec((8, tr, cols), lambda i: (0, i, 0)), spec, spec, spec],
        out_specs=[spec] * 4, out_shape=[shp] * 4, compiler_params=_cp(48),
    )(rb2, w2, m2, v2)
    return [o.reshape(shape) for o in outs]


def sum8(g8, name):
    n = g8.shape[1]

    def body(g_ref, o_ref):
        acc = g_ref[0]
        for dev in range(1, 8):
            acc = acc + g_ref[dev]
        o_ref[...] = acc

    return pl.pallas_call(
        body, name=name, grid=(1,), in_specs=[pl.BlockSpec((8, n, 128), lambda i: (0, 0, 0))],
        out_specs=pl.BlockSpec((n, 128), lambda i: (0, 0)),
        out_shape=jax.ShapeDtypeStruct((n, 128), F32), compiler_params=_cp(48),
    )(g8)


def adamw_small(g, w, m, v, name):
    def body(g_ref, w_ref, m_ref, v_ref, d_ref, nm_ref, nv_ref):
        d_ref[...], nm_ref[...], nv_ref[...] = _adamw(w_ref[...], g_ref[...], m_ref[...], v_ref[...])

    spec = pl.BlockSpec(g.shape, lambda i: (0, 0))
    shp = jax.ShapeDtypeStruct(g.shape, F32)
    return pl.pallas_call(
        body, name=name, grid=(1,), in_specs=[spec] * 4, out_specs=[spec] * 3, out_shape=[shp] * 3,
        compiler_params=_cp(48),
    )(g, w, m, v)


def _pack(arrs):
    flat = jnp.concatenate([a.reshape(-1).astype(F32) for a in arrs])
    pad = (-flat.shape[0]) % 1024
    return jnp.pad(flat, (0, pad)).reshape(-1, 128)


def _unpack(packed, shapes):
    flat = packed.reshape(-1)
    out, off = [], 0
    for s in shapes:
        n = 1
        for d in s:
            n *= d
        out.append(flat[off:off + n].reshape(s))
        off += n
    return out


def _lower_bounds(lb_logits):
    pr = jax.nn.softmax(lb_logits.astype(F32), axis=1)
    return jnp.cumsum(pr, axis=1) - pr[:, :1]


def device_step(xs, target, mods, sp, wts, gbufs):
    lbs = _lower_bounds(sp["lb_logits"])
    saved = []
    for layer in range(4):
        md = mods[layer]
        s = {"xs": xs}
        h1 = norm_mod(xs, sp["g_norm_mix"][layer][None], md, 0, f"norm_mix{layer}")
        s["h1"] = h1
        if layer % 2 == 0:
            e = layer // 2
            p = mm_cols(h1, wts["w_in_even"], e, f"in_proj{layer}")
            of, sf = hgrn_fwd(p, lbs[0, e][None], False, f"scan_f{layer}")
            ob, sb = hgrn_fwd(p, lbs[1, e][None], True, f"scan_b{layer}")
            ya = readout_fwd(of, ob, p, sp["g_hgrn_out"][e][None], f"readout{layer}")
            yb = chunkmlp_fwd(p, sp["w_spatial"][e], sp["b_spatial"][e][:, :, None], sp["g_spatial_v"][e][None],
                              f"cmlp{layer}")
            ycat = jnp.concatenate([ya, yb], axis=1)
            x1, f1 = mm_rows_res(ycat, wts["w_out_even"], e, xs, md, 2, f"out_proj{layer}")
            s.update(p=p, of=of, ob=ob, sf=sf, sb=sb, ycat=ycat, f1=f1)
        else:
            o = layer // 2
            pp = mm_rows(h1, wts["w_in_pool"], o, f"pool_in{layer}")
            z = pool_op(pp, False, f"pool{layer}")
            x1, ypre = mm_grp_res(z, wts["w_grp_pool"], o, sp["b_grp_pool"][o].reshape(1, D),
                                  sp["scale_pool"][o][None], xs, md, 2, f"pool_grp{layer}")
            s.update(z=z, ypre=ypre)
        h2 = norm_mod(x1, sp["g_norm_ffn"][layer][None], md, 3, f"norm_ffn{layer}")
        u = mm_cols(h2, wts["w_ffn_up"], layer, f"ffn_up{layer}", epi=_epi_relu2, out_dtype=BF16)
        x2, f2 = mm_rows_res(u, wts["w_ffn_down"], layer, x1, md, 5, f"ffn_down{layer}")
        s.update(x1=x1, h2=h2, u=u, f2=f2)
        saved.append(s)
        xs = x2

    dx, loss_lanes, dg_final = final_loss(xs, sp["g_norm_final"][None], target, "final_loss")

    gb = dict(gbufs)
    dmods = [None] * 4
    sg = {"g_norm_final": dg_final[0], "g_norm_mix": [None] * 4, "g_norm_ffn": [None] * 4,
          "dlbs": [[None, None], [None, None]], "g_hgrn_out": [None] * 2, "w_spatial": [None] * 2,
          "b_spatial": [None] * 2, "g_spatial_v": [None] * 2, "b_grp_pool": [None] * 2, "scale_pool": [None] * 2}
    for layer in reversed(range(4)):
        md, s = mods[layer], saved[layer]
        df2, dgt2 = gate_in(dx, s["f2"], md, 5, f"gate_ffn{layer}")
        da = mm_t_rows(df2, wts["w_ffn_down"], layer, f"ffn_down_t{layer}", epi=_epi_2sqrt, out_dtype=BF16,
                       extras_of=s["u"], per=2)
        gb["w_ffn_down"] = grad_rows(s["u"], df2, gb["w_ffn_down"], layer, f"g_ffn_down{layer}", per=2)
        dh2 = mm_t_cols(da, wts["w_ffn_up"], layer, f"ffn_up_t{layer}")
        gb["w_ffn_up"] = grad_cols(s["h2"], da, gb["w_ffn_up"], layer, f"g_ffn_up{layer}")
        dx1, dsh2, dsc2, dgf = normmod_bwd(dh2, s["x1"], dx, sp["g_norm_ffn"][layer][None], md, 3,
                                           f"norm_ffn_b{layer}")
        sg["g_norm_ffn"][layer] = dgf[0]
        if layer % 2 == 0:
            e = layer // 2
            df1, dgt1 = gate_in(dx1, s["f1"], md, 2, f"gate_mix{layer}")
            dycat = mm_t_rows(df1, wts["w_out_even"], e, f"out_proj_t{layer}")
            gb["w_out_even"] = grad_rows(s["ycat"], df1, gb["w_out_even"], e, f"g_out_proj{layer}")
            do, dpg, dng = readout_bwd(dycat, s["of"], s["ob"], s["p"], sp["g_hgrn_out"][e][None],
                                       f"readout_b{layer}")
            du, dv, dws, dbs, dgn = chunkmlp_bwd(dycat, s["p"], sp["w_spatial"][e], sp["b_spatial"][e][:, :, None],
                                                 sp["g_spatial_v"][e][None], f"cmlp_b{layer}")
            dq_f, di_f, dff, dlb_f = hgrn_bwd(s["p"], do, s["sf"], lbs[0, e][None], False, f"scan_f_b{layer}")
            dq, di, dfb, dlb_b = hgrn_bwd(s["p"], do, s["sb"], lbs[1, e][None], True, f"scan_b_b{layer}",
                                          add=(dq_f, di_f))
            dp = jnp.concatenate([dq, dff, dfb, di, dpg, du, dv], axis=1)
            dh1 = mm_t_cols(dp, wts["w_in_even"], e, f"in_proj_t{layer}")
            gb["w_in_even"] = grad_cols(s["h1"], dp, gb["w_in_even"], e, f"g_in_proj{layer}")
            sg["dlbs"][0][e], sg["dlbs"][1][e] = dlb_f[0], dlb_b[0]
            sg["g_hgrn_out"][e], sg["w_spatial"][e] = dng[0], dws
            sg["b_spatial"][e], sg["g_spatial_v"][e] = dbs[:, :, 0], dgn[0]
        else:
            o = layer // 2
            dyp, dgt1, dscale, dbias = gate_in_pool(dx1, s["ypre"], md, sp["scale_pool"][o][None], 2,
                                                    f"gate_mix{layer}")
            dz = mm_t_grp(dyp, wts["w_grp_pool"], o, f"pool_grp_t{layer}")
            gb["w_grp_pool"] = grad_grp(s["z"], dyp, gb["w_grp_pool"], o, f"g_pool_grp{layer}")
            dpp = pool_op(dz, True, f"pool_t{layer}")
            dh1 = mm_t_rows(dpp, wts["w_in_pool"], o, f"pool_in_t{layer}")
            gb["w_in_pool"] = grad_rows(s["h1"], dpp, gb["w_in_pool"], o, f"g_pool_in{layer}")
            sg["b_grp_pool"][o], sg["scale_pool"][o] = dbias[0].reshape(4, 512), dscale[0]
        dx, dsh1, dsc1, dgm = normmod_bwd(dh1, s["xs"], dx1, sp["g_norm_mix"][layer][None], md, 0,
                                          f"norm_mix_b{layer}")
        sg["g_norm_mix"][layer] = dgm[0]
        dmods[layer] = jnp.concatenate([dsh1, dsc1, dgt1, dsh2, dsc2, dgt2], axis=1)
    return loss_lanes, dx, dmods, sg, gb


BIG = ("w_in_even", "w_out_even", "w_in_pool", "w_grp_pool", "w_ffn_up", "w_ffn_down")
SMALL = ("b_ada", "g_norm_mix", "g_norm_ffn", "lb_logits", "g_hgrn_out", "w_spatial", "b_spatial", "g_spatial_v",
         "b_grp_pool", "scale_pool", "g_norm_final")
WEIGHTS = ("c_ctx", "w_ada", "b_ada", "g_norm_mix", "g_norm_ffn", "w_in_even", "w_out_even", "lb_logits",
           "g_hgrn_out", "w_spatial", "b_spatial", "g_spatial_v", "w_in_pool", "w_grp_pool", "b_grp_pool",
           "scale_pool", "w_ffn_up", "w_ffn_down", "g_norm_final")


def kernel(x, c, ctx, c_ctx, w_ada, b_ada, g_norm_mix, g_norm_ffn, w_in_even, w_out_even, lb_logits, g_hgrn_out, w_spatial, b_spatial, g_spatial_v, w_in_pool, w_grp_pool, b_grp_pool, scale_pool, w_ffn_up, w_ffn_down, g_norm_final, loss_target, m_c_ctx, m_w_ada, m_b_ada, m_g_norm_mix, m_g_norm_ffn, m_w_in_even, m_w_out_even, m_lb_logits, m_g_hgrn_out, m_w_spatial, m_b_spatial, m_g_spatial_v, m_w_in_pool, m_w_grp_pool, m_b_grp_pool, m_scale_pool, m_w_ffn_up, m_w_ffn_down, m_g_norm_final, v_c_ctx, v_w_ada, v_b_ada, v_g_norm_mix, v_g_norm_ffn, v_w_in_even, v_w_out_even, v_lb_logits, v_g_hgrn_out, v_w_spatial, v_b_spatial, v_g_spatial_v, v_w_in_pool, v_w_grp_pool, v_b_grp_pool, v_scale_pool, v_w_ffn_up, v_w_ffn_down, v_g_norm_final):
    loc = dict(c_ctx=c_ctx, w_ada=w_ada, b_ada=b_ada, g_norm_mix=g_norm_mix, g_norm_ffn=g_norm_ffn,
               w_in_even=w_in_even, w_out_even=w_out_even, lb_logits=lb_logits, g_hgrn_out=g_hgrn_out,
               w_spatial=w_spatial, b_spatial=b_spatial, g_spatial_v=g_spatial_v, w_in_pool=w_in_pool,
               w_grp_pool=w_grp_pool, b_grp_pool=b_grp_pool, scale_pool=scale_pool, w_ffn_up=w_ffn_up,
               w_ffn_down=w_ffn_down, g_norm_final=g_norm_final)
    mom = dict(c_ctx=m_c_ctx, w_ada=m_w_ada, b_ada=m_b_ada, g_norm_mix=m_g_norm_mix, g_norm_ffn=m_g_norm_ffn,
               w_in_even=m_w_in_even, w_out_even=m_w_out_even, lb_logits=m_lb_logits, g_hgrn_out=m_g_hgrn_out,
               w_spatial=m_w_spatial, b_spatial=m_b_spatial, g_spatial_v=m_g_spatial_v, w_in_pool=m_w_in_pool,
               w_grp_pool=m_w_grp_pool, b_grp_pool=m_b_grp_pool, scale_pool=m_scale_pool, w_ffn_up=m_w_ffn_up,
               w_ffn_down=m_w_ffn_down, g_norm_final=m_g_norm_final)
    var = dict(c_ctx=v_c_ctx, w_ada=v_w_ada, b_ada=v_b_ada, g_norm_mix=v_g_norm_mix, g_norm_ffn=v_g_norm_ffn,
               w_in_even=v_w_in_even, w_out_even=v_w_out_even, lb_logits=v_lb_logits, g_hgrn_out=v_g_hgrn_out,
               w_spatial=v_w_spatial, b_spatial=v_b_spatial, g_spatial_v=v_g_spatial_v, w_in_pool=v_w_in_pool,
               w_grp_pool=v_w_grp_pool, b_grp_pool=v_b_grp_pool, scale_pool=v_scale_pool, w_ffn_up=v_w_ffn_up,
               w_ffn_down=v_w_ffn_down, g_norm_final=v_g_norm_final)
    mx, my, mc = _mesh_pos()
    chip = 2 * mx + my
    dev = 2 * chip + mc

    full = gather_chips([loc[n].astype(BF16) for n in BIG], "gather_weights")
    wts = dict(zip(BIG, full))

    hello = allgather8(_pack([c[0], lb_logits, b_grp_pool, scale_pool]), "gather_small")
    parts = [_unpack(hello[d], [(D,), (2, 2, 256), (2, 4, 128), (2, 512)]) for d in range(8)]
    cs = jnp.concatenate([jnp.stack([parts[d][0] for d in range(8)]), c_ctx[None], jnp.zeros((7, D), F32)])
    chips_of = [parts[2 * s] for s in range(NSH)]
    sp = dict(loc)
    sp["lb_logits"] = jnp.concatenate([q[1] for q in chips_of], axis=2)
    sp["b_grp_pool"] = jnp.concatenate([q[2] for q in chips_of], axis=2)
    sp["scale_pool"] = jnp.concatenate([q[3] for q in chips_of], axis=1)

    b_loc = lax.dynamic_slice_in_dim(b_ada, chip * 3072, 3072, axis=1)[:, None, :]
    mods_loc = ada_mods(cs, w_ada, b_loc, "ada_mods")
    mods_all = allgather8(mods_loc.reshape(-1, 128), "gather_mods").reshape(8, 4, 16, 3072)
    mods_full = jnp.concatenate([mods_all[2 * s] for s in range(NSH)], axis=2)
    mine = lax.dynamic_index_in_dim(mods_full, dev, axis=1, keepdims=False)
    mods = [jnp.stack([mods_full[l, 8].reshape(6, D), mine[l].reshape(6, D)]) for l in range(4)]

    xs = jnp.concatenate([ctx[0], x[0]], axis=0)
    gbufs = {n: jnp.zeros((NSH,) + loc[n].shape, BF16) for n in BIG}
    loss_lanes, dxs, dmods, sg, gb = device_step(xs, loss_target[0], mods, sp, wts, gbufs)
    grad_x = dxs[CTX:][None]

    rbs = exchange_grads([gb[n] for n in BIG], "exchange_grads")
    out = {}
    for n, rb in zip(BIG, rbs):
        out[n] = adamw_big(rb, loc[n], mom[n], var[n], f"adamw_{n}")

    dm_lat = jnp.stack([dmods[l][1].reshape(6 * D) for l in range(4)])
    dm_ctx = jnp.stack([dmods[l][0].reshape(6 * D) for l in range(4)])
    small_shapes = [(4, 6 * D), (4, 6 * D), (4, D), (4, D), (2, 2, AW), (2, AW), (2, NH, 128, 128), (2, NH, 128),
                    (2, AW), (2, 4, 512), (2, D), (D,), (128,)]
    mine_small = _pack([dm_lat, dm_ctx, jnp.stack(sg["g_norm_mix"]), jnp.stack(sg["g_norm_ffn"]),
                        jnp.stack([jnp.stack(sg["dlbs"][0]), jnp.stack(sg["dlbs"][1])]),
                        jnp.stack(sg["g_hgrn_out"]), jnp.stack(sg["w_spatial"]), jnp.stack(sg["b_spatial"]),
                        jnp.stack(sg["g_spatial_v"]), jnp.stack(sg["b_grp_pool"]), jnp.stack(sg["scale_pool"]),
                        sg["g_norm_final"], loss_lanes[0]])
    all_small = allgather8(mine_small, "gather_small_grads")
    tot = _unpack(sum8(all_small, "sum_small_grads"), small_shapes)
    (_, dm_ctx_tot, g_mix, g_ffn, dlbs, g_hg, g_ws, g_bs, g_gv, g_bg, g_sc, g_fin, loss_v) = tot
    loss = loss_v[0]
    dm_lat_all = jnp.stack([_unpack(all_small[d], small_shapes[:1])[0] for d in range(8)])
    g_b_ada = jnp.sum(dm_lat_all, axis=0) + dm_ctx_tot
    _, lb_vjp = jax.vjp(_lower_bounds, sp["lb_logits"])
    g_lb_full = lb_vjp(dlbs)[0]
    grads = {"b_ada": g_b_ada, "g_norm_mix": g_mix, "g_norm_ffn": g_ffn,
             "lb_logits": lax.dynamic_slice_in_dim(g_lb_full, chip * 256, 256, axis=2),
             "g_hgrn_out": g_hg, "w_spatial": g_ws, "b_spatial": g_bs, "g_spatial_v": g_gv,
             "b_grp_pool": lax.dynamic_slice_in_dim(g_bg, chip * 128, 128, axis=2),
             "scale_pool": lax.dynamic_slice_in_dim(g_sc, chip * 512, 512, axis=1), "g_norm_final": g_fin}

    dm_rows = jnp.concatenate([dm_lat_all.transpose(1, 0, 2), dm_ctx_tot[:, None, :], jnp.zeros((4, 7, 6 * D), F32)],
                              axis=1)
    dm_loc = lax.dynamic_slice_in_dim(dm_rows, chip * 3072, 3072, axis=2)
    g_wa, d_wa, nm_wa, nv_wa, dc_part = ada_update(cs, dm_loc, w_ada, m_w_ada, v_w_ada, "ada_update")
    out["w_ada"] = [g_wa, d_wa, nm_wa, nv_wa]
    dc_all = allgather8(dc_part[8].reshape(16, 128), "gather_dc")
    dpre = dc_all[0] + dc_all[2] + dc_all[4] + dc_all[6]
    sig = jax.nn.sigmoid(c_ctx)
    grads["c_ctx"] = dpre.reshape(D) * (sig * (1.0 + c_ctx * (1.0 - sig)))

    names = ("c_ctx",) + SMALL
    shapes = [loc[n].shape for n in names]
    d_s, nm_s, nv_s = adamw_small(_pack([grads[n] for n in names]), _pack([loc[n] for n in names]),
                                  _pack([mom[n] for n in names]), _pack([var[n] for n in names]), "adamw_small")
    for n, dl, nm, nv in zip(names, _unpack(d_s, shapes), _unpack(nm_s, shapes), _unpack(nv_s, shapes)):
        out[n] = [grads[n], dl, nm, nv]

    return (loss, grad_x, *[out[n][0] for n in WEIGHTS], *[out[n][1] for n in WEIGHTS],
            *[out[n][2] for n in WEIGHTS], *[out[n][3] for n in WEIGHTS])
```

```python
import functools

import jax
import jax.numpy as jnp
from jax import lax
from jax.experimental import pallas as pl
from jax.experimental.pallas import tpu as pltpu

F32 = jnp.float32
BF16 = jnp.bfloat16
MESH = pl.DeviceIdType.MESH
ANY = pl.BlockSpec(memory_space=pl.ANY)
HIGHEST = lax.Precision.HIGHEST

D = 2048
DFF = 8192
CTX = 256
TR = 256
GRID_W = 64
EPS = 1e-6
LOG_FLOOR = 1e-30
NH = 8
DK = 128
SUB = 16
B_CHUNK = 128
AW = NH * DK
POOL_WINDOWS = (2, 4, 8, 16)
NSH = 4
ADAM_LR, ADAM_B1, ADAM_B2, ADAM_EPS, ADAM_WD, ADAM_STEP = 0.001, 0.9, 0.999, 1e-08, 0.01, 10
MIB = 1024 * 1024


def _cp(vmem_mib):
    return pltpu.CompilerParams(vmem_limit_bytes=vmem_mib * MIB)


def _bf(v):
    return v.astype(BF16)


def _nn(a, b):
    return lax.dot_general(a, b, (((1,), (0,)), ((), ())), preferred_element_type=F32)


def _nt(a, b):
    return lax.dot_general(a, b, (((1,), (1,)), ((), ())), preferred_element_type=F32)


def _tn(a, b):
    return lax.dot_general(a, b, (((0,), (0,)), ((), ())), preferred_element_type=F32)


def _mesh_pos():
    return lax.axis_index("x"), lax.axis_index("y"), lax.axis_index("c")


HBM = pl.BlockSpec(memory_space=pltpu.HBM)
SEM = pl.BlockSpec(memory_space=pltpu.SEMAPHORE)
EFFECT = pltpu.SideEffectType.DATAFLOW_SIDE_EFFECTING


def _peer_copies(exchange, srcs, lands, send_sems, recv_sems):
    x, y, c = _mesh_pos()
    me = 2 * x + y
    pairs = []
    for t in range(len(srcs)):
        for j, (px, py) in enumerate([(1 - x, y), (x, 1 - y), (1 - x, 1 - y)]):
            peer = 2 * px + py
            src = srcs[t].at[peer] if exchange else srcs[t]
            out_slot, in_slot = (2 * me + c, 2 * peer + c) if exchange else (me, peer)

            def mk(slot, t=t, j=j, px=px, py=py, src=src):
                return pltpu.make_async_remote_copy(
                    src_ref=src, dst_ref=lands[t].at[slot], send_sem=send_sems.at[3 * t + j],
                    recv_sem=recv_sems.at[3 * t + j], device_id=(px, py, c), device_id_type=MESH)

            pairs.append((mk(out_slot), mk(in_slot)))
    return pairs


def ici_start(srcs, lands, after, exchange, name):
    n = len(srcs)

    def body(*refs):
        send_sems, recv_sems = refs[2 * n + 1], refs[2 * n + 2]
        for out_copy, _ in _peer_copies(exchange, refs[:n], refs[n:2 * n], send_sems, recv_sems):
            out_copy.start()
        refs[-1][...] = jnp.zeros_like(refs[-1])

    arrs = list(srcs) + list(lands)
    res = pl.pallas_call(
        body, name=name,
        out_shape=(pltpu.SemaphoreType.DMA((3 * n,)), pltpu.SemaphoreType.DMA((3 * n,)),
                   *[pltpu.HBM(a.shape, a.dtype) for a in arrs], jax.ShapeDtypeStruct((8, 128), F32)),
        in_specs=[HBM] * (2 * n) + [ANY],
        out_specs=(SEM, SEM, *[HBM] * (2 * n), pl.BlockSpec(memory_space=pltpu.VMEM)),
        input_output_aliases={t: 2 + t for t in range(2 * n)},
        compiler_params=pltpu.CompilerParams(has_side_effects=EFFECT),
    )(*[pltpu.with_memory_space_constraint(a, pltpu.HBM) for a in arrs], after)
    return res[0], res[1], list(res[2:2 + n]), list(res[2 + n:2 + 2 * n]), res[-1]


def ici_wait(started, after, exchange, name):
    send_sems, recv_sems, srcs, lands, _ = started
    n = len(srcs)

    def body(*refs):
        for out_copy, in_copy in _peer_copies(exchange, refs[:n], refs[n:2 * n], refs[2 * n], refs[2 * n + 1]):
            out_copy.wait_send()
            in_copy.wait_recv()

    arrs = list(srcs) + list(lands)
    res = pl.pallas_call(
        body, name=name,
        out_shape=tuple(pltpu.HBM(a.shape, a.dtype) for a in arrs),
        in_specs=[HBM] * (2 * n) + [SEM, SEM, ANY], out_specs=tuple([HBM] * (2 * n)),
        input_output_aliases={t: t for t in range(2 * n)},
        compiler_params=pltpu.CompilerParams(has_side_effects=EFFECT),
    )(*arrs, send_sems, recv_sems, after)
    return list(res[:n]), list(res[n:])


def forward_to_sibling(gs, rbs, name):
    n = len(gs)

    def body(*refs):
        g, rb, outs = refs[:n], refs[n:2 * n], refs[2 * n:3 * n]
        send_sems, recv_sems, local_sems = refs[3 * n:]
        x, y, c = _mesh_pos()
        sibling = (x, y, 1 - c)
        me = 2 * x + y
        others = [2 * (1 - x) + y, 2 * x + 1 - y, 2 * (1 - x) + 1 - y]

        def copy(t, k, src, slot):
            return pltpu.make_async_remote_copy(
                src_ref=src, dst_ref=outs[t].at[slot], send_sem=send_sems.at[4 * t + k],
                recv_sem=recv_sems.at[4 * t + k], device_id=sibling, device_id_type=MESH)

        local, sends = [], []
        for t in range(n):
            loc = pltpu.make_async_copy(g[t].at[me], outs[t].at[2 * me + c], local_sems.at[t])
            loc.start()
            local.append(loc)
            mine = [copy(t, 0, g[t].at[me], 2 * me + c)]
            mine += [copy(t, 1 + j, rb[t].at[2 * p + c], 2 * p + c) for j, p in enumerate(others)]
            for cp in mine:
                cp.start()
            sends += mine
        for t in range(n):
            copy(t, 0, g[t].at[me], 2 * me + 1 - c).wait_recv()
            for j, p in enumerate(others):
                copy(t, 1 + j, g[t].at[me], 2 * p + 1 - c).wait_recv()
        for cp in sends:
            cp.wait_send()
        for loc in local:
            loc.wait()

    return pl.pallas_call(
        body, name=name,
        out_shape=[jax.ShapeDtypeStruct(r.shape, r.dtype) for r in rbs],
        in_specs=[ANY] * (2 * n), out_specs=[ANY] * n,
        input_output_aliases={n + t: t for t in range(n)},
        scratch_shapes=[pltpu.SemaphoreType.DMA((4 * n,)), pltpu.SemaphoreType.DMA((4 * n,)),
                        pltpu.SemaphoreType.DMA((n,))],
    )(*gs, *rbs)


def allgather8(v, name):
    m, n = v.shape

    def body(x_ref, out_ref, send_sems, recv_sems, local_sem):
        x, y, c = _mesh_pos()
        me, sibling = (x, y, c), (x, y, 1 - c)
        chips = [(1 - x, y), (x, 1 - y), (1 - x, 1 - y)]

        def rows(px, py, pc):
            return out_ref.at[4 * px + 2 * py + pc]

        def copy(k, block, to, src=None):
            return pltpu.make_async_remote_copy(
                src_ref=rows(*block) if src is None else src, dst_ref=rows(*block),
                send_sem=send_sems.at[k], recv_sem=recv_sems.at[k], device_id=to, device_id_type=MESH)

        mine = pltpu.make_async_copy(x_ref, rows(*me), local_sem)
        mine.start()
        first = [copy(0, me, sibling, src=x_ref)]
        first += [copy(1 + j, me, (*chip, c), src=x_ref) for j, chip in enumerate(chips)]
        for cp in first:
            cp.start()
        passed = [copy(4 + j, (*chip, c), sibling) for j, chip in enumerate(chips)]
        for j, chip in enumerate(chips):
            copy(1 + j, (*chip, c), me).wait_recv()
            passed[j].start()
        copy(0, sibling, me).wait_recv()
        for j, chip in enumerate(chips):
            copy(4 + j, (*chip, 1 - c), me).wait_recv()
        for cp in first + passed:
            cp.wait_send()
        mine.wait()

    return pl.pallas_call(
        body, name=name,
        out_shape=jax.ShapeDtypeStruct((8, m, n), v.dtype),
        in_specs=[pl.BlockSpec(memory_space=pltpu.VMEM)],
        out_specs=pl.BlockSpec(memory_space=pltpu.VMEM),
        scratch_shapes=[pltpu.SemaphoreType.DMA((7,)), pltpu.SemaphoreType.DMA((7,)), pltpu.SemaphoreType.DMA],
        compiler_params=_cp(40),
    )(v)


def _row_spec(width=D, off=0):
    return pl.BlockSpec((TR, width), lambda i, off=off: (i, off))


def _vec_spec(width=D):
    return pl.BlockSpec((1, width), lambda i: (0, 0))


def _mod_spec():
    return pl.BlockSpec((None, 6, D), lambda i: (jnp.minimum(i, 1), 0, 0))


def _pair_spec(width=D):
    return pl.BlockSpec((None, 1, width), lambda i: (jnp.minimum(i, 1), 0, 0))


def _accum(ref, val, first):
    @pl.when(first)
    def _():
        ref[...] = val

    @pl.when(jnp.logical_not(first))
    def _():
        ref[...] += val


def norm_mod(xs, g, mod, si, name):
    r = xs.shape[0]

    def body(x_ref, g_ref, m_ref, o_ref):
        x = x_ref[...]
        rstd = lax.rsqrt(jnp.mean(x * x, axis=-1, keepdims=True) + EPS)
        n = x * rstd * g_ref[...]
        o_ref[...] = (n * (1.0 + m_ref[si + 1:si + 2, :]) + m_ref[si:si + 1, :]).astype(BF16)

    return pl.pallas_call(
        body, name=name, grid=(r // TR,),
        in_specs=[_row_spec(), _vec_spec(), _mod_spec()], out_specs=_row_spec(),
        out_shape=jax.ShapeDtypeStruct((r, D), BF16), compiler_params=_cp(32),
    )(xs, g, mod)


def gate_in(dx, f, mod, gi, name):
    r = dx.shape[0]

    def body(dx_ref, f_ref, m_ref, o_ref, dg_ref):
        i = pl.program_id(0)
        dxv = dx_ref[...]
        o_ref[...] = (dxv * m_ref[gi:gi + 1, :]).astype(BF16)
        _accum(dg_ref, jnp.sum(dxv * f_ref[...].astype(F32), axis=0, keepdims=True), i <= 1)

    return pl.pallas_call(
        body, name=name, grid=(r // TR,),
        in_specs=[_row_spec(), _row_spec(), _mod_spec()], out_specs=[_row_spec(), _pair_spec()],
        out_shape=[jax.ShapeDtypeStruct((r, D), BF16), jax.ShapeDtypeStruct((2, 1, D), F32)],
        compiler_params=_cp(32),
    )(dx, f, mod)


def gate_in_pool(dx, ypre, mod, scale, gi, name):
    r = dx.shape[0]

    def body(dx_ref, y_ref, m_ref, s_ref, o_ref, dg_ref, ds_ref, db_ref):
        i = pl.program_id(0)
        dxv = dx_ref[...]
        yp = y_ref[...].astype(F32)
        sc = s_ref[...]
        dy = dxv * m_ref[gi:gi + 1, :]
        dyp = dy * sc
        o_ref[...] = dyp.astype(BF16)
        _accum(dg_ref, jnp.sum(dxv * (yp * sc), axis=0, keepdims=True), i <= 1)
        _accum(ds_ref, jnp.sum(dy * yp, axis=0, keepdims=True), i == 0)
        _accum(db_ref, jnp.sum(dyp, axis=0, keepdims=True), i == 0)

    return pl.pallas_call(
        body, name=name, grid=(r // TR,),
        in_specs=[_row_spec(), _row_spec(), _mod_spec(), _vec_spec()],
        out_specs=[_row_spec(), _pair_spec(), _vec_spec(), _vec_spec()],
        out_shape=[jax.ShapeDtypeStruct((r, D), BF16), jax.ShapeDtypeStruct((2, 1, D), F32),
                   jax.ShapeDtypeStruct((1, D), F32), jax.ShapeDtypeStruct((1, D), F32)],
        compiler_params=_cp(32),
    )(dx, ypre, mod, scale)


def normmod_bwd(dh, x, dxo, g, mod, si, name):
    r = x.shape[0]

    def body(dh_ref, x_ref, dxo_ref, g_ref, m_ref, dx_ref, dsh_ref, dsc_ref, dg_ref):
        i = pl.program_id(0)
        xv = x_ref[...]
        dhv = dh_ref[...]
        gv = g_ref[...]
        rstd = lax.rsqrt(jnp.mean(xv * xv, axis=-1, keepdims=True) + EPS)
        xhat = xv * rstd
        dn = dhv * (1.0 + m_ref[si + 1:si + 2, :])
        dxh = dn * gv
        dx_ref[...] = rstd * (dxh - xhat * jnp.mean(dxh * xhat, axis=-1, keepdims=True)) + dxo_ref[...]
        _accum(dsh_ref, jnp.sum(dhv, axis=0, keepdims=True), i <= 1)
        _accum(dsc_ref, jnp.sum(dhv * (xhat * gv), axis=0, keepdims=True), i <= 1)
        _accum(dg_ref, jnp.sum(dn * xhat, axis=0, keepdims=True), i == 0)

    return pl.pallas_call(
        body, name=name, grid=(r // TR,),
        in_specs=[_row_spec(), _row_spec(), _row_spec(), _vec_spec(), _mod_spec()],
        out_specs=[_row_spec(), _pair_spec(), _pair_spec(), _vec_spec()],
        out_shape=[jax.ShapeDtypeStruct((r, D), F32), jax.ShapeDtypeStruct((2, 1, D), F32),
                   jax.ShapeDtypeStruct((2, 1, D), F32), jax.ShapeDtypeStruct((1, D), F32)],
        compiler_params=_cp(48),
    )(dh, x, dxo, g, mod)


def final_loss(xs, g, target, name):
    r = xs.shape[0]

    def body(x_ref, g_ref, t_ref, dx_ref, loss_ref, dg_ref):
        i = pl.program_id(0)

        @pl.when(i == 0)
        def _():
            dx_ref[...] = jnp.zeros_like(dx_ref)
            loss_ref[...] = jnp.zeros_like(loss_ref)
            dg_ref[...] = jnp.zeros_like(dg_ref)

        @pl.when(i > 0)
        def _():
            xv = x_ref[...]
            gv = g_ref[...]
            rstd = lax.rsqrt(jnp.mean(xv * xv, axis=-1, keepdims=True) + EPS)
            xhat = xv * rstd
            err = xhat * gv - t_ref[...]
            part = 0.5 * jnp.sum(jnp.mean(err * err, axis=-1, keepdims=True), axis=0, keepdims=True)
            lane = lax.broadcasted_iota(jnp.int32, (1, 128), 1)
            loss_ref[...] += jnp.where(lane == 0, part, 0.0)
            dy = err * (1.0 / D)
            dg_ref[...] += jnp.sum(dy * xhat, axis=0, keepdims=True)
            dxh = dy * gv
            dx_ref[...] = rstd * (dxh - xhat * jnp.mean(dxh * xhat, axis=-1, keepdims=True))

    return pl.pallas_call(
        body, name=name, grid=(r // TR,),
        in_specs=[_row_spec(), _vec_spec(), pl.BlockSpec((TR, D), lambda i: (jnp.maximum(i - 1, 0), 0))],
        out_specs=[_row_spec(), pl.BlockSpec((1, 128), lambda i: (0, 0)), _vec_spec()],
        out_shape=[jax.ShapeDtypeStruct((r, D), F32), jax.ShapeDtypeStruct((1, 128), F32),
                   jax.ShapeDtypeStruct((1, D), F32)],
        compiler_params=_cp(32),
    )(xs, g, target)


def _mm(name, mode, a, b, grid, a_spec, b_spec, out_shapes, out_specs, epi, kaxis=None, acc=None,
        extras=(), vmem=48):
    ne, no = len(extras), len(out_shapes)
    dot = {"nn": _nn, "nt": _nt, "tn": _tn}[mode]
    nk = grid[kaxis] if kaxis is not None else 1

    def body(*refs):
        a_ref, b_ref = refs[0], refs[1]
        ex = refs[2:2 + ne]
        outs = refs[2 + ne:2 + ne + no]
        ids = [pl.program_id(ax) for ax in range(len(grid))]
        part = dot(a_ref[...], b_ref[...])
        if kaxis is None:
            epi(part, ids, ex, outs)
        else:
            acc_ref = refs[-1]
            k = ids[kaxis]

            @pl.when(k == 0)
            def _():
                acc_ref[...] = part

            @pl.when(k > 0)
            def _():
                acc_ref[...] += part

            @pl.when(k == nk - 1)
            def _():
                epi(acc_ref[...], ids, ex, outs)

    operands = [a, b] + [e[0] for e in extras]
    in_specs = [a_spec, b_spec] + [e[1] for e in extras]
    return pl.pallas_call(
        body, name=name, grid=grid, in_specs=in_specs, out_specs=out_specs, out_shape=out_shapes,
        scratch_shapes=[] if kaxis is None else [pltpu.VMEM(acc, F32)], compiler_params=_cp(vmem),
    )(*operands)


def _epi_store(acc, ids, ex, outs):
    outs[0][...] = acc.astype(outs[0].dtype)


def _epi_relu2(acc, ids, ex, outs):
    rl = jnp.maximum(acc, 0.0)
    outs[0][...] = (rl * rl).astype(BF16)


def _epi_2sqrt(acc, ids, ex, outs):
    outs[0][...] = (acc * (2.0 * jnp.sqrt(ex[0][...].astype(F32)))).astype(BF16)


def _gate_rows(ids, tm, shape, mod_ref, gi):
    rid = ids[0] * tm + lax.broadcasted_iota(jnp.int32, shape, 0)
    return jnp.where(rid < CTX, mod_ref[0, gi:gi + 1, :], mod_ref[1, gi:gi + 1, :])


def _epi_res(gi, tm):
    def epi(acc, ids, ex, outs):
        outs[0][...] = ex[0][...] + _gate_rows(ids, tm, acc.shape, ex[1], gi) * acc
        outs[1][...] = acc.astype(BF16)
    return epi


def _epi_pool(gi, tm):
    def epi(acc, ids, ex, outs):
        ypre = acc + ex[2][...]
        outs[0][...] = ex[0][...] + _gate_rows(ids, tm, acc.shape, ex[1], gi) * (ypre * ex[3][...])
        outs[1][...] = ypre.astype(BF16)
    return epi


def _tm(r):
    return 768 if r % 768 == 0 else TR


def mm_cols(a, w, l, name, epi=_epi_store, out_dtype=F32, per=2):
    r, k = a.shape
    c = w.shape[3]
    tm, tn = _tm(r), c // per
    return _mm(name, "nn", a, w, (r // tm, NSH * per),
               pl.BlockSpec((tm, k), lambda i, j: (i, 0)),
               pl.BlockSpec((None, None, k, tn), lambda i, j: (j // per, l, 0, j % per)),
               [jax.ShapeDtypeStruct((r, NSH * c), out_dtype)], [pl.BlockSpec((tm, tn), lambda i, j: (i, j))],
               epi, vmem=56)[0]


def mm_rows_res(a, w, l, res, mod, gi, name):
    r = a.shape[0]
    kc = w.shape[2]
    tm, tn = _tm(r), 1024
    return _mm(name, "nn", a, w, (r // tm, D // tn, NSH),
               pl.BlockSpec((tm, kc), lambda i, j, k: (i, k)),
               pl.BlockSpec((None, None, kc, tn), lambda i, j, k: (k, l, 0, j)),
               [jax.ShapeDtypeStruct((r, D), F32), jax.ShapeDtypeStruct((r, D), BF16)],
               [pl.BlockSpec((tm, tn), lambda i, j, k: (i, j))] * 2,
               _epi_res(gi, tm), kaxis=2, acc=(tm, tn),
               extras=[(res, pl.BlockSpec((tm, tn), lambda i, j, k: (i, j))),
                       (mod, pl.BlockSpec((2, 6, tn), lambda i, j, k: (0, 0, j)))], vmem=56)


def mm_rows(a, w, l, name):
    r = a.shape[0]
    kc = w.shape[2]
    tm, tn = _tm(r), 1024
    return _mm(name, "nn", a, w, (r // tm, D // tn, NSH),
               pl.BlockSpec((tm, kc), lambda i, j, k: (i, k)),
               pl.BlockSpec((None, None, kc, tn), lambda i, j, k: (k, l, 0, j)),
               [jax.ShapeDtypeStruct((r, D), F32)], [pl.BlockSpec((tm, tn), lambda i, j, k: (i, j))],
               _epi_store, kaxis=2, acc=(tm, tn), vmem=56)[0]


def mm_grp_res(z, w, o, bias, scale, res, mod, gi, name):
    r = z.shape[0]
    tm = _tm(r)
    return _mm(name, "nn", z, w, (r // tm, 4, NSH),
               pl.BlockSpec((tm, 128), lambda i, g, k: (i, 4 * g + k)),
               pl.BlockSpec((None, None, None, 128, 512), lambda i, g, k: (k, o, g, 0, 0)),
               [jax.ShapeDtypeStruct((r, D), F32), jax.ShapeDtypeStruct((r, D), BF16)],
               [pl.BlockSpec((tm, 512), lambda i, g, k: (i, g))] * 2,
               _epi_pool(gi, tm), kaxis=2, acc=(tm, 512),
               extras=[(res, pl.BlockSpec((tm, 512), lambda i, g, k: (i, g))),
                       (mod, pl.BlockSpec((2, 6, 512), lambda i, g, k: (0, 0, g))),
                       (bias, pl.BlockSpec((1, 512), lambda i, g, k: (0, g))),
                       (scale, pl.BlockSpec((1, 512), lambda i, g, k: (0, g)))], vmem=48)


def mm_t_rows(a, w, l, name, epi=_epi_store, out_dtype=F32, extras_of=None, per=1):
    r, n = a.shape
    kc = w.shape[2]
    tm, tn = _tm(r), kc // per
    extras = []
    if extras_of is not None:
        extras = [(extras_of, pl.BlockSpec((tm, tn), lambda i, j: (i, j)))]
    return _mm(name, "nt", a, w, (r // tm, NSH * per),
               pl.BlockSpec((tm, n), lambda i, j: (i, 0)),
               pl.BlockSpec((None, None, tn, n), lambda i, j: (j // per, l, j % per, 0)),
               [jax.ShapeDtypeStruct((r, NSH * kc), out_dtype)], [pl.BlockSpec((tm, tn), lambda i, j: (i, j))],
               epi, extras=extras, vmem=56)[0]


def mm_t_cols(a, w, l, name):
    r = a.shape[0]
    k, c = w.shape[2], w.shape[3]
    tm, tn = _tm(r), 1024
    return _mm(name, "nt", a, w, (r // tm, k // tn, NSH),
               pl.BlockSpec((tm, c), lambda i, j, s: (i, s)),
               pl.BlockSpec((None, None, tn, c), lambda i, j, s: (s, l, j, 0)),
               [jax.ShapeDtypeStruct((r, k), F32)], [pl.BlockSpec((tm, tn), lambda i, j, s: (i, j))],
               _epi_store, kaxis=2, acc=(tm, tn), vmem=56)[0]


def mm_t_grp(dy, w, o, name):
    r = dy.shape[0]
    tm = _tm(r)
    return _mm(name, "nt", dy, w, (r // tm, 4, NSH),
               pl.BlockSpec((tm, 512), lambda i, g, s: (i, g)),
               pl.BlockSpec((None, None, None, 128, 512), lambda i, g, s: (s, o, g, 0, 0)),
               [jax.ShapeDtypeStruct((r, D), F32)], [pl.BlockSpec((tm, 128), lambda i, g, s: (i, 4 * g + s))],
               _epi_store, vmem=40)[0]


def grad_cols(a, b, name, ta=1024, per=2):
    r, k = a.shape
    c = b.shape[1] // NSH
    tk, tn = _tm(r), c // per
    return _mm(name, "tn", a, b, (NSH, k // ta, per, r // tk),
               pl.BlockSpec((tk, ta), lambda s, i, j, t: (t, i)),
               pl.BlockSpec((tk, tn), lambda s, i, j, t: (t, s * per + j)),
               [jax.ShapeDtypeStruct((NSH, 1, k, c), BF16)],
               [pl.BlockSpec((None, None, ta, tn), lambda s, i, j, t: (s, 0, i, j))],
               _epi_store, kaxis=3, acc=(ta, tn), vmem=56)[0]


def grad_rows(a, b, name, per=1, tn=1024):
    r = a.shape[0]
    kc, n = a.shape[1] // NSH, b.shape[1]
    tk, ta = _tm(r), kc // per
    return _mm(name, "tn", a, b, (NSH, per, n // tn, r // tk),
               pl.BlockSpec((tk, ta), lambda s, i, j, t: (t, s * per + i)),
               pl.BlockSpec((tk, tn), lambda s, i, j, t: (t, j)),
               [jax.ShapeDtypeStruct((NSH, 1, kc, n), BF16)],
               [pl.BlockSpec((None, None, ta, tn), lambda s, i, j, t: (s, 0, i, j))],
               _epi_store, kaxis=3, acc=(ta, tn), vmem=56)[0]


def grad_grp(z, dy, name):
    r = z.shape[0]
    tk = _tm(r)
    return _mm(name, "tn", z, dy, (NSH, 4, r // tk),
               pl.BlockSpec((tk, 128), lambda s, g, t: (t, 4 * g + s)),
               pl.BlockSpec((tk, 512), lambda s, g, t: (t, g)),
               [jax.ShapeDtypeStruct((NSH, 1, 4, 128, 512), BF16)],
               [pl.BlockSpec((None, None, None, 128, 512), lambda s, g, t: (s, 0, g, 0, 0))],
               _epi_store, kaxis=2, acc=(128, 512), vmem=40)[0]


def _scan_tile(reverse, nt):
    if reverse:
        return lambda p: jnp.where(p == 0, 0, nt - p)
    return lambda p: p


def _gates(f, lbv):
    sg = jax.nn.sigmoid(f)
    fg = lbv + (1.0 - lbv) * sg
    g = jnp.log(jnp.maximum(fg, LOG_FLOOR))
    kk = (1.0 - lbv) * jax.nn.sigmoid(-f)
    return sg, fg, g, kk


def _chunk_cumsum(g, reverse):
    n = g.shape[0]
    rr = lax.broadcasted_iota(jnp.int32, (n, n), 0)
    cc = lax.broadcasted_iota(jnp.int32, (n, n), 1)
    inside = (rr // SUB) == (cc // SUB)
    tri = jnp.where(inside & ((cc >= rr) if reverse else (cc <= rr)), 1.0, 0.0).astype(F32)
    return jnp.dot(tri, g, precision=HIGHEST, preferred_element_type=F32)


def _decay(b, s, rows, reverse):
    dec = jnp.exp(jnp.minimum(b - b[s:s + 1], 0.0))
    return jnp.where((rows <= s) if reverse else (rows >= s), dec, 0.0)


def hgrn_fwd(p, lb, reverse, name):
    r = p.shape[0]
    nt = r // TR
    nsub = TR // SUB
    fcol = 2 if reverse else 1
    tile = _scan_tile(reverse, nt)

    def body(q_ref, f_ref, v_ref, lb_ref, o_ref, sin_ref, st, k_s, b_s):
        i = pl.program_id(0)

        @pl.when(i == 0)
        def _():
            st[...] = jnp.zeros_like(st)

        sin_ref[...] = st[...]
        _, _, g, kk = _gates(f_ref[...], lb_ref[...])
        k_s[...] = kk
        b_s[...] = _chunk_cumsum(g, reverse)
        rows = lax.broadcasted_iota(jnp.int32, (SUB, DK), 0)

        def sub(jj, carry):
            j = (nsub - 1 - jj) if reverse else jj
            rs = pl.ds(pl.multiple_of(j * SUB, SUB), SUB)
            for h in range(NH):
                sl = slice(h * DK, (h + 1) * DK)
                q, k, b, v = q_ref[rs, sl], k_s[rs, sl], b_s[rs, sl], v_ref[rs, sl]
                btot = b[0:1] if reverse else b[SUB - 1:SUB]
                o = _nt(_bf(q * jnp.exp(b)), _bf(st[h]))
                for s in range(SUB):
                    col = jnp.sum(q * k[s:s + 1] * _decay(b, s, rows, reverse), axis=-1, keepdims=True)
                    o = o + col * v[s:s + 1]
                o_ref[rs, sl] = o
                st[h] = st[h] * jnp.exp(btot) + _tn(_bf(v), _bf(k * jnp.exp(btot - b)))
            return carry

        lax.fori_loop(0, nsub, sub, 0)

    seg = lambda col: pl.BlockSpec((TR, AW), lambda i, col=col: (tile(i), col))
    return pl.pallas_call(
        body, name=name, grid=(nt,),
        in_specs=[seg(0), seg(fcol), seg(3), _vec_spec(AW)],
        out_specs=[pl.BlockSpec((TR, AW), lambda i: (tile(i), 0)),
                   pl.BlockSpec((None, NH, DK, DK), lambda i: (tile(i), 0, 0, 0))],
        out_shape=[jax.ShapeDtypeStruct((r, AW), F32), jax.ShapeDtypeStruct((nt, NH, DK, DK), F32)],
        scratch_shapes=[pltpu.VMEM((NH, DK, DK), F32), pltpu.VMEM((TR, AW), F32), pltpu.VMEM((TR, AW), F32)],
        compiler_params=_cp(40),
    )(p, p, p, lb)


def hgrn_bwd(p, do, sin, lb, reverse, name, add=None):
    r = p.shape[0]
    nt = r // TR
    nsub = TR // SUB
    fcol = 2 if reverse else 1
    tile0 = _scan_tile(reverse, nt)
    tile = lambda i: tile0(nt - 1 - i)
    nadd = 0 if add is None else 2
    out_dt = F32 if add is None else BF16

    def body(*refs):
        q_ref, f_ref, v_ref, do_ref, sin_ref, lb_ref = refs[:6]
        adds = refs[6:6 + nadd]
        dq_ref, dv_ref, df_ref, dlb_ref = refs[6 + nadd:10 + nadd]
        dst, srun, ssub, k_s, b_s, sg_s, fg_s = refs[10 + nadd:]
        i = pl.program_id(0)

        @pl.when(i == 0)
        def _():
            dst[...] = jnp.zeros_like(dst)
            dlb_ref[...] = jnp.zeros_like(dlb_ref)

        lbv = lb_ref[...]
        sg, fg, g, kk = _gates(f_ref[...], lbv)
        k_s[...] = kk
        sg_s[...] = sg
        fg_s[...] = fg
        b_s[...] = _chunk_cumsum(g, reverse)
        srun[...] = sin_ref[...]
        rows = lax.broadcasted_iota(jnp.int32, (SUB, DK), 0)
        r16 = lax.broadcasted_iota(jnp.int32, (SUB, SUB), 0)
        c16 = lax.broadcasted_iota(jnp.int32, (SUB, SUB), 1)
        later = jnp.where((c16 <= r16) if reverse else (c16 >= r16), 1.0, 0.0).astype(F32)

        def recompute(jj, c):
            j = (nsub - 1 - jj) if reverse else jj
            rs = pl.ds(pl.multiple_of(j * SUB, SUB), SUB)
            for h in range(NH):
                sl = slice(h * DK, (h + 1) * DK)
                k, b, v = k_s[rs, sl], b_s[rs, sl], v_ref[rs, sl]
                btot = b[0:1] if reverse else b[SUB - 1:SUB]
                ssub[jj, h] = srun[h]
                srun[h] = srun[h] * jnp.exp(btot) + _tn(_bf(v), _bf(k * jnp.exp(btot - b)))
            return c

        lax.fori_loop(0, nsub, recompute, 0)
        for h in range(NH):
            ssub[nsub, h] = srun[h]

        def back(jj, c):
            pos = nsub - 1 - jj
            j = jj if reverse else pos
            rs = pl.ds(pl.multiple_of(j * SUB, SUB), SUB)
            for h in range(NH):
                sl = slice(h * DK, (h + 1) * DK)
                q, k, b, v, dov = q_ref[rs, sl], k_s[rs, sl], b_s[rs, sl], v_ref[rs, sl], do_ref[rs, sl]
                btot = b[0:1] if reverse else b[SUB - 1:SUB]
                s0 = ssub[pos, h]
                ds = dst[h]
                dg_next = jnp.sum(ds * ssub[pos + 1, h], axis=0, keepdims=True)
                eb = jnp.exp(b)
                ebt = jnp.exp(btot - b)
                ke = k * ebt
                dq = _nn(_bf(dov), _bf(s0)) * eb
                dk = _nn(_bf(v), _bf(ds)) * ebt
                dv = _nt(_bf(ke), _bf(ds))
                for s in range(SUB):
                    dec = _decay(b, s, rows, reverse)
                    dsc = jnp.sum(dov * v[s:s + 1], axis=-1, keepdims=True)
                    qd = q * dec
                    dq = dq + (dsc * dec) * k[s:s + 1]
                    dk_row = jnp.sum(dsc * qd, axis=0, keepdims=True)
                    sc = jnp.sum(qd * k[s:s + 1], axis=-1, keepdims=True)
                    dv_row = jnp.sum(sc * dov, axis=0, keepdims=True)
                    dk = dk + jnp.where(rows == s, dk_row, 0.0)
                    dv = dv + jnp.where(rows == s, dv_row, 0.0)
                dst[h] = ds * jnp.exp(btot) + _tn(_bf(dov), _bf(q * eb))
                dg = jnp.dot(later, q * dq - k * dk, precision=HIGHEST, preferred_element_type=F32) + dg_next
                sgv, fgv, lbh = sg_s[rs, sl], fg_s[rs, sl], lbv[:, sl]
                dfg = jnp.where(fgv > LOG_FLOOR, dg / fgv, 0.0)
                df_ref[rs, sl] = ((1.0 - lbh) * sgv * (1.0 - sgv) * (dfg - dk)).astype(BF16)
                dlb_ref[:, sl] += jnp.sum((dfg - dk) * (1.0 - sgv), axis=0, keepdims=True)
                if add is None:
                    dq_ref[rs, sl] = dq
                    dv_ref[rs, sl] = dv
                else:
                    dq_ref[rs, sl] = (dq + adds[0][rs, sl]).astype(BF16)
                    dv_ref[rs, sl] = (dv + adds[1][rs, sl]).astype(BF16)
            return c

        lax.fori_loop(0, nsub, back, 0)

    seg = lambda col: pl.BlockSpec((TR, AW), lambda i, col=col: (tile(i), col))
    plain = pl.BlockSpec((TR, AW), lambda i: (tile(i), 0))
    operands = [p, p, p, do, sin, lb]
    in_specs = [seg(0), seg(fcol), seg(3), plain,
                pl.BlockSpec((None, NH, DK, DK), lambda i: (tile(i), 0, 0, 0)), _vec_spec(AW)]
    if add is not None:
        operands += list(add)
        in_specs += [plain, plain]
    return pl.pallas_call(
        body, name=name, grid=(nt,), in_specs=in_specs,
        out_specs=[plain, plain, plain, _vec_spec(AW)],
        out_shape=[jax.ShapeDtypeStruct((r, AW), out_dt), jax.ShapeDtypeStruct((r, AW), out_dt),
                   jax.ShapeDtypeStruct((r, AW), BF16), jax.ShapeDtypeStruct((1, AW), F32)],
        scratch_shapes=[pltpu.VMEM((NH, DK, DK), F32), pltpu.VMEM((NH, DK, DK), F32),
                        pltpu.VMEM((nsub + 1, NH, DK, DK), F32)]
        + [pltpu.VMEM((TR, AW), F32)] * 4,
        compiler_params=_cp(56),
    )(*operands)


def _silu(v):
    return v * jax.nn.sigmoid(v)


def readout_fwd(of, ob, p, ng, name):
    r = of.shape[0]

    def body(of_ref, ob_ref, g_ref, ng_ref, y_ref):
        for h in range(NH):
            sl = slice(h * DK, (h + 1) * DK)
            o = of_ref[:, sl] + ob_ref[:, sl]
            on = o * lax.rsqrt(jnp.mean(o * o, axis=-1, keepdims=True) + EPS) * ng_ref[:, sl]
            y_ref[:, sl] = (on * _silu(g_ref[:, sl])).astype(BF16)

    return pl.pallas_call(
        body, name=name, grid=(r // TR,),
        in_specs=[_row_spec(AW), _row_spec(AW), _row_spec(AW, 4), _vec_spec(AW)], out_specs=_row_spec(AW),
        out_shape=jax.ShapeDtypeStruct((r, AW), BF16), compiler_params=_cp(32),
    )(of, ob, p, ng)


def readout_bwd(dy, of, ob, p, ng, name):
    r = of.shape[0]

    def body(dy_ref, of_ref, ob_ref, g_ref, ng_ref, do_ref, dg_ref, dn_ref):
        i = pl.program_id(0)

        @pl.when(i == 0)
        def _():
            dn_ref[...] = jnp.zeros_like(dn_ref)

        for h in range(NH):
            sl = slice(h * DK, (h + 1) * DK)
            o = of_ref[:, sl] + ob_ref[:, sl]
            rstd = lax.rsqrt(jnp.mean(o * o, axis=-1, keepdims=True) + EPS)
            oh = o * rstd
            gv = g_ref[:, sl]
            sig = jax.nn.sigmoid(gv)
            dyv = dy_ref[:, sl]
            don = dyv * (gv * sig)
            dg_ref[:, sl] = (dyv * (oh * ng_ref[:, sl]) * (sig * (1.0 + gv * (1.0 - sig)))).astype(BF16)
            dn_ref[:, sl] += jnp.sum(don * oh, axis=0, keepdims=True)
            doh = don * ng_ref[:, sl]
            do_ref[:, sl] = rstd * (doh - oh * jnp.mean(doh * oh, axis=-1, keepdims=True))

    return pl.pallas_call(
        body, name=name, grid=(r // TR,),
        in_specs=[_row_spec(AW), _row_spec(AW), _row_spec(AW), _row_spec(AW, 4), _vec_spec(AW)],
        out_specs=[_row_spec(AW), _row_spec(AW), _vec_spec(AW)],
        out_shape=[jax.ShapeDtypeStruct((r, AW), F32), jax.ShapeDtypeStruct((r, AW), BF16),
                   jax.ShapeDtypeStruct((1, AW), F32)],
        compiler_params=_cp(32),
    )(dy, of, ob, p, ng)


def _gelu(v):
    return 0.5 * v * (1.0 + lax.erf(v * 0.7071067811865476))


def _gelu_grad(v):
    return 0.5 * (1.0 + lax.erf(v * 0.7071067811865476)) + v * (0.3989422804014327 * jnp.exp(-0.5 * v * v))


def _cmlp_norm(vv, gn):
    vg = _gelu(vv)
    mu = jnp.mean(vg, axis=-1, keepdims=True)
    cen = vg - mu
    rstd = lax.rsqrt(jnp.mean(cen * cen, axis=-1, keepdims=True) + EPS)
    xhat = cen * rstd
    return xhat, rstd, xhat * gn


def chunkmlp_fwd(p, ws, bias, gn, name):
    r = p.shape[0]

    def body(u_ref, v_ref, ws_ref, b_ref, gn_ref, y_ref):
        for ci in range(TR // B_CHUNK):
            rs = slice(ci * B_CHUNK, (ci + 1) * B_CHUNK)
            for gidx in range(NH):
                sl = slice(gidx * DK, (gidx + 1) * DK)
                _, _, vn = _cmlp_norm(v_ref[rs, sl], gn_ref[:, sl])
                mixed = _nn(_bf(ws_ref[gidx]), _bf(vn)) + b_ref[gidx]
                y_ref[rs, sl] = (_gelu(u_ref[rs, sl]) * mixed).astype(BF16)

    return pl.pallas_call(
        body, name=name, grid=(r // TR,),
        in_specs=[_row_spec(AW, 5), _row_spec(AW, 6), pl.BlockSpec((NH, B_CHUNK, B_CHUNK), lambda i: (0, 0, 0)),
                  pl.BlockSpec((NH, B_CHUNK, 1), lambda i: (0, 0, 0)), _vec_spec(AW)],
        out_specs=_row_spec(AW), out_shape=jax.ShapeDtypeStruct((r, AW), BF16), compiler_params=_cp(32),
    )(p, p, ws, bias, gn)


def chunkmlp_bwd(dy, p, ws, bias, gn, name):
    r = p.shape[0]

    def body(dy_ref, u_ref, v_ref, ws_ref, b_ref, gn_ref, du_ref, dv_ref, dws_ref, db_ref, dgn_ref):
        i = pl.program_id(0)

        @pl.when(i == 0)
        def _():
            dws_ref[...] = jnp.zeros_like(dws_ref)
            db_ref[...] = jnp.zeros_like(db_ref)
            dgn_ref[...] = jnp.zeros_like(dgn_ref)

        for ci in range(TR // B_CHUNK):
            rs = slice(ci * B_CHUNK, (ci + 1) * B_CHUNK)
            for gidx in range(NH):
                sl = slice(gidx * DK, (gidx + 1) * DK)
                vv, uv, dyv, gnv = v_ref[rs, sl], u_ref[rs, sl], dy_ref[rs, sl], gn_ref[:, sl]
                xhat, rstd, vn = _cmlp_norm(vv, gnv)
                wg = _bf(ws_ref[gidx])
                mixed = _nn(wg, _bf(vn)) + b_ref[gidx]
                dmixed = dyv * _gelu(uv)
                du_ref[rs, sl] = (dyv * mixed * _gelu_grad(uv)).astype(BF16)
                dws_ref[gidx] += _nt(_bf(dmixed), _bf(vn))
                db_ref[gidx] += jnp.sum(dmixed, axis=-1, keepdims=True)
                dvn = _tn(wg, _bf(dmixed))
                dgn_ref[:, sl] += jnp.sum(dvn * xhat, axis=0, keepdims=True)
                dxh = dvn * gnv
                dvg = rstd * (dxh - jnp.mean(dxh, axis=-1, keepdims=True)
                              - xhat * jnp.mean(dxh * xhat, axis=-1, keepdims=True))
                dv_ref[rs, sl] = (dvg * _gelu_grad(vv)).astype(BF16)

    return pl.pallas_call(
        body, name=name, grid=(r // TR,),
        in_specs=[_row_spec(AW, 1), _row_spec(AW, 5), _row_spec(AW, 6),
                  pl.BlockSpec((NH, B_CHUNK, B_CHUNK), lambda i: (0, 0, 0)),
                  pl.BlockSpec((NH, B_CHUNK, 1), lambda i: (0, 0, 0)), _vec_spec(AW)],
        out_specs=[_row_spec(AW), _row_spec(AW), pl.BlockSpec((NH, B_CHUNK, B_CHUNK), lambda i: (0, 0, 0)),
                   pl.BlockSpec((NH, B_CHUNK, 1), lambda i: (0, 0, 0)), _vec_spec(AW)],
        out_shape=[jax.ShapeDtypeStruct((r, AW), BF16), jax.ShapeDtypeStruct((r, AW), BF16),
                   jax.ShapeDtypeStruct((NH, B_CHUNK, B_CHUNK), F32), jax.ShapeDtypeStruct((NH, B_CHUNK, 1), F32),
                   jax.ShapeDtypeStruct((1, AW), F32)],
        compiler_params=_cp(32),
    )(dy, p, p, ws, bias, gn)


def _win_count(pos, k, n):
    lo = jnp.maximum(pos - k // 2, 0)
    hi = jnp.minimum(pos - k // 2 + k, n)
    return (hi - lo).astype(F32)


POOL_CW = 256
POOL_PAD = (POOL_WINDOWS[-1] // 2) * GRID_W


def pool_op(p, transpose, name):
    r = p.shape[0]
    seq = r - CTX
    grows = seq // GRID_W
    nt = seq // TR

    def body(x_ref, o_ref, y_s):
        j = pl.program_id(0)

        @pl.when(j == 0)
        def _():
            y_s[:POOL_PAD, :] = jnp.zeros((POOL_PAD, POOL_CW), F32)
            y_s[POOL_PAD + seq:, :] = jnp.zeros((POOL_PAD, POOL_CW), F32)

        for gi, k in enumerate(POOL_WINDOWS):
            @pl.when(j // (512 // POOL_CW) == gi)
            def _(k=k):
                offs = list(range(-(k // 2) + 1, k // 2 + 1) if transpose else range(-(k // 2), k // 2))
                tt = lax.broadcasted_iota(jnp.int32, (TR, TR), 0)
                ss = lax.broadcasted_iota(jnp.int32, (TR, TR), 1)
                band = (ss - tt >= offs[0]) & (ss - tt <= offs[-1])
                b_ctx = jnp.where(band, 1.0, 0.0).astype(F32)
                b_grid = jnp.where(band & ((tt >> 6) == (ss >> 6)), 1.0, 0.0).astype(F32)
                trow = lax.broadcasted_iota(jnp.int32, (TR, POOL_CW), 0)

                def count(i):
                    t = i * TR + trow
                    return _win_count(t & (GRID_W - 1), k, GRID_W) * _win_count(t >> 6, k, grows)

                def col_pass(i, carry):
                    xt = x_ref[pl.ds(pl.multiple_of(CTX + i * TR, TR), TR), :]
                    if transpose:
                        xt = xt / count(i)
                    y_s[pl.ds(pl.multiple_of(POOL_PAD + i * TR, TR), TR), :] = jnp.dot(
                        b_grid, xt, precision=HIGHEST, preferred_element_type=F32)
                    return carry

                lax.fori_loop(0, nt, col_pass, 0)

                def row_pass(i, carry):
                    base = POOL_PAD + i * TR
                    acc = y_s[pl.ds(pl.multiple_of(base + offs[0] * GRID_W, GRID_W), TR), :]
                    for d in offs[1:]:
                        acc = acc + y_s[pl.ds(pl.multiple_of(base + d * GRID_W, GRID_W), TR), :]
                    rows = pl.ds(pl.multiple_of(CTX + i * TR, TR), TR)
                    if not transpose:
                        acc = acc / count(i)
                    o_ref[rows, :] = (acc - x_ref[rows, :]).astype(BF16)
                    return carry

                lax.fori_loop(0, nt, row_pass, 0)

                cx = x_ref[:CTX, :]
                cntc = _win_count(trow, k, CTX)
                accc = jnp.dot(b_ctx, cx / cntc if transpose else cx, precision=HIGHEST, preferred_element_type=F32)
                o_ref[:CTX, :] = ((accc if transpose else accc / cntc) - cx).astype(BF16)

    spec = pl.BlockSpec((r, POOL_CW), lambda j: (0, j))
    return pl.pallas_call(
        body, name=name, grid=(D // POOL_CW,), in_specs=[spec], out_specs=spec,
        out_shape=jax.ShapeDtypeStruct((r, D), BF16),
        scratch_shapes=[pltpu.VMEM((seq + 2 * POOL_PAD, POOL_CW), F32)], compiler_params=_cp(56),
    )(p)


def ada_mods(cs, w_ada, b_loc, name):
    nl, _, cl = w_ada.shape
    tn = 1024

    def body(c_ref, w_ref, b_ref, o_ref):
        o_ref[...] = _nn(_bf(_silu(c_ref[...])), _bf(w_ref[...])) + b_ref[...]

    return pl.pallas_call(
        body, name=name, grid=(nl, cl // tn),
        in_specs=[pl.BlockSpec((16, D), lambda l, j: (0, 0)), pl.BlockSpec((None, D, tn), lambda l, j: (l, 0, j)),
                  pl.BlockSpec((None, 1, tn), lambda l, j: (l, 0, j))],
        out_specs=pl.BlockSpec((None, 16, tn), lambda l, j: (l, 0, j)),
        out_shape=jax.ShapeDtypeStruct((nl, 16, cl), F32), compiler_params=_cp(40),
    )(cs, w_ada, b_loc)


def _adamw(w, g, m, v):
    m = ADAM_B1 * m + (1.0 - ADAM_B1) * g
    v = ADAM_B2 * v + (1.0 - ADAM_B2) * (g * g)
    m_hat = m / (1.0 - ADAM_B1 ** ADAM_STEP)
    v_hat = v / (1.0 - ADAM_B2 ** ADAM_STEP)
    delta = -ADAM_LR * (m_hat / (jnp.sqrt(v_hat) + ADAM_EPS) + ADAM_WD * w)
    return delta, m, v


def ada_update(cs, dm, w, m, v, name):
    nl, _, cl = w.shape
    ta, tn = 256, 1024

    def body(c_ref, dm_ref, w_ref, m_ref, v_ref, g_ref, d_ref, nm_ref, nv_ref, dc_ref):
        l, j = pl.program_id(1), pl.program_id(2)
        a = _bf(_silu(c_ref[...]))
        bmat = _bf(dm_ref[...])
        wv = w_ref[...]
        g = _tn(a, bmat)
        g_ref[...] = g
        d_ref[...], nm_ref[...], nv_ref[...] = _adamw(wv, g, m_ref[...], v_ref[...])
        _accum(dc_ref, _nt(bmat, _bf(wv)), (l == 0) & (j == 0))

    wspec = pl.BlockSpec((None, ta, tn), lambda i, l, j: (l, i, j))
    shp = jax.ShapeDtypeStruct(w.shape, F32)
    return pl.pallas_call(
        body, name=name, grid=(D // ta, nl, cl // tn),
        in_specs=[pl.BlockSpec((16, ta), lambda i, l, j: (0, i)),
                  pl.BlockSpec((None, 16, tn), lambda i, l, j: (l, 0, j)), wspec, wspec, wspec],
        out_specs=[wspec, wspec, wspec, wspec, pl.BlockSpec((16, ta), lambda i, l, j: (0, i))],
        out_shape=[shp, shp, shp, shp, jax.ShapeDtypeStruct((16, D), F32)], compiler_params=_cp(40),
    )(cs, dm, w, m, v)


def adamw_big(rb, w, m, v, l, dsts, name):
    nl = w.shape[0]
    cols = w.shape[-1]
    rows = w.size // (cols * nl)
    tr = 128
    w3, m3, v3 = (t.reshape(nl, rows, cols) for t in (w, m, v))
    nd = 0 if dsts is None else 4

    def body(rb_ref, w_ref, m_ref, v_ref, *rest):
        g_ref, d_ref, nm_ref, nv_ref = rest[nd:]
        g = rb_ref[0].astype(F32)
        for dev in range(1, 8):
            g = g + rb_ref[dev].astype(F32)
        g_ref[...] = g
        d_ref[...], nm_ref[...], nv_ref[...] = _adamw(w_ref[...], g, m_ref[...], v_ref[...])

    spec = pl.BlockSpec((None, tr, cols), lambda i: (l, i, 0))
    shp = jax.ShapeDtypeStruct((nl, rows, cols), F32)
    return pl.pallas_call(
        body, name=name, grid=(rows // tr,),
        in_specs=[pl.BlockSpec((8, tr, cols), lambda i: (0, i, 0)), spec, spec, spec] + [ANY] * nd,
        out_specs=[spec] * 4, out_shape=[shp] * 4,
        input_output_aliases={4 + t: t for t in range(nd)}, compiler_params=_cp(48),
    )(rb.reshape(8, rows, cols), w3, m3, v3, *([] if dsts is None else dsts))


def sum8(g8, name):
    n = g8.shape[1]

    def body(g_ref, o_ref):
        acc = g_ref[0]
        for dev in range(1, 8):
            acc = acc + g_ref[dev]
        o_ref[...] = acc

    return pl.pallas_call(
        body, name=name, grid=(1,), in_specs=[pl.BlockSpec((8, n, 128), lambda i: (0, 0, 0))],
        out_specs=pl.BlockSpec((n, 128), lambda i: (0, 0)),
        out_shape=jax.ShapeDtypeStruct((n, 128), F32), compiler_params=_cp(48),
    )(g8)


def adamw_small(g, w, m, v, name):
    def body(g_ref, w_ref, m_ref, v_ref, d_ref, nm_ref, nv_ref):
        d_ref[...], nm_ref[...], nv_ref[...] = _adamw(w_ref[...], g_ref[...], m_ref[...], v_ref[...])

    spec = pl.BlockSpec(g.shape, lambda i: (0, 0))
    shp = jax.ShapeDtypeStruct(g.shape, F32)
    return pl.pallas_call(
        body, name=name, grid=(1,), in_specs=[spec] * 4, out_specs=[spec] * 3, out_shape=[shp] * 3,
        compiler_params=_cp(48),
    )(g, w, m, v)


def _pack(arrs):
    flat = jnp.concatenate([a.reshape(-1).astype(F32) for a in arrs])
    pad = (-flat.shape[0]) % 1024
    return jnp.pad(flat, (0, pad)).reshape(-1, 128)


def _unpack(packed, shapes):
    flat = packed.reshape(-1)
    out, off = [], 0
    for s in shapes:
        n = 1
        for d in s:
            n *= d
        out.append(flat[off:off + n].reshape(s))
        off += n
    return out


def _lower_bounds(lb_logits):
    pr = jax.nn.softmax(lb_logits.astype(F32), axis=1)
    return jnp.cumsum(pr, axis=1) - pr[:, :1]


def device_step(xs, target, mods, sp, weights_for, grads_done):
    lbs = _lower_bounds(sp["lb_logits"])
    saved = []
    for layer in range(4):
        wl, md = weights_for(layer, xs, mods[layer])
        s = {"xs": xs, "w": wl, "md": md}
        h1 = norm_mod(xs, sp["g_norm_mix"][layer][None], md, 0, f"norm_mix{layer}")
        s["h1"] = h1
        if layer % 2 == 0:
            e = layer // 2
            p = mm_cols(h1, wl["in"], 0, f"in_proj{layer}")
            of, sf = hgrn_fwd(p, lbs[0, e][None], False, f"scan_f{layer}")
            ob, sb = hgrn_fwd(p, lbs[1, e][None], True, f"scan_b{layer}")
            ya = readout_fwd(of, ob, p, sp["g_hgrn_out"][e][None], f"readout{layer}")
            yb = chunkmlp_fwd(p, sp["w_spatial"][e], sp["b_spatial"][e][:, :, None], sp["g_spatial_v"][e][None],
                              f"cmlp{layer}")
            ycat = jnp.concatenate([ya, yb], axis=1)
            x1, f1 = mm_rows_res(ycat, wl["out"], 0, xs, md, 2, f"out_proj{layer}")
            s.update(p=p, of=of, ob=ob, sf=sf, sb=sb, ycat=ycat, f1=f1)
        else:
            o = layer // 2
            pp = mm_rows(h1, wl["pin"], 0, f"pool_in{layer}")
            z = pool_op(pp, False, f"pool{layer}")
            x1, ypre = mm_grp_res(z, wl["grp"], 0, sp["b_grp_pool"][o].reshape(1, D),
                                  sp["scale_pool"][o][None], xs, md, 2, f"pool_grp{layer}")
            s.update(z=z, ypre=ypre)
        h2 = norm_mod(x1, sp["g_norm_ffn"][layer][None], md, 3, f"norm_ffn{layer}")
        u = mm_cols(h2, wl["up"], 0, f"ffn_up{layer}", epi=_epi_relu2, out_dtype=BF16)
        x2, f2 = mm_rows_res(u, wl["down"], 0, x1, md, 5, f"ffn_down{layer}")
        s.update(x1=x1, h2=h2, u=u, f2=f2)
        saved.append(s)
        xs = x2

    dx, loss_lanes, dg_final = final_loss(xs, sp["g_norm_final"][None], target, "final_loss")

    token = jnp.zeros((8, 128), F32)
    dmods = [None] * 4
    sg = {"g_norm_final": dg_final[0], "g_norm_mix": [None] * 4, "g_norm_ffn": [None] * 4,
          "dlbs": [[None, None], [None, None]], "g_hgrn_out": [None] * 2, "w_spatial": [None] * 2,
          "b_spatial": [None] * 2, "g_spatial_v": [None] * 2, "b_grp_pool": [None] * 2, "scale_pool": [None] * 2}
    for layer in reversed(range(4)):
        s = saved[layer]
        wl, md = s["w"], s["md"] + token[0, 0]
        df2, dgt2 = gate_in(dx, s["f2"], md, 5, f"gate_ffn{layer}")
        da = mm_t_rows(df2, wl["down"], 0, f"ffn_down_t{layer}", epi=_epi_2sqrt, out_dtype=BF16,
                       extras_of=s["u"], per=2)
        g_down = grad_rows(s["u"], df2, f"g_ffn_down{layer}", per=2, tn=2048)
        dh2 = mm_t_cols(da, wl["up"], 0, f"ffn_up_t{layer}")
        g_up = grad_cols(s["h2"], da, f"g_ffn_up{layer}", per=1)
        dx1, dsh2, dsc2, dgf = normmod_bwd(dh2, s["x1"], dx, sp["g_norm_ffn"][layer][None], md, 3,
                                           f"norm_ffn_b{layer}")
        sg["g_norm_ffn"][layer] = dgf[0]
        if layer % 2 == 0:
            e = layer // 2
            df1, dgt1 = gate_in(dx1, s["f1"], md, 2, f"gate_mix{layer}")
            dycat = mm_t_rows(df1, wl["out"], 0, f"out_proj_t{layer}")
            g_b = grad_rows(s["ycat"], df1, f"g_out_proj{layer}")
            do, dpg, dng = readout_bwd(dycat, s["of"], s["ob"], s["p"], sp["g_hgrn_out"][e][None],
                                       f"readout_b{layer}")
            du, dv, dws, dbs, dgn = chunkmlp_bwd(dycat, s["p"], sp["w_spatial"][e], sp["b_spatial"][e][:, :, None],
                                                 sp["g_spatial_v"][e][None], f"cmlp_b{layer}")
            dq_f, di_f, dff, dlb_f = hgrn_bwd(s["p"], do, s["sf"], lbs[0, e][None], False, f"scan_f_b{layer}")
            dq, di, dfb, dlb_b = hgrn_bwd(s["p"], do, s["sb"], lbs[1, e][None], True, f"scan_b_b{layer}",
                                          add=(dq_f, di_f))
            dp = jnp.concatenate([dq, dff, dfb, di, dpg, du, dv], axis=1)
            dh1 = mm_t_cols(dp, wl["in"], 0, f"in_proj_t{layer}")
            g_a = grad_cols(s["h1"], dp, f"g_in_proj{layer}")
            sg["dlbs"][0][e], sg["dlbs"][1][e] = dlb_f[0], dlb_b[0]
            sg["g_hgrn_out"][e], sg["w_spatial"][e] = dng[0], dws
            sg["b_spatial"][e], sg["g_spatial_v"][e] = dbs[:, :, 0], dgn[0]
        else:
            o = layer // 2
            dyp, dgt1, dscale, dbias = gate_in_pool(dx1, s["ypre"], md, sp["scale_pool"][o][None], 2,
                                                    f"gate_mix{layer}")
            dz = mm_t_grp(dyp, wl["grp"], 0, f"pool_grp_t{layer}")
            g_b = grad_grp(s["z"], dyp, f"g_pool_grp{layer}")
            dpp = pool_op(dz, True, f"pool_t{layer}")
            dh1 = mm_t_rows(dpp, wl["pin"], 0, f"pool_in_t{layer}")
            g_a = grad_rows(s["h1"], dpp, f"g_pool_in{layer}")
            sg["b_grp_pool"][o], sg["scale_pool"][o] = dbias[0].reshape(4, 512), dscale[0]
        dx, dsh1, dsc1, dgm = normmod_bwd(dh1, s["xs"], dx1, sp["g_norm_mix"][layer][None], md, 0,
                                          f"norm_mix_b{layer}")
        sg["g_norm_mix"][layer] = dgm[0]
        dmods[layer] = jnp.concatenate([dsh1, dsc1, dgt1, dsh2, dsc2, dgt2], axis=1)
        token = grads_done(layer, [g_a, g_b, g_up, g_down], dx)
    return loss_lanes, dx, dmods, sg


BIG = ("w_in_even", "w_out_even", "w_in_pool", "w_grp_pool", "w_ffn_up", "w_ffn_down")
SMALL = ("b_ada", "g_norm_mix", "g_norm_ffn", "lb_logits", "g_hgrn_out", "w_spatial", "b_spatial", "g_spatial_v",
         "b_grp_pool", "scale_pool", "g_norm_final")
WEIGHTS = ("c_ctx", "w_ada", "b_ada", "g_norm_mix", "g_norm_ffn", "w_in_even", "w_out_even", "lb_logits",
           "g_hgrn_out", "w_spatial", "b_spatial", "g_spatial_v", "w_in_pool", "w_grp_pool", "b_grp_pool",
           "scale_pool", "w_ffn_up", "w_ffn_down", "g_norm_final")


def kernel(x, c, ctx, c_ctx, w_ada, b_ada, g_norm_mix, g_norm_ffn, w_in_even, w_out_even, lb_logits, g_hgrn_out, w_spatial, b_spatial, g_spatial_v, w_in_pool, w_grp_pool, b_grp_pool, scale_pool, w_ffn_up, w_ffn_down, g_norm_final, loss_target, m_c_ctx, m_w_ada, m_b_ada, m_g_norm_mix, m_g_norm_ffn, m_w_in_even, m_w_out_even, m_lb_logits, m_g_hgrn_out, m_w_spatial, m_b_spatial, m_g_spatial_v, m_w_in_pool, m_w_grp_pool, m_b_grp_pool, m_scale_pool, m_w_ffn_up, m_w_ffn_down, m_g_norm_final, v_c_ctx, v_w_ada, v_b_ada, v_g_norm_mix, v_g_norm_ffn, v_w_in_even, v_w_out_even, v_lb_logits, v_g_hgrn_out, v_w_spatial, v_b_spatial, v_g_spatial_v, v_w_in_pool, v_w_grp_pool, v_b_grp_pool, v_scale_pool, v_w_ffn_up, v_w_ffn_down, v_g_norm_final):
    loc = dict(c_ctx=c_ctx, w_ada=w_ada, b_ada=b_ada, g_norm_mix=g_norm_mix, g_norm_ffn=g_norm_ffn,
               w_in_even=w_in_even, w_out_even=w_out_even, lb_logits=lb_logits, g_hgrn_out=g_hgrn_out,
               w_spatial=w_spatial, b_spatial=b_spatial, g_spatial_v=g_spatial_v, w_in_pool=w_in_pool,
               w_grp_pool=w_grp_pool, b_grp_pool=b_grp_pool, scale_pool=scale_pool, w_ffn_up=w_ffn_up,
               w_ffn_down=w_ffn_down, g_norm_final=g_norm_final)
    mom = dict(c_ctx=m_c_ctx, w_ada=m_w_ada, b_ada=m_b_ada, g_norm_mix=m_g_norm_mix, g_norm_ffn=m_g_norm_ffn,
               w_in_even=m_w_in_even, w_out_even=m_w_out_even, lb_logits=m_lb_logits, g_hgrn_out=m_g_hgrn_out,
               w_spatial=m_w_spatial, b_spatial=m_b_spatial, g_spatial_v=m_g_spatial_v, w_in_pool=m_w_in_pool,
               w_grp_pool=m_w_grp_pool, b_grp_pool=m_b_grp_pool, scale_pool=m_scale_pool, w_ffn_up=m_w_ffn_up,
               w_ffn_down=m_w_ffn_down, g_norm_final=m_g_norm_final)
    var = dict(c_ctx=v_c_ctx, w_ada=v_w_ada, b_ada=v_b_ada, g_norm_mix=v_g_norm_mix, g_norm_ffn=v_g_norm_ffn,
               w_in_even=v_w_in_even, w_out_even=v_w_out_even, lb_logits=v_lb_logits, g_hgrn_out=v_g_hgrn_out,
               w_spatial=v_w_spatial, b_spatial=v_b_spatial, g_spatial_v=v_g_spatial_v, w_in_pool=v_w_in_pool,
               w_grp_pool=v_w_grp_pool, b_grp_pool=v_b_grp_pool, scale_pool=v_scale_pool, w_ffn_up=v_w_ffn_up,
               w_ffn_down=v_w_ffn_down, g_norm_final=v_g_norm_final)
    mx, my, mc = _mesh_pos()
    chip = 2 * mx + my
    dev = 2 * chip + mc

    def layer_tensors(layer):
        i = layer // 2
        pair = ("w_in_even", "w_out_even") if layer % 2 == 0 else ("w_in_pool", "w_grp_pool")
        return [(pair[0], i), (pair[1], i), ("w_ffn_up", layer), ("w_ffn_down", layer)]

    def start_gather(layer, after):
        srcs = [loc[n][i][None].astype(BF16) for n, i in layer_tensors(layer)]
        lands = [lax.dynamic_update_slice(jnp.zeros((NSH,) + s.shape, BF16), s[None], (chip,) + (0,) * s.ndim)
                 for s in srcs]
        return ici_start(srcs, lands, after, False, f"gather_start{layer}")

    gathers = {0: start_gather(0, c)}

    hello = allgather8(_pack([c[0], lb_logits, b_grp_pool, scale_pool]), "gather_small")
    parts = [_unpack(hello[d], [(D,), (2, 2, 256), (2, 4, 128), (2, 512)]) for d in range(8)]
    cs = jnp.concatenate([jnp.stack([parts[d][0] for d in range(8)]), c_ctx[None], jnp.zeros((7, D), F32)])
    chips_of = [parts[2 * s] for s in range(NSH)]
    sp = dict(loc)
    sp["lb_logits"] = jnp.concatenate([q[1] for q in chips_of], axis=2)
    sp["b_grp_pool"] = jnp.concatenate([q[2] for q in chips_of], axis=2)
    sp["scale_pool"] = jnp.concatenate([q[3] for q in chips_of], axis=1)

    b_loc = lax.dynamic_slice_in_dim(b_ada, chip * 3072, 3072, axis=1)[:, None, :]
    mods_loc = ada_mods(cs, w_ada, b_loc, "ada_mods")
    mods_all = allgather8(mods_loc.reshape(-1, 128), "gather_mods").reshape(8, 4, 16, 3072)
    mods_full = jnp.concatenate([mods_all[2 * s] for s in range(NSH)], axis=2)
    mine = lax.dynamic_index_in_dim(mods_full, dev, axis=1, keepdims=False)
    mods = [jnp.stack([mods_full[l, 8].reshape(6, D), mine[l].reshape(6, D)]) for l in range(4)]

    def weights_for(layer, xs_l, mod):
        _, got = ici_wait(gathers[layer], xs_l, False, f"gather_wait{layer}")
        if layer < 3:
            gathers[layer + 1] = start_gather(layer + 1, got[0])
            mod = mod + gathers[layer + 1][4][0, 0]
        keys = ("in", "out", "up", "down") if layer % 2 == 0 else ("pin", "grp", "up", "down")
        return dict(zip(keys, got)), mod

    exchanges = {}

    def grads_done(layer, gs, dx_l):
        lands = [lax.empty((8,) + g.shape[1:], BF16) for g in gs]
        exchanges[layer] = ici_start(gs, lands, dx_l, True, f"exchange_start{layer}")
        return exchanges[layer][4]

    xs = jnp.concatenate([ctx[0], x[0]], axis=0)
    loss_lanes, dxs, dmods, sg = device_step(xs, loss_target[0], mods, sp, weights_for, grads_done)
    grad_x = dxs[CTX:][None]

    out = {}

    def finish_layer(layer, after):
        gs, rbs = ici_wait(exchanges[layer], after, True, f"exchange_wait{layer}")
        rbs = forward_to_sibling(gs, rbs, f"exchange_d2d{layer}")
        for (n, i), rb in zip(layer_tensors(layer), rbs):
            out[n] = adamw_big(rb, loc[n], mom[n], var[n], i, out.get(n), f"adamw_{n}{i}")

    for layer in (3, 2, 1):
        finish_layer(layer, dxs)

    dm_lat = jnp.stack([dmods[l][1].reshape(6 * D) for l in range(4)])
    dm_ctx = jnp.stack([dmods[l][0].reshape(6 * D) for l in range(4)])
    small_shapes = [(4, 6 * D), (4, 6 * D), (4, D), (4, D), (2, 2, AW), (2, AW), (2, NH, 128, 128), (2, NH, 128),
                    (2, AW), (2, 4, 512), (2, D), (D,), (128,)]
    mine_small = _pack([dm_lat, dm_ctx, jnp.stack(sg["g_norm_mix"]), jnp.stack(sg["g_norm_ffn"]),
                        jnp.stack([jnp.stack(sg["dlbs"][0]), jnp.stack(sg["dlbs"][1])]),
                        jnp.stack(sg["g_hgrn_out"]), jnp.stack(sg["w_spatial"]), jnp.stack(sg["b_spatial"]),
                        jnp.stack(sg["g_spatial_v"]), jnp.stack(sg["b_grp_pool"]), jnp.stack(sg["scale_pool"]),
                        sg["g_norm_final"], loss_lanes[0]])
    all_small = allgather8(mine_small, "gather_small_grads")
    tot = _unpack(sum8(all_small, "sum_small_grads"), small_shapes)
    (_, dm_ctx_tot, g_mix, g_ffn, dlbs, g_hg, g_ws, g_bs, g_gv, g_bg, g_sc, g_fin, loss_v) = tot
    loss = loss_v[0]
    dm_lat_all = jnp.stack([_unpack(all_small[d], small_shapes[:1])[0] for d in range(8)])
    g_b_ada = jnp.sum(dm_lat_all, axis=0) + dm_ctx_tot
    _, lb_vjp = jax.vjp(_lower_bounds, sp["lb_logits"])
    g_lb_full = lb_vjp(dlbs)[0]
    grads = {"b_ada": g_b_ada, "g_norm_mix": g_mix, "g_norm_ffn": g_ffn,
             "lb_logits": lax.dynamic_slice_in_dim(g_lb_full, chip * 256, 256, axis=2),
             "g_hgrn_out": g_hg, "w_spatial": g_ws, "b_spatial": g_bs, "g_spatial_v": g_gv,
             "b_grp_pool": lax.dynamic_slice_in_dim(g_bg, chip * 128, 128, axis=2),
             "scale_pool": lax.dynamic_slice_in_dim(g_sc, chip * 512, 512, axis=1), "g_norm_final": g_fin}

    dm_rows = jnp.concatenate([dm_lat_all.transpose(1, 0, 2), dm_ctx_tot[:, None, :], jnp.zeros((4, 7, 6 * D), F32)],
                              axis=1)
    dm_loc = lax.dynamic_slice_in_dim(dm_rows, chip * 3072, 3072, axis=2)
    g_wa, d_wa, nm_wa, nv_wa, dc_part = ada_update(cs, dm_loc, w_ada, m_w_ada, v_w_ada, "ada_update")
    out["w_ada"] = [g_wa, d_wa, nm_wa, nv_wa]
    dc_all = allgather8(dc_part[8].reshape(16, 128), "gather_dc")
    dpre = dc_all[0] + dc_all[2] + dc_all[4] + dc_all[6]
    sig = jax.nn.sigmoid(c_ctx)
    grads["c_ctx"] = dpre.reshape(D) * (sig * (1.0 + c_ctx * (1.0 - sig)))

    names = ("c_ctx",) + SMALL
    shapes = [loc[n].shape for n in names]
    d_s, nm_s, nv_s = adamw_small(_pack([grads[n] for n in names]), _pack([loc[n] for n in names]),
                                  _pack([mom[n] for n in names]), _pack([var[n] for n in names]), "adamw_small")
    for n, dl, nm, nv in zip(names, _unpack(d_s, shapes), _unpack(nm_s, shapes), _unpack(nv_s, shapes)):
        out[n] = [grads[n], dl, nm, nv]

    finish_layer(0, d_s)
    for n in BIG:
        out[n] = [t.reshape(loc[n].shape) for t in out[n]]

    return (loss, grad_x, *[out[n][0] for n in WEIGHTS], *[out[n][1] for n in WEIGHTS],
            *[out[n][2] for n in WEIGHTS], *[out[n][3] for n in WEIGHTS])
```

```python
import functools

import jax
import jax.numpy as jnp
from jax import lax
from jax.experimental import pallas as pl
from jax.experimental.pallas import tpu as pltpu

F32 = jnp.float32
BF16 = jnp.bfloat16
MESH = pl.DeviceIdType.MESH
ANY = pl.BlockSpec(memory_space=pl.ANY)
HIGHEST = lax.Precision.HIGHEST

D = 2048
DFF = 8192
CTX = 256
TR = 256
GRID_W = 64
EPS = 1e-6
LOG_FLOOR = 1e-30
NH = 8
DK = 128
SUB = 8
B_CHUNK = 128
AW = NH * DK
POOL_WINDOWS = (2, 4, 8, 16)
NSH = 4
ADAM_LR, ADAM_B1, ADAM_B2, ADAM_EPS, ADAM_WD, ADAM_STEP = 0.001, 0.9, 0.999, 1e-08, 0.01, 10
MIB = 1024 * 1024


def _cp(vmem_mib):
    return pltpu.CompilerParams(vmem_limit_bytes=vmem_mib * MIB)


def _bf(v):
    return v.astype(BF16)


def _nn(a, b):
    return lax.dot_general(a, b, (((1,), (0,)), ((), ())), preferred_element_type=F32)


def _nt(a, b):
    return lax.dot_general(a, b, (((1,), (1,)), ((), ())), preferred_element_type=F32)


def _tn(a, b):
    return lax.dot_general(a, b, (((0,), (0,)), ((), ())), preferred_element_type=F32)


def _mesh_pos():
    return lax.axis_index("x"), lax.axis_index("y"), lax.axis_index("c")


HBM = pl.BlockSpec(memory_space=pltpu.HBM)
SEM = pl.BlockSpec(memory_space=pltpu.SEMAPHORE)
EFFECT = pltpu.SideEffectType.DATAFLOW_SIDE_EFFECTING


def _peer_copies(exchange, srcs, lands, send_sems, recv_sems):
    x, y, c = _mesh_pos()
    me = 2 * x + y
    pairs = []
    for t in range(len(srcs)):
        for j, (px, py) in enumerate([(1 - x, y), (x, 1 - y), (1 - x, 1 - y)]):
            peer = 2 * px + py
            src = srcs[t].at[peer] if exchange else srcs[t]
            out_slot, in_slot = (2 * me + c, 2 * peer + c) if exchange else (me, peer)

            def mk(slot, t=t, j=j, px=px, py=py, src=src):
                return pltpu.make_async_remote_copy(
                    src_ref=src, dst_ref=lands[t].at[slot], send_sem=send_sems.at[3 * t + j],
                    recv_sem=recv_sems.at[3 * t + j], device_id=(px, py, c), device_id_type=MESH)

            pairs.append((mk(out_slot), mk(in_slot)))
    return pairs


def ici_start(srcs, lands, after, exchange, name):
    n = len(srcs)

    def body(*refs):
        send_sems, recv_sems = refs[2 * n + 1], refs[2 * n + 2]
        for out_copy, _ in _peer_copies(exchange, refs[:n], refs[n:2 * n], send_sems, recv_sems):
            out_copy.start()
        refs[-1][...] = jnp.zeros_like(refs[-1])

    arrs = list(srcs) + list(lands)
    res = pl.pallas_call(
        body, name=name,
        out_shape=(pltpu.SemaphoreType.DMA((3 * n,)), pltpu.SemaphoreType.DMA((3 * n,)),
                   *[pltpu.HBM(a.shape, a.dtype) for a in arrs], jax.ShapeDtypeStruct((8, 128), F32)),
        in_specs=[HBM] * (2 * n) + [ANY],
        out_specs=(SEM, SEM, *[HBM] * (2 * n), pl.BlockSpec(memory_space=pltpu.VMEM)),
        input_output_aliases={t: 2 + t for t in range(2 * n)},
        compiler_params=pltpu.CompilerParams(has_side_effects=EFFECT),
    )(*[pltpu.with_memory_space_constraint(a, pltpu.HBM) for a in arrs], after)
    return res[0], res[1], list(res[2:2 + n]), list(res[2 + n:2 + 2 * n]), res[-1]


def ici_wait(started, after, exchange, name):
    send_sems, recv_sems, srcs, lands, _ = started
    n = len(srcs)

    def body(*refs):
        for out_copy, in_copy in _peer_copies(exchange, refs[:n], refs[n:2 * n], refs[2 * n], refs[2 * n + 1]):
            out_copy.wait_send()
            in_copy.wait_recv()

    arrs = list(srcs) + list(lands)
    res = pl.pallas_call(
        body, name=name,
        out_shape=tuple(pltpu.HBM(a.shape, a.dtype) for a in arrs),
        in_specs=[HBM] * (2 * n) + [SEM, SEM, ANY], out_specs=tuple([HBM] * (2 * n)),
        input_output_aliases={t: t for t in range(2 * n)},
        compiler_params=pltpu.CompilerParams(has_side_effects=EFFECT),
    )(*arrs, send_sems, recv_sems, after)
    return list(res[:n]), list(res[n:])


def sum_blocks(g, rb, name):
    cols = g.shape[-1]
    rows = g.size // (NSH * cols)
    tr = min(rows, 256)

    def body(g_ref, r1_ref, r2_ref, r3_ref, o_ref):
        acc = g_ref[...].astype(F32) + r1_ref[...].astype(F32) + r2_ref[...].astype(F32) + r3_ref[...].astype(F32)
        o_ref[...] = acc.astype(BF16)

    def mine(i):
        x, y, _ = _mesh_pos()
        return (2 * x + y, i, 0)

    def peer(fx, fy):
        def index(i):
            x, y, c = _mesh_pos()
            return (2 * (2 * (x ^ fx) + (y ^ fy)) + c, i, 0)
        return pl.BlockSpec((None, tr, cols), index)

    return pl.pallas_call(
        body, name=name, grid=(rows // tr,),
        in_specs=[pl.BlockSpec((None, tr, cols), mine), peer(1, 0), peer(0, 1), peer(1, 1)],
        out_specs=pl.BlockSpec((tr, cols), lambda i: (i, 0)),
        out_shape=jax.ShapeDtypeStruct((rows, cols), BF16), compiler_params=_cp(32),
    )(g.reshape(NSH, rows, cols), *[rb.reshape(8, rows, cols)] * 3)


def swap_sibling(ps, name):
    n = len(ps)

    def body(*refs):
        ins, outs, send_sems, recv_sems = refs[:n], refs[n:2 * n], refs[2 * n], refs[2 * n + 1]
        x, y, c = _mesh_pos()
        copies = [pltpu.make_async_remote_copy(
            src_ref=ins[t], dst_ref=outs[t], send_sem=send_sems.at[t], recv_sem=recv_sems.at[t],
            device_id=(x, y, 1 - c), device_id_type=MESH) for t in range(n)]
        for cp in copies:
            cp.start()
        for cp in copies:
            cp.wait_recv()
            cp.wait_send()

    return pl.pallas_call(
        body, name=name, out_shape=[jax.ShapeDtypeStruct(p.shape, p.dtype) for p in ps],
        in_specs=[ANY] * n, out_specs=[ANY] * n,
        scratch_shapes=[pltpu.SemaphoreType.DMA((n,)), pltpu.SemaphoreType.DMA((n,))],
    )(*ps)


def allgather8(v, name):
    m, n = v.shape

    def body(x_ref, out_ref, send_sems, recv_sems, local_sem):
        x, y, c = _mesh_pos()
        me, sibling = (x, y, c), (x, y, 1 - c)
        chips = [(1 - x, y), (x, 1 - y), (1 - x, 1 - y)]

        def rows(px, py, pc):
            return out_ref.at[4 * px + 2 * py + pc]

        def copy(k, block, to, src=None):
            return pltpu.make_async_remote_copy(
                src_ref=rows(*block) if src is None else src, dst_ref=rows(*block),
                send_sem=send_sems.at[k], recv_sem=recv_sems.at[k], device_id=to, device_id_type=MESH)

        mine = pltpu.make_async_copy(x_ref, rows(*me), local_sem)
        mine.start()
        first = [copy(0, me, sibling, src=x_ref)]
        first += [copy(1 + j, me, (*chip, c), src=x_ref) for j, chip in enumerate(chips)]
        for cp in first:
            cp.start()
        passed = [copy(4 + j, (*chip, c), sibling) for j, chip in enumerate(chips)]
        for j, chip in enumerate(chips):
            copy(1 + j, (*chip, c), me).wait_recv()
            passed[j].start()
        copy(0, sibling, me).wait_recv()
        for j, chip in enumerate(chips):
            copy(4 + j, (*chip, 1 - c), me).wait_recv()
        for cp in first + passed:
            cp.wait_send()
        mine.wait()

    return pl.pallas_call(
        body, name=name,
        out_shape=jax.ShapeDtypeStruct((8, m, n), v.dtype),
        in_specs=[pl.BlockSpec(memory_space=pltpu.VMEM)],
        out_specs=pl.BlockSpec(memory_space=pltpu.VMEM),
        scratch_shapes=[pltpu.SemaphoreType.DMA((7,)), pltpu.SemaphoreType.DMA((7,)), pltpu.SemaphoreType.DMA],
        compiler_params=_cp(40),
    )(v)


def _row_spec(width=D, off=0):
    return pl.BlockSpec((TR, width), lambda i, off=off: (i, off))


def _vec_spec(width=D):
    return pl.BlockSpec((1, width), lambda i: (0, 0))


def _mod_spec():
    return pl.BlockSpec((None, 6, D), lambda i: (jnp.minimum(i, 1), 0, 0))


def _pair_spec(width=D):
    return pl.BlockSpec((None, 1, width), lambda i: (jnp.minimum(i, 1), 0, 0))


def _accum(ref, val, first):
    @pl.when(first)
    def _():
        ref[...] = val

    @pl.when(jnp.logical_not(first))
    def _():
        ref[...] += val


def norm_mod(xs, g, mod, si, name):
    r = xs.shape[0]

    def body(x_ref, g_ref, m_ref, o_ref):
        x = x_ref[...]
        rstd = lax.rsqrt(jnp.mean(x * x, axis=-1, keepdims=True) + EPS)
        n = x * rstd * g_ref[...]
        o_ref[...] = (n * (1.0 + m_ref[si + 1:si + 2, :]) + m_ref[si:si + 1, :]).astype(BF16)

    return pl.pallas_call(
        body, name=name, grid=(r // TR,),
        in_specs=[_row_spec(), _vec_spec(), _mod_spec()], out_specs=_row_spec(),
        out_shape=jax.ShapeDtypeStruct((r, D), BF16), compiler_params=_cp(32),
    )(xs, g, mod)


def gate_in(dx, f, mod, gi, name):
    r = dx.shape[0]

    def body(dx_ref, f_ref, m_ref, o_ref, dg_ref):
        i = pl.program_id(0)
        dxv = dx_ref[...]
        o_ref[...] = (dxv * m_ref[gi:gi + 1, :]).astype(BF16)
        _accum(dg_ref, jnp.sum(dxv * f_ref[...].astype(F32), axis=0, keepdims=True), i <= 1)

    return pl.pallas_call(
        body, name=name, grid=(r // TR,),
        in_specs=[_row_spec(), _row_spec(), _mod_spec()], out_specs=[_row_spec(), _pair_spec()],
        out_shape=[jax.ShapeDtypeStruct((r, D), BF16), jax.ShapeDtypeStruct((2, 1, D), F32)],
        compiler_params=_cp(32),
    )(dx, f, mod)


def gate_in_pool(dx, ypre, mod, scale, gi, name):
    r = dx.shape[0]

    def body(dx_ref, y_ref, m_ref, s_ref, o_ref, dg_ref, ds_ref, db_ref):
        i = pl.program_id(0)
        dxv = dx_ref[...]
        yp = y_ref[...].astype(F32)
        sc = s_ref[...]
        dy = dxv * m_ref[gi:gi + 1, :]
        dyp = dy * sc
        o_ref[...] = dyp.astype(BF16)
        _accum(dg_ref, jnp.sum(dxv * (yp * sc), axis=0, keepdims=True), i <= 1)
        _accum(ds_ref, jnp.sum(dy * yp, axis=0, keepdims=True), i == 0)
        _accum(db_ref, jnp.sum(dyp, axis=0, keepdims=True), i == 0)

    return pl.pallas_call(
        body, name=name, grid=(r // TR,),
        in_specs=[_row_spec(), _row_spec(), _mod_spec(), _vec_spec()],
        out_specs=[_row_spec(), _pair_spec(), _vec_spec(), _vec_spec()],
        out_shape=[jax.ShapeDtypeStruct((r, D), BF16), jax.ShapeDtypeStruct((2, 1, D), F32),
                   jax.ShapeDtypeStruct((1, D), F32), jax.ShapeDtypeStruct((1, D), F32)],
        compiler_params=_cp(32),
    )(dx, ypre, mod, scale)


def normmod_bwd(dh, x, dxo, g, mod, si, name):
    r = x.shape[0]

    def body(dh_ref, x_ref, dxo_ref, g_ref, m_ref, dx_ref, dsh_ref, dsc_ref, dg_ref):
        i = pl.program_id(0)
        xv = x_ref[...]
        dhv = dh_ref[...]
        gv = g_ref[...]
        rstd = lax.rsqrt(jnp.mean(xv * xv, axis=-1, keepdims=True) + EPS)
        xhat = xv * rstd
        dn = dhv * (1.0 + m_ref[si + 1:si + 2, :])
        dxh = dn * gv
        dx_ref[...] = rstd * (dxh - xhat * jnp.mean(dxh * xhat, axis=-1, keepdims=True)) + dxo_ref[...]
        _accum(dsh_ref, jnp.sum(dhv, axis=0, keepdims=True), i <= 1)
        _accum(dsc_ref, jnp.sum(dhv * (xhat * gv), axis=0, keepdims=True), i <= 1)
        _accum(dg_ref, jnp.sum(dn * xhat, axis=0, keepdims=True), i == 0)

    return pl.pallas_call(
        body, name=name, grid=(r // TR,),
        in_specs=[_row_spec(), _row_spec(), _row_spec(), _vec_spec(), _mod_spec()],
        out_specs=[_row_spec(), _pair_spec(), _pair_spec(), _vec_spec()],
        out_shape=[jax.ShapeDtypeStruct((r, D), F32), jax.ShapeDtypeStruct((2, 1, D), F32),
                   jax.ShapeDtypeStruct((2, 1, D), F32), jax.ShapeDtypeStruct((1, D), F32)],
        compiler_params=_cp(48),
    )(dh, x, dxo, g, mod)


def final_loss(xs, g, target, name):
    r = xs.shape[0]

    def body(x_ref, g_ref, t_ref, dx_ref, loss_ref, dg_ref):
        i = pl.program_id(0)

        @pl.when(i == 0)
        def _():
            dx_ref[...] = jnp.zeros_like(dx_ref)
            loss_ref[...] = jnp.zeros_like(loss_ref)
            dg_ref[...] = jnp.zeros_like(dg_ref)

        @pl.when(i > 0)
        def _():
            xv = x_ref[...]
            gv = g_ref[...]
            rstd = lax.rsqrt(jnp.mean(xv * xv, axis=-1, keepdims=True) + EPS)
            xhat = xv * rstd
            err = xhat * gv - t_ref[...]
            part = 0.5 * jnp.sum(jnp.mean(err * err, axis=-1, keepdims=True), axis=0, keepdims=True)
            lane = lax.broadcasted_iota(jnp.int32, (1, 128), 1)
            loss_ref[...] += jnp.where(lane == 0, part, 0.0)
            dy = err * (1.0 / D)
            dg_ref[...] += jnp.sum(dy * xhat, axis=0, keepdims=True)
            dxh = dy * gv
            dx_ref[...] = rstd * (dxh - xhat * jnp.mean(dxh * xhat, axis=-1, keepdims=True))

    return pl.pallas_call(
        body, name=name, grid=(r // TR,),
        in_specs=[_row_spec(), _vec_spec(), pl.BlockSpec((TR, D), lambda i: (jnp.maximum(i - 1, 0), 0))],
        out_specs=[_row_spec(), pl.BlockSpec((1, 128), lambda i: (0, 0)), _vec_spec()],
        out_shape=[jax.ShapeDtypeStruct((r, D), F32), jax.ShapeDtypeStruct((1, 128), F32),
                   jax.ShapeDtypeStruct((1, D), F32)],
        compiler_params=_cp(32),
    )(xs, g, target)


def _mm(name, mode, a, b, grid, a_spec, b_spec, out_shapes, out_specs, epi, kaxis=None, acc=None,
        extras=(), vmem=48):
    ne, no = len(extras), len(out_shapes)
    dot = {"nn": _nn, "nt": _nt, "tn": _tn}[mode]
    nk = grid[kaxis] if kaxis is not None else 1

    def body(*refs):
        a_ref, b_ref = refs[0], refs[1]
        ex = refs[2:2 + ne]
        outs = refs[2 + ne:2 + ne + no]
        ids = [pl.program_id(ax) for ax in range(len(grid))]
        part = dot(a_ref[...], b_ref[...])
        if kaxis is None:
            epi(part, ids, ex, outs)
        else:
            acc_ref = refs[-1]
            k = ids[kaxis]

            @pl.when(k == 0)
            def _():
                acc_ref[...] = part

            @pl.when(k > 0)
            def _():
                acc_ref[...] += part

            @pl.when(k == nk - 1)
            def _():
                epi(acc_ref[...], ids, ex, outs)

    operands = [a, b] + [e[0] for e in extras]
    in_specs = [a_spec, b_spec] + [e[1] for e in extras]
    return pl.pallas_call(
        body, name=name, grid=grid, in_specs=in_specs, out_specs=out_specs, out_shape=out_shapes,
        scratch_shapes=[] if kaxis is None else [pltpu.VMEM(acc, F32)], compiler_params=_cp(vmem),
    )(*operands)


def _epi_store(acc, ids, ex, outs):
    outs[0][...] = acc.astype(outs[0].dtype)


def _epi_relu2(acc, ids, ex, outs):
    rl = jnp.maximum(acc, 0.0)
    outs[0][...] = (rl * rl).astype(BF16)


def _epi_2sqrt(acc, ids, ex, outs):
    outs[0][...] = (acc * (2.0 * jnp.sqrt(ex[0][...].astype(F32)))).astype(BF16)


def _gate_rows(ids, tm, shape, mod_ref, gi):
    rid = ids[0] * tm + lax.broadcasted_iota(jnp.int32, shape, 0)
    return jnp.where(rid < CTX, mod_ref[0, gi:gi + 1, :], mod_ref[1, gi:gi + 1, :])


def _epi_res(gi, tm):
    def epi(acc, ids, ex, outs):
        outs[0][...] = ex[0][...] + _gate_rows(ids, tm, acc.shape, ex[1], gi) * acc
        outs[1][...] = acc.astype(BF16)
    return epi


def _epi_pool(gi, tm):
    def epi(acc, ids, ex, outs):
        ypre = acc + ex[2][...]
        outs[0][...] = ex[0][...] + _gate_rows(ids, tm, acc.shape, ex[1], gi) * (ypre * ex[3][...])
        outs[1][...] = ypre.astype(BF16)
    return epi


def _tm(r):
    return 768 if r % 768 == 0 else TR


def mm_cols(a, w, l, name, epi=_epi_store, out_dtype=F32, per=2):
    r, k = a.shape
    c = w.shape[3]
    tm, tn = _tm(r), c // per
    return _mm(name, "nn", a, w, (r // tm, NSH * per),
               pl.BlockSpec((tm, k), lambda i, j: (i, 0)),
               pl.BlockSpec((None, None, k, tn), lambda i, j: (j // per, l, 0, j % per)),
               [jax.ShapeDtypeStruct((r, NSH * c), out_dtype)], [pl.BlockSpec((tm, tn), lambda i, j: (i, j))],
               epi, vmem=56)[0]


def mm_rows_res(a, w, l, res, mod, gi, name):
    r = a.shape[0]
    kc = w.shape[2]
    tm, tn = _tm(r), 1024
    return _mm(name, "nn", a, w, (r // tm, D // tn, NSH),
               pl.BlockSpec((tm, kc), lambda i, j, k: (i, k)),
               pl.BlockSpec((None, None, kc, tn), lambda i, j, k: (k, l, 0, j)),
               [jax.ShapeDtypeStruct((r, D), F32), jax.ShapeDtypeStruct((r, D), BF16)],
               [pl.BlockSpec((tm, tn), lambda i, j, k: (i, j))] * 2,
               _epi_res(gi, tm), kaxis=2, acc=(tm, tn),
               extras=[(res, pl.BlockSpec((tm, tn), lambda i, j, k: (i, j))),
                       (mod, pl.BlockSpec((2, 6, tn), lambda i, j, k: (0, 0, j)))], vmem=56)


def mm_rows(a, w, l, name):
    r = a.shape[0]
    kc = w.shape[2]
    tm, tn = _tm(r), 1024
    return _mm(name, "nn", a, w, (r // tm, D // tn, NSH),
               pl.BlockSpec((tm, kc), lambda i, j, k: (i, k)),
               pl.BlockSpec((None, None, kc, tn), lambda i, j, k: (k, l, 0, j)),
               [jax.ShapeDtypeStruct((r, D), F32)], [pl.BlockSpec((tm, tn), lambda i, j, k: (i, j))],
               _epi_store, kaxis=2, acc=(tm, tn), vmem=56)[0]


def mm_grp_res(z, w, o, bias, scale, res, mod, gi, name):
    r = z.shape[0]
    tm = _tm(r)
    return _mm(name, "nn", z, w, (r // tm, 4, NSH),
               pl.BlockSpec((tm, 128), lambda i, g, k: (i, 4 * g + k)),
               pl.BlockSpec((None, None, None, 128, 512), lambda i, g, k: (k, o, g, 0, 0)),
               [jax.ShapeDtypeStruct((r, D), F32), jax.ShapeDtypeStruct((r, D), BF16)],
               [pl.BlockSpec((tm, 512), lambda i, g, k: (i, g))] * 2,
               _epi_pool(gi, tm), kaxis=2, acc=(tm, 512),
               extras=[(res, pl.BlockSpec((tm, 512), lambda i, g, k: (i, g))),
                       (mod, pl.BlockSpec((2, 6, 512), lambda i, g, k: (0, 0, g))),
                       (bias, pl.BlockSpec((1, 512), lambda i, g, k: (0, g))),
                       (scale, pl.BlockSpec((1, 512), lambda i, g, k: (0, g)))], vmem=48)


def mm_t_rows(a, w, l, name, epi=_epi_store, out_dtype=F32, extras_of=None, per=1):
    r, n = a.shape
    kc = w.shape[2]
    tm, tn = _tm(r), kc // per
    extras = []
    if extras_of is not None:
        extras = [(extras_of, pl.BlockSpec((tm, tn), lambda i, j: (i, j)))]
    return _mm(name, "nt", a, w, (r // tm, NSH * per),
               pl.BlockSpec((tm, n), lambda i, j: (i, 0)),
               pl.BlockSpec((None, None, tn, n), lambda i, j: (j // per, l, j % per, 0)),
               [jax.ShapeDtypeStruct((r, NSH * kc), out_dtype)], [pl.BlockSpec((tm, tn), lambda i, j: (i, j))],
               epi, extras=extras, vmem=56)[0]


def mm_t_cols(a, w, l, name):
    r = a.shape[0]
    k, c = w.shape[2], w.shape[3]
    tm, tn = _tm(r), 1024
    return _mm(name, "nt", a, w, (r // tm, k // tn, NSH),
               pl.BlockSpec((tm, c), lambda i, j, s: (i, s)),
               pl.BlockSpec((None, None, tn, c), lambda i, j, s: (s, l, j, 0)),
               [jax.ShapeDtypeStruct((r, k), F32)], [pl.BlockSpec((tm, tn), lambda i, j, s: (i, j))],
               _epi_store, kaxis=2, acc=(tm, tn), vmem=56)[0]


def mm_t_grp(dy, w, o, name):
    r = dy.shape[0]
    tm = _tm(r)
    return _mm(name, "nt", dy, w, (r // tm, 4, NSH),
               pl.BlockSpec((tm, 512), lambda i, g, s: (i, g)),
               pl.BlockSpec((None, None, None, 128, 512), lambda i, g, s: (s, o, g, 0, 0)),
               [jax.ShapeDtypeStruct((r, D), F32)], [pl.BlockSpec((tm, 128), lambda i, g, s: (i, 4 * g + s))],
               _epi_store, vmem=40)[0]


def grad_cols(a, b, name, ta=1024, per=2):
    r, k = a.shape
    c = b.shape[1] // NSH
    tk, tn = _tm(r), c // per
    return _mm(name, "tn", a, b, (NSH, k // ta, per, r // tk),
               pl.BlockSpec((tk, ta), lambda s, i, j, t: (t, i)),
               pl.BlockSpec((tk, tn), lambda s, i, j, t: (t, s * per + j)),
               [jax.ShapeDtypeStruct((NSH, 1, k, c), BF16)],
               [pl.BlockSpec((None, None, ta, tn), lambda s, i, j, t: (s, 0, i, j))],
               _epi_store, kaxis=3, acc=(ta, tn), vmem=56)[0]


def grad_rows(a, b, name, per=1, tn=1024):
    r = a.shape[0]
    kc, n = a.shape[1] // NSH, b.shape[1]
    tk, ta = _tm(r), kc // per
    return _mm(name, "tn", a, b, (NSH, per, n // tn, r // tk),
               pl.BlockSpec((tk, ta), lambda s, i, j, t: (t, s * per + i)),
               pl.BlockSpec((tk, tn), lambda s, i, j, t: (t, j)),
               [jax.ShapeDtypeStruct((NSH, 1, kc, n), BF16)],
               [pl.BlockSpec((None, None, ta, tn), lambda s, i, j, t: (s, 0, i, j))],
               _epi_store, kaxis=3, acc=(ta, tn), vmem=56)[0]


def grad_grp(z, dy, name):
    r = z.shape[0]
    tk = _tm(r)
    return _mm(name, "tn", z, dy, (NSH, 4, r // tk),
               pl.BlockSpec((tk, 128), lambda s, g, t: (t, 4 * g + s)),
               pl.BlockSpec((tk, 512), lambda s, g, t: (t, g)),
               [jax.ShapeDtypeStruct((NSH, 1, 4, 128, 512), BF16)],
               [pl.BlockSpec((None, None, None, 128, 512), lambda s, g, t: (s, 0, g, 0, 0))],
               _epi_store, kaxis=2, acc=(128, 512), vmem=40)[0]


def _scan_tile(reverse, nt):
    if reverse:
        return lambda p: jnp.where(p == 0, 0, nt - p)
    return lambda p: p


def _gates(f, lbv):
    sg = jax.nn.sigmoid(f)
    fg = lbv + (1.0 - lbv) * sg
    g = jnp.log(jnp.maximum(fg, LOG_FLOOR))
    kk = (1.0 - lbv) * jax.nn.sigmoid(-f)
    return sg, fg, g, kk


def _chunk_cumsum(g, reverse):
    n = g.shape[0]
    rr = lax.broadcasted_iota(jnp.int32, (n, n), 0)
    cc = lax.broadcasted_iota(jnp.int32, (n, n), 1)
    inside = (rr // SUB) == (cc // SUB)
    tri = jnp.where(inside & ((cc >= rr) if reverse else (cc <= rr)), 1.0, 0.0).astype(F32)
    return jnp.dot(tri, g, precision=HIGHEST, preferred_element_type=F32)


def _decay(b, s, rows, reverse):
    return jnp.where((rows <= s) if reverse else (rows >= s), jnp.exp(b - b[s:s + 1]), 0.0)


def hgrn_fwd(p, lb, reverse, name):
    r = p.shape[0]
    nt = r // TR
    nsub = TR // SUB
    fcol = 2 if reverse else 1
    tile = _scan_tile(reverse, nt)

    def body(q_ref, f_ref, v_ref, lb_ref, o_ref, sin_ref, st, k_s, b_s):
        i = pl.program_id(0)

        @pl.when(i == 0)
        def _():
            st[...] = jnp.zeros_like(st)

        sin_ref[...] = st[...]
        _, _, g, kk = _gates(f_ref[...], lb_ref[...])
        k_s[...] = kk
        b_s[...] = _chunk_cumsum(g, reverse)
        rows = lax.broadcasted_iota(jnp.int32, (SUB, DK), 0)

        def sub(jj, carry):
            j = (nsub - 1 - jj) if reverse else jj
            rs = pl.ds(pl.multiple_of(j * SUB, SUB), SUB)
            for h in range(NH):
                sl = slice(h * DK, (h + 1) * DK)
                q, k, b, v = q_ref[rs, sl], k_s[rs, sl], b_s[rs, sl], v_ref[rs, sl]
                btot = b[0:1] if reverse else b[SUB - 1:SUB]
                o = _nt(_bf(q * jnp.exp(b)), _bf(st[h]))
                for s in range(SUB):
                    col = jnp.sum(q * k[s:s + 1] * _decay(b, s, rows, reverse), axis=-1, keepdims=True)
                    o = o + col * v[s:s + 1]
                o_ref[rs, sl] = o
                st[h] = st[h] * jnp.exp(btot) + _tn(_bf(v), _bf(k * jnp.exp(btot - b)))
            return carry

        lax.fori_loop(0, nsub, sub, 0)

    seg = lambda col: pl.BlockSpec((TR, AW), lambda i, col=col: (tile(i), col))
    return pl.pallas_call(
        body, name=name, grid=(nt,),
        in_specs=[seg(0), seg(fcol), seg(3), _vec_spec(AW)],
        out_specs=[pl.BlockSpec((TR, AW), lambda i: (tile(i), 0)),
                   pl.BlockSpec((None, NH, DK, DK), lambda i: (tile(i), 0, 0, 0))],
        out_shape=[jax.ShapeDtypeStruct((r, AW), F32), jax.ShapeDtypeStruct((nt, NH, DK, DK), F32)],
        scratch_shapes=[pltpu.VMEM((NH, DK, DK), F32), pltpu.VMEM((TR, AW), F32), pltpu.VMEM((TR, AW), F32)],
        compiler_params=_cp(40),
    )(p, p, p, lb)


def hgrn_bwd(p, do, sin, lb, reverse, name, add=None):
    r = p.shape[0]
    nt = r // TR
    nsub = TR // SUB
    fcol = 2 if reverse else 1
    tile0 = _scan_tile(reverse, nt)
    tile = lambda i: tile0(nt - 1 - i)
    nadd = 0 if add is None else 2
    out_dt = F32 if add is None else BF16

    def body(*refs):
        q_ref, f_ref, v_ref, do_ref, sin_ref, lb_ref = refs[:6]
        adds = refs[6:6 + nadd]
        dq_ref, dv_ref, df_ref, dlb_ref = refs[6 + nadd:10 + nadd]
        dst, srun, ssub, k_s, b_s, sg_s, fg_s = refs[10 + nadd:]
        i = pl.program_id(0)

        @pl.when(i == 0)
        def _():
            dst[...] = jnp.zeros_like(dst)
            dlb_ref[...] = jnp.zeros_like(dlb_ref)

        lbv = lb_ref[...]
        sg, fg, g, kk = _gates(f_ref[...], lbv)
        k_s[...] = kk
        sg_s[...] = sg
        fg_s[...] = fg
        b_s[...] = _chunk_cumsum(g, reverse)
        srun[...] = sin_ref[...]
        rows = lax.broadcasted_iota(jnp.int32, (SUB, DK), 0)
        r16 = lax.broadcasted_iota(jnp.int32, (SUB, SUB), 0)
        c16 = lax.broadcasted_iota(jnp.int32, (SUB, SUB), 1)
        later = jnp.where((c16 <= r16) if reverse else (c16 >= r16), 1.0, 0.0).astype(F32)

        def recompute(jj, c):
            j = (nsub - 1 - jj) if reverse else jj
            rs = pl.ds(pl.multiple_of(j * SUB, SUB), SUB)
            for h in range(NH):
                sl = slice(h * DK, (h + 1) * DK)
                k, b, v = k_s[rs, sl], b_s[rs, sl], v_ref[rs, sl]
                btot = b[0:1] if reverse else b[SUB - 1:SUB]
                ssub[jj, h] = srun[h]
                srun[h] = srun[h] * jnp.exp(btot) + _tn(_bf(v), _bf(k * jnp.exp(btot - b)))
            return c

        lax.fori_loop(0, nsub, recompute, 0)
        for h in range(NH):
            ssub[nsub, h] = srun[h]

        def back(jj, c):
            pos = nsub - 1 - jj
            j = jj if reverse else pos
            rs = pl.ds(pl.multiple_of(j * SUB, SUB), SUB)
            for h in range(NH):
                sl = slice(h * DK, (h + 1) * DK)
                q, k, b, v, dov = q_ref[rs, sl], k_s[rs, sl], b_s[rs, sl], v_ref[rs, sl], do_ref[rs, sl]
                btot = b[0:1] if reverse else b[SUB - 1:SUB]
                s0 = ssub[pos, h]
                ds = dst[h]
                dg_next = jnp.sum(ds * ssub[pos + 1, h], axis=0, keepdims=True)
                eb = jnp.exp(b)
                ebt = jnp.exp(btot - b)
                ke = k * ebt
                dq = _nn(_bf(dov), _bf(s0)) * eb
                dk = _nn(_bf(v), _bf(ds)) * ebt
                dv = _nt(_bf(ke), _bf(ds))
                for s in range(SUB):
                    dec = _decay(b, s, rows, reverse)
                    dsc = jnp.sum(dov * v[s:s + 1], axis=-1, keepdims=True)
                    qd = q * dec
                    dq = dq + (dsc * dec) * k[s:s + 1]
                    dk_row = jnp.sum(dsc * qd, axis=0, keepdims=True)
                    sc = jnp.sum(qd * k[s:s + 1], axis=-1, keepdims=True)
                    dv_row = jnp.sum(sc * dov, axis=0, keepdims=True)
                    dk = dk + jnp.where(rows == s, dk_row, 0.0)
                    dv = dv + jnp.where(rows == s, dv_row, 0.0)
                dst[h] = ds * jnp.exp(btot) + _tn(_bf(dov), _bf(q * eb))
                dg = jnp.dot(later, q * dq - k * dk, precision=HIGHEST, preferred_element_type=F32) + dg_next
                sgv, fgv, lbh = sg_s[rs, sl], fg_s[rs, sl], lbv[:, sl]
                dfg = jnp.where(fgv > LOG_FLOOR, dg / fgv, 0.0)
                df_ref[rs, sl] = ((1.0 - lbh) * sgv * (1.0 - sgv) * (dfg - dk)).astype(BF16)
                dlb_ref[:, sl] += jnp.sum((dfg - dk) * (1.0 - sgv), axis=0, keepdims=True)
                if add is None:
                    dq_ref[rs, sl] = dq
                    dv_ref[rs, sl] = dv
                else:
                    dq_ref[rs, sl] = (dq + adds[0][rs, sl]).astype(BF16)
                    dv_ref[rs, sl] = (dv + adds[1][rs, sl]).astype(BF16)
            return c

        lax.fori_loop(0, nsub, back, 0)

    seg = lambda col: pl.BlockSpec((TR, AW), lambda i, col=col: (tile(i), col))
    plain = pl.BlockSpec((TR, AW), lambda i: (tile(i), 0))
    operands = [p, p, p, do, sin, lb]
    in_specs = [seg(0), seg(fcol), seg(3), plain,
                pl.BlockSpec((None, NH, DK, DK), lambda i: (tile(i), 0, 0, 0)), _vec_spec(AW)]
    if add is not None:
        operands += list(add)
        in_specs += [plain, plain]
    return pl.pallas_call(
        body, name=name, grid=(nt,), in_specs=in_specs,
        out_specs=[plain, plain, plain, _vec_spec(AW)],
        out_shape=[jax.ShapeDtypeStruct((r, AW), out_dt), jax.ShapeDtypeStruct((r, AW), out_dt),
                   jax.ShapeDtypeStruct((r, AW), BF16), jax.ShapeDtypeStruct((1, AW), F32)],
        scratch_shapes=[pltpu.VMEM((NH, DK, DK), F32), pltpu.VMEM((NH, DK, DK), F32),
                        pltpu.VMEM((nsub + 1, NH, DK, DK), F32)]
        + [pltpu.VMEM((TR, AW), F32)] * 4,
        compiler_params=_cp(56),
    )(*operands)


def _silu(v):
    return v * jax.nn.sigmoid(v)


def readout_fwd(of, ob, p, ng, name):
    r = of.shape[0]

    def body(of_ref, ob_ref, g_ref, ng_ref, y_ref):
        for h in range(NH):
            sl = slice(h * DK, (h + 1) * DK)
            o = of_ref[:, sl] + ob_ref[:, sl]
            on = o * lax.rsqrt(jnp.mean(o * o, axis=-1, keepdims=True) + EPS) * ng_ref[:, sl]
            y_ref[:, sl] = (on * _silu(g_ref[:, sl])).astype(BF16)

    return pl.pallas_call(
        body, name=name, grid=(r // TR,),
        in_specs=[_row_spec(AW), _row_spec(AW), _row_spec(AW, 4), _vec_spec(AW)], out_specs=_row_spec(AW),
        out_shape=jax.ShapeDtypeStruct((r, AW), BF16), compiler_params=_cp(32),
    )(of, ob, p, ng)


def readout_bwd(dy, of, ob, p, ng, name):
    r = of.shape[0]

    def body(dy_ref, of_ref, ob_ref, g_ref, ng_ref, do_ref, dg_ref, dn_ref):
        i = pl.program_id(0)

        @pl.when(i == 0)
        def _():
            dn_ref[...] = jnp.zeros_like(dn_ref)

        for h in range(NH):
            sl = slice(h * DK, (h + 1) * DK)
            o = of_ref[:, sl] + ob_ref[:, sl]
            rstd = lax.rsqrt(jnp.mean(o * o, axis=-1, keepdims=True) + EPS)
            oh = o * rstd
            gv = g_ref[:, sl]
            sig = jax.nn.sigmoid(gv)
            dyv = dy_ref[:, sl]
            don = dyv * (gv * sig)
            dg_ref[:, sl] = (dyv * (oh * ng_ref[:, sl]) * (sig * (1.0 + gv * (1.0 - sig)))).astype(BF16)
            dn_ref[:, sl] += jnp.sum(don * oh, axis=0, keepdims=True)
            doh = don * ng_ref[:, sl]
            do_ref[:, sl] = rstd * (doh - oh * jnp.mean(doh * oh, axis=-1, keepdims=True))

    return pl.pallas_call(
        body, name=name, grid=(r // TR,),
        in_specs=[_row_spec(AW), _row_spec(AW), _row_spec(AW), _row_spec(AW, 4), _vec_spec(AW)],
        out_specs=[_row_spec(AW), _row_spec(AW), _vec_spec(AW)],
        out_shape=[jax.ShapeDtypeStruct((r, AW), F32), jax.ShapeDtypeStruct((r, AW), BF16),
                   jax.ShapeDtypeStruct((1, AW), F32)],
        compiler_params=_cp(32),
    )(dy, of, ob, p, ng)


def _gelu(v):
    return 0.5 * v * (1.0 + lax.erf(v * 0.7071067811865476))


def _gelu_grad(v):
    return 0.5 * (1.0 + lax.erf(v * 0.7071067811865476)) + v * (0.3989422804014327 * jnp.exp(-0.5 * v * v))


def _cmlp_norm(vv, gn):
    vg = _gelu(vv)
    mu = jnp.mean(vg, axis=-1, keepdims=True)
    cen = vg - mu
    rstd = lax.rsqrt(jnp.mean(cen * cen, axis=-1, keepdims=True) + EPS)
    xhat = cen * rstd
    return xhat, rstd, xhat * gn


def chunkmlp_fwd(p, ws, bias, gn, name):
    r = p.shape[0]

    def body(u_ref, v_ref, ws_ref, b_ref, gn_ref, y_ref):
        for ci in range(TR // B_CHUNK):
            rs = slice(ci * B_CHUNK, (ci + 1) * B_CHUNK)
            for gidx in range(NH):
                sl = slice(gidx * DK, (gidx + 1) * DK)
                _, _, vn = _cmlp_norm(v_ref[rs, sl], gn_ref[:, sl])
                mixed = _nn(_bf(ws_ref[gidx]), _bf(vn)) + b_ref[gidx]
                y_ref[rs, sl] = (_gelu(u_ref[rs, sl]) * mixed).astype(BF16)

    return pl.pallas_call(
        body, name=name, grid=(r // TR,),
        in_specs=[_row_spec(AW, 5), _row_spec(AW, 6), pl.BlockSpec((NH, B_CHUNK, B_CHUNK), lambda i: (0, 0, 0)),
                  pl.BlockSpec((NH, B_CHUNK, 1), lambda i: (0, 0, 0)), _vec_spec(AW)],
        out_specs=_row_spec(AW), out_shape=jax.ShapeDtypeStruct((r, AW), BF16), compiler_params=_cp(32),
    )(p, p, ws, bias, gn)


def chunkmlp_bwd(dy, p, ws, bias, gn, name):
    r = p.shape[0]

    def body(dy_ref, u_ref, v_ref, ws_ref, b_ref, gn_ref, du_ref, dv_ref, dws_ref, db_ref, dgn_ref):
        i = pl.program_id(0)

        @pl.when(i == 0)
        def _():
            dws_ref[...] = jnp.zeros_like(dws_ref)
            db_ref[...] = jnp.zeros_like(db_ref)
            dgn_ref[...] = jnp.zeros_like(dgn_ref)

        for ci in range(TR // B_CHUNK):
            rs = slice(ci * B_CHUNK, (ci + 1) * B_CHUNK)
            for gidx in range(NH):
                sl = slice(gidx * DK, (gidx + 1) * DK)
                vv, uv, dyv, gnv = v_ref[rs, sl], u_ref[rs, sl], dy_ref[rs, sl], gn_ref[:, sl]
                xhat, rstd, vn = _cmlp_norm(vv, gnv)
                wg = _bf(ws_ref[gidx])
                mixed = _nn(wg, _bf(vn)) + b_ref[gidx]
                dmixed = dyv * _gelu(uv)
                du_ref[rs, sl] = (dyv * mixed * _gelu_grad(uv)).astype(BF16)
                dws_ref[gidx] += _nt(_bf(dmixed), _bf(vn))
                db_ref[gidx] += jnp.sum(dmixed, axis=-1, keepdims=True)
                dvn = _tn(wg, _bf(dmixed))
                dgn_ref[:, sl] += jnp.sum(dvn * xhat, axis=0, keepdims=True)
                dxh = dvn * gnv
                dvg = rstd * (dxh - jnp.mean(dxh, axis=-1, keepdims=True)
                              - xhat * jnp.mean(dxh * xhat, axis=-1, keepdims=True))
                dv_ref[rs, sl] = (dvg * _gelu_grad(vv)).astype(BF16)

    return pl.pallas_call(
        body, name=name, grid=(r // TR,),
        in_specs=[_row_spec(AW, 1), _row_spec(AW, 5), _row_spec(AW, 6),
                  pl.BlockSpec((NH, B_CHUNK, B_CHUNK), lambda i: (0, 0, 0)),
                  pl.BlockSpec((NH, B_CHUNK, 1), lambda i: (0, 0, 0)), _vec_spec(AW)],
        out_specs=[_row_spec(AW), _row_spec(AW), pl.BlockSpec((NH, B_CHUNK, B_CHUNK), lambda i: (0, 0, 0)),
                   pl.BlockSpec((NH, B_CHUNK, 1), lambda i: (0, 0, 0)), _vec_spec(AW)],
        out_shape=[jax.ShapeDtypeStruct((r, AW), BF16), jax.ShapeDtypeStruct((r, AW), BF16),
                   jax.ShapeDtypeStruct((NH, B_CHUNK, B_CHUNK), F32), jax.ShapeDtypeStruct((NH, B_CHUNK, 1), F32),
                   jax.ShapeDtypeStruct((1, AW), F32)],
        compiler_params=_cp(32),
    )(dy, p, p, ws, bias, gn)


def _win_count(pos, k, n):
    lo = jnp.maximum(pos - k // 2, 0)
    hi = jnp.minimum(pos - k // 2 + k, n)
    return (hi - lo).astype(F32)


POOL_CW = 256
POOL_PAD = (POOL_WINDOWS[-1] // 2) * GRID_W


def pool_op(p, transpose, name):
    r = p.shape[0]
    seq = r - CTX
    grows = seq // GRID_W
    nt = seq // TR

    def body(x_ref, o_ref, y_s):
        j = pl.program_id(0)

        @pl.when(j == 0)
        def _():
            y_s[:POOL_PAD, :] = jnp.zeros((POOL_PAD, POOL_CW), F32)
            y_s[POOL_PAD + seq:, :] = jnp.zeros((POOL_PAD, POOL_CW), F32)

        for gi, k in enumerate(POOL_WINDOWS):
            @pl.when(j // (512 // POOL_CW) == gi)
            def _(k=k):
                offs = list(range(-(k // 2) + 1, k // 2 + 1) if transpose else range(-(k // 2), k // 2))
                tt = lax.broadcasted_iota(jnp.int32, (TR, TR), 0)
                ss = lax.broadcasted_iota(jnp.int32, (TR, TR), 1)
                band = (ss - tt >= offs[0]) & (ss - tt <= offs[-1])
                b_ctx = jnp.where(band, 1.0, 0.0).astype(F32)
                b_grid = jnp.where(band & ((tt >> 6) == (ss >> 6)), 1.0, 0.0).astype(F32)
                trow = lax.broadcasted_iota(jnp.int32, (TR, POOL_CW), 0)

                def count(i):
                    t = i * TR + trow
                    return _win_count(t & (GRID_W - 1), k, GRID_W) * _win_count(t >> 6, k, grows)

                def col_pass(i, carry):
                    xt = x_ref[pl.ds(pl.multiple_of(CTX + i * TR, TR), TR), :]
                    if transpose:
                        xt = xt / count(i)
                    y_s[pl.ds(pl.multiple_of(POOL_PAD + i * TR, TR), TR), :] = jnp.dot(
                        b_grid, xt, precision=HIGHEST, preferred_element_type=F32)
                    return carry

                lax.fori_loop(0, nt, col_pass, 0)

                def row_pass(i, carry):
                    base = POOL_PAD + i * TR
                    acc = y_s[pl.ds(pl.multiple_of(base + offs[0] * GRID_W, GRID_W), TR), :]
                    for d in offs[1:]:
                        acc = acc + y_s[pl.ds(pl.multiple_of(base + d * GRID_W, GRID_W), TR), :]
                    rows = pl.ds(pl.multiple_of(CTX + i * TR, TR), TR)
                    if not transpose:
                        acc = acc / count(i)
                    o_ref[rows, :] = (acc - x_ref[rows, :]).astype(BF16)
                    return carry

                lax.fori_loop(0, nt, row_pass, 0)

                cx = x_ref[:CTX, :]
                cntc = _win_count(trow, k, CTX)
                accc = jnp.dot(b_ctx, cx / cntc if transpose else cx, precision=HIGHEST, preferred_element_type=F32)
                o_ref[:CTX, :] = ((accc if transpose else accc / cntc) - cx).astype(BF16)

    spec = pl.BlockSpec((r, POOL_CW), lambda j: (0, j))
    return pl.pallas_call(
        body, name=name, grid=(D // POOL_CW,), in_specs=[spec], out_specs=spec,
        out_shape=jax.ShapeDtypeStruct((r, D), BF16),
        scratch_shapes=[pltpu.VMEM((seq + 2 * POOL_PAD, POOL_CW), F32)], compiler_params=_cp(56),
    )(p)


def ada_mods(cs, w_ada, b_loc, name):
    nl, _, cl = w_ada.shape
    tn = 1024

    def body(c_ref, w_ref, b_ref, o_ref):
        o_ref[...] = _nn(_bf(_silu(c_ref[...])), _bf(w_ref[...])) + b_ref[...]

    return pl.pallas_call(
        body, name=name, grid=(nl, cl // tn),
        in_specs=[pl.BlockSpec((16, D), lambda l, j: (0, 0)), pl.BlockSpec((None, D, tn), lambda l, j: (l, 0, j)),
                  pl.BlockSpec((None, 1, tn), lambda l, j: (l, 0, j))],
        out_specs=pl.BlockSpec((None, 16, tn), lambda l, j: (l, 0, j)),
        out_shape=jax.ShapeDtypeStruct((nl, 16, cl), F32), compiler_params=_cp(40),
    )(cs, w_ada, b_loc)


def _adamw(w, g, m, v):
    m = ADAM_B1 * m + (1.0 - ADAM_B1) * g
    v = ADAM_B2 * v + (1.0 - ADAM_B2) * (g * g)
    m_hat = m / (1.0 - ADAM_B1 ** ADAM_STEP)
    v_hat = v / (1.0 - ADAM_B2 ** ADAM_STEP)
    delta = -ADAM_LR * (m_hat / (jnp.sqrt(v_hat) + ADAM_EPS) + ADAM_WD * w)
    return delta, m, v


def ada_update(cs, dm, w, m, v, name):
    nl, _, cl = w.shape
    ta, tn = 256, 1024

    def body(c_ref, dm_ref, w_ref, m_ref, v_ref, g_ref, d_ref, nm_ref, nv_ref, dc_ref):
        l, j = pl.program_id(1), pl.program_id(2)
        a = _bf(_silu(c_ref[...]))
        bmat = _bf(dm_ref[...])
        wv = w_ref[...]
        g = _tn(a, bmat)
        g_ref[...] = g
        d_ref[...], nm_ref[...], nv_ref[...] = _adamw(wv, g, m_ref[...], v_ref[...])
        _accum(dc_ref, _nt(bmat, _bf(wv)), (l == 0) & (j == 0))

    wspec = pl.BlockSpec((None, ta, tn), lambda i, l, j: (l, i, j))
    shp = jax.ShapeDtypeStruct(w.shape, F32)
    return pl.pallas_call(
        body, name=name, grid=(D // ta, nl, cl // tn),
        in_specs=[pl.BlockSpec((16, ta), lambda i, l, j: (0, i)),
                  pl.BlockSpec((None, 16, tn), lambda i, l, j: (l, 0, j)), wspec, wspec, wspec],
        out_specs=[wspec, wspec, wspec, wspec, pl.BlockSpec((16, ta), lambda i, l, j: (0, i))],
        out_shape=[shp, shp, shp, shp, jax.ShapeDtypeStruct((16, D), F32)], compiler_params=_cp(40),
    )(cs, dm, w, m, v)


def adamw_big(p, q, w, m, v, l, dsts, name):
    nl = w.shape[0]
    rows, cols = p.shape
    tr = min(rows, 256)
    w3, m3, v3 = (t.reshape(nl, rows, cols) for t in (w, m, v))
    nd = 0 if dsts is None else 4

    def body(p_ref, q_ref, w_ref, m_ref, v_ref, *rest):
        g_ref, d_ref, nm_ref, nv_ref = rest[nd:]
        g = p_ref[...].astype(F32) + q_ref[...].astype(F32)
        g_ref[...] = g
        d_ref[...], nm_ref[...], nv_ref[...] = _adamw(w_ref[...], g, m_ref[...], v_ref[...])

    part = pl.BlockSpec((tr, cols), lambda i: (i, 0))
    spec = pl.BlockSpec((None, tr, cols), lambda i: (l, i, 0))
    shp = jax.ShapeDtypeStruct((nl, rows, cols), F32)
    return pl.pallas_call(
        body, name=name, grid=(rows // tr,),
        in_specs=[part, part, spec, spec, spec] + [ANY] * nd,
        out_specs=[spec] * 4, out_shape=[shp] * 4,
        input_output_aliases={5 + t: t for t in range(nd)}, compiler_params=_cp(48),
    )(p, q, w3, m3, v3, *([] if dsts is None else dsts))


def sum8(g8, name):
    n = g8.shape[1]

    def body(g_ref, o_ref):
        acc = g_ref[0]
        for dev in range(1, 8):
            acc = acc + g_ref[dev]
        o_ref[...] = acc

    return pl.pallas_call(
        body, name=name, grid=(1,), in_specs=[pl.BlockSpec((8, n, 128), lambda i: (0, 0, 0))],
        out_specs=pl.BlockSpec((n, 128), lambda i: (0, 0)),
        out_shape=jax.ShapeDtypeStruct((n, 128), F32), compiler_params=_cp(48),
    )(g8)


def adamw_small(g, w, m, v, name):
    def body(g_ref, w_ref, m_ref, v_ref, d_ref, nm_ref, nv_ref):
        d_ref[...], nm_ref[...], nv_ref[...] = _adamw(w_ref[...], g_ref[...], m_ref[...], v_ref[...])

    spec = pl.BlockSpec(g.shape, lambda i: (0, 0))
    shp = jax.ShapeDtypeStruct(g.shape, F32)
    return pl.pallas_call(
        body, name=name, grid=(1,), in_specs=[spec] * 4, out_specs=[spec] * 3, out_shape=[shp] * 3,
        compiler_params=_cp(48),
    )(g, w, m, v)


def _pack(arrs):
    flat = jnp.concatenate([a.reshape(-1).astype(F32) for a in arrs])
    pad = (-flat.shape[0]) % 1024
    return jnp.pad(flat, (0, pad)).reshape(-1, 128)


def _unpack(packed, shapes):
    flat = packed.reshape(-1)
    out, off = [], 0
    for s in shapes:
        n = 1
        for d in s:
            n *= d
        out.append(flat[off:off + n].reshape(s))
        off += n
    return out


def _lower_bounds(lb_logits):
    pr = jax.nn.softmax(lb_logits.astype(F32), axis=1)
    return jnp.cumsum(pr, axis=1) - pr[:, :1]


class LayerWeights:
    def __init__(self, fetch):
        self.fetch, self.have, self.tokens = fetch, {}, []

    def get(self, key, after):
        if key not in self.have:
            new, token = self.fetch(key, after)
            self.have.update(new)
            if token is not None:
                self.tokens.append(token)
        return self.have[key]

    def tie(self, mod):
        for token in self.tokens:
            mod = mod + token[0, 0]
        self.tokens = []
        return mod


def device_step(xs, target, mods, sp, weights_for, grads_done):
    lbs = _lower_bounds(sp["lb_logits"])
    saved = []
    for layer in range(4):
        wl, md = weights_for(layer), mods[layer]
        w_in = wl.get("in" if layer % 2 == 0 else "pin", xs)
        md = wl.tie(md)
        s = {"xs": xs}
        h1 = norm_mod(xs, sp["g_norm_mix"][layer][None], md, 0, f"norm_mix{layer}")
        s["h1"] = h1
        if layer % 2 == 0:
            e = layer // 2
            p = mm_cols(h1, w_in, 0, f"in_proj{layer}")
            of, sf = hgrn_fwd(p, lbs[0, e][None], False, f"scan_f{layer}")
            ob, sb = hgrn_fwd(p, lbs[1, e][None], True, f"scan_b{layer}")
            ya = readout_fwd(of, ob, p, sp["g_hgrn_out"][e][None], f"readout{layer}")
            yb = chunkmlp_fwd(p, sp["w_spatial"][e], sp["b_spatial"][e][:, :, None], sp["g_spatial_v"][e][None],
                              f"cmlp{layer}")
            ycat = jnp.concatenate([ya, yb], axis=1)
            w_out = wl.get("out", ycat)
            md = wl.tie(md)
            x1, f1 = mm_rows_res(ycat, w_out, 0, xs, md, 2, f"out_proj{layer}")
            s.update(p=p, of=of, ob=ob, sf=sf, sb=sb, ycat=ycat, f1=f1)
        else:
            o = layer // 2
            pp = mm_rows(h1, w_in, 0, f"pool_in{layer}")
            z = pool_op(pp, False, f"pool{layer}")
            x1, ypre = mm_grp_res(z, wl.get("grp", z), 0, sp["b_grp_pool"][o].reshape(1, D),
                                  sp["scale_pool"][o][None], xs, md, 2, f"pool_grp{layer}")
            s.update(z=z, ypre=ypre)
        h2 = norm_mod(x1, sp["g_norm_ffn"][layer][None], md, 3, f"norm_ffn{layer}")
        u = mm_cols(h2, wl.get("up", h2), 0, f"ffn_up{layer}", epi=_epi_relu2, out_dtype=BF16)
        x2, f2 = mm_rows_res(u, wl.get("down", u), 0, x1, md, 5, f"ffn_down{layer}")
        s.update(x1=x1, h2=h2, u=u, f2=f2, w=wl.have, md=md)
        saved.append(s)
        xs = x2

    dx, loss_lanes, dg_final = final_loss(xs, sp["g_norm_final"][None], target, "final_loss")

    token = jnp.zeros((8, 128), F32)
    dmods = [None] * 4
    sg = {"g_norm_final": dg_final[0], "g_norm_mix": [None] * 4, "g_norm_ffn": [None] * 4,
          "dlbs": [[None, None], [None, None]], "g_hgrn_out": [None] * 2, "w_spatial": [None] * 2,
          "b_spatial": [None] * 2, "g_spatial_v": [None] * 2, "b_grp_pool": [None] * 2, "scale_pool": [None] * 2}
    for layer in reversed(range(4)):
        s = saved[layer]
        wl, md = s["w"], s["md"] + token[0, 0]
        df2, dgt2 = gate_in(dx, s["f2"], md, 5, f"gate_ffn{layer}")
        da = mm_t_rows(df2, wl["down"], 0, f"ffn_down_t{layer}", epi=_epi_2sqrt, out_dtype=BF16,
                       extras_of=s["u"], per=2)
        g_down = grad_rows(s["u"], df2, f"g_ffn_down{layer}", per=2, tn=2048)
        dh2 = mm_t_cols(da, wl["up"], 0, f"ffn_up_t{layer}")
        g_up = grad_cols(s["h2"], da, f"g_ffn_up{layer}", per=1)
        dx1, dsh2, dsc2, dgf = normmod_bwd(dh2, s["x1"], dx, sp["g_norm_ffn"][layer][None], md, 3,
                                           f"norm_ffn_b{layer}")
        sg["g_norm_ffn"][layer] = dgf[0]
        md = md + grads_done(layer, "ffn", [g_up, g_down], dx1)[0, 0]
        if layer % 2 == 0:
            e = layer // 2
            df1, dgt1 = gate_in(dx1, s["f1"], md, 2, f"gate_mix{layer}")
            dycat = mm_t_rows(df1, wl["out"], 0, f"out_proj_t{layer}")
            g_b = grad_rows(s["ycat"], df1, f"g_out_proj{layer}")
            do, dpg, dng = readout_bwd(dycat, s["of"], s["ob"], s["p"], sp["g_hgrn_out"][e][None],
                                       f"readout_b{layer}")
            du, dv, dws, dbs, dgn = chunkmlp_bwd(dycat, s["p"], sp["w_spatial"][e], sp["b_spatial"][e][:, :, None],
                                                 sp["g_spatial_v"][e][None], f"cmlp_b{layer}")
            dq_f, di_f, dff, dlb_f = hgrn_bwd(s["p"], do, s["sf"], lbs[0, e][None], False, f"scan_f_b{layer}")
            dq, di, dfb, dlb_b = hgrn_bwd(s["p"], do, s["sb"], lbs[1, e][None], True, f"scan_b_b{layer}",
                                          add=(dq_f, di_f))
            dp = jnp.concatenate([dq, dff, dfb, di, dpg, du, dv], axis=1)
            dh1 = mm_t_cols(dp, wl["in"], 0, f"in_proj_t{layer}")
            g_a = grad_cols(s["h1"], dp, f"g_in_proj{layer}")
            sg["dlbs"][0][e], sg["dlbs"][1][e] = dlb_f[0], dlb_b[0]
            sg["g_hgrn_out"][e], sg["w_spatial"][e] = dng[0], dws
            sg["b_spatial"][e], sg["g_spatial_v"][e] = dbs[:, :, 0], dgn[0]
        else:
            o = layer // 2
            dyp, dgt1, dscale, dbias = gate_in_pool(dx1, s["ypre"], md, sp["scale_pool"][o][None], 2,
                                                    f"gate_mix{layer}")
            dz = mm_t_grp(dyp, wl["grp"], 0, f"pool_grp_t{layer}")
            g_b = grad_grp(s["z"], dyp, f"g_pool_grp{layer}")
            dpp = pool_op(dz, True, f"pool_t{layer}")
            dh1 = mm_t_rows(dpp, wl["pin"], 0, f"pool_in_t{layer}")
            g_a = grad_rows(s["h1"], dpp, f"g_pool_in{layer}")
            sg["b_grp_pool"][o], sg["scale_pool"][o] = dbias[0].reshape(4, 512), dscale[0]
        dx, dsh1, dsc1, dgm = normmod_bwd(dh1, s["xs"], dx1, sp["g_norm_mix"][layer][None], md, 0,
                                          f"norm_mix_b{layer}")
        sg["g_norm_mix"][layer] = dgm[0]
        dmods[layer] = jnp.concatenate([dsh1, dsc1, dgt1, dsh2, dsc2, dgt2], axis=1)
        token = grads_done(layer, "mix", [g_a, g_b], dx)
    return loss_lanes, dx, dmods, sg


BIG = ("w_in_even", "w_out_even", "w_in_pool", "w_grp_pool", "w_ffn_up", "w_ffn_down")
SMALL = ("b_ada", "g_norm_mix", "g_norm_ffn", "lb_logits", "g_hgrn_out", "w_spatial", "b_spatial", "g_spatial_v",
         "b_grp_pool", "scale_pool", "g_norm_final")
WEIGHTS = ("c_ctx", "w_ada", "b_ada", "g_norm_mix", "g_norm_ffn", "w_in_even", "w_out_even", "lb_logits",
           "g_hgrn_out", "w_spatial", "b_spatial", "g_spatial_v", "w_in_pool", "w_grp_pool", "b_grp_pool",
           "scale_pool", "w_ffn_up", "w_ffn_down", "g_norm_final")


def kernel(x, c, ctx, c_ctx, w_ada, b_ada, g_norm_mix, g_norm_ffn, w_in_even, w_out_even, lb_logits, g_hgrn_out, w_spatial, b_spatial, g_spatial_v, w_in_pool, w_grp_pool, b_grp_pool, scale_pool, w_ffn_up, w_ffn_down, g_norm_final, loss_target, m_c_ctx, m_w_ada, m_b_ada, m_g_norm_mix, m_g_norm_ffn, m_w_in_even, m_w_out_even, m_lb_logits, m_g_hgrn_out, m_w_spatial, m_b_spatial, m_g_spatial_v, m_w_in_pool, m_w_grp_pool, m_b_grp_pool, m_scale_pool, m_w_ffn_up, m_w_ffn_down, m_g_norm_final, v_c_ctx, v_w_ada, v_b_ada, v_g_norm_mix, v_g_norm_ffn, v_w_in_even, v_w_out_even, v_lb_logits, v_g_hgrn_out, v_w_spatial, v_b_spatial, v_g_spatial_v, v_w_in_pool, v_w_grp_pool, v_b_grp_pool, v_scale_pool, v_w_ffn_up, v_w_ffn_down, v_g_norm_final):
    loc = dict(c_ctx=c_ctx, w_ada=w_ada, b_ada=b_ada, g_norm_mix=g_norm_mix, g_norm_ffn=g_norm_ffn,
               w_in_even=w_in_even, w_out_even=w_out_even, lb_logits=lb_logits, g_hgrn_out=g_hgrn_out,
               w_spatial=w_spatial, b_spatial=b_spatial, g_spatial_v=g_spatial_v, w_in_pool=w_in_pool,
               w_grp_pool=w_grp_pool, b_grp_pool=b_grp_pool, scale_pool=scale_pool, w_ffn_up=w_ffn_up,
               w_ffn_down=w_ffn_down, g_norm_final=g_norm_final)
    mom = dict(c_ctx=m_c_ctx, w_ada=m_w_ada, b_ada=m_b_ada, g_norm_mix=m_g_norm_mix, g_norm_ffn=m_g_norm_ffn,
               w_in_even=m_w_in_even, w_out_even=m_w_out_even, lb_logits=m_lb_logits, g_hgrn_out=m_g_hgrn_out,
               w_spatial=m_w_spatial, b_spatial=m_b_spatial, g_spatial_v=m_g_spatial_v, w_in_pool=m_w_in_pool,
               w_grp_pool=m_w_grp_pool, b_grp_pool=m_b_grp_pool, scale_pool=m_scale_pool, w_ffn_up=m_w_ffn_up,
               w_ffn_down=m_w_ffn_down, g_norm_final=m_g_norm_final)
    var = dict(c_ctx=v_c_ctx, w_ada=v_w_ada, b_ada=v_b_ada, g_norm_mix=v_g_norm_mix, g_norm_ffn=v_g_norm_ffn,
               w_in_even=v_w_in_even, w_out_even=v_w_out_even, lb_logits=v_lb_logits, g_hgrn_out=v_g_hgrn_out,
               w_spatial=v_w_spatial, b_spatial=v_b_spatial, g_spatial_v=v_g_spatial_v, w_in_pool=v_w_in_pool,
               w_grp_pool=v_w_grp_pool, b_grp_pool=v_b_grp_pool, scale_pool=v_scale_pool, w_ffn_up=v_w_ffn_up,
               w_ffn_down=v_w_ffn_down, g_norm_final=v_g_norm_final)
    mx, my, mc = _mesh_pos()
    chip = 2 * mx + my
    dev = 2 * chip + mc

    def layer_tensors(layer):
        i = layer // 2
        pair = ("w_in_even", "w_out_even") if layer % 2 == 0 else ("w_in_pool", "w_grp_pool")
        return [(pair[0], i), (pair[1], i), ("w_ffn_up", layer), ("w_ffn_down", layer)]

    def layer_keys(layer):
        return ("in", "out", "up", "down") if layer % 2 == 0 else ("pin", "grp", "up", "down")

    def gather_parts(layer):
        return {"a": slice(0, 1), "b": slice(1, 4)} if layer == 0 else {"a": slice(0, 4)}

    def start_gather(layer, part, after):
        srcs = [loc[n][i][None].astype(BF16) for n, i in layer_tensors(layer)[gather_parts(layer)[part]]]
        lands = [lax.dynamic_update_slice(jnp.zeros((NSH,) + s.shape, BF16), s[None], (chip,) + (0,) * s.ndim)
                 for s in srcs]
        return ici_start(srcs, lands, after, False, f"gather_start{layer}{part}")

    hello = allgather8(_pack([c[0], lb_logits, b_grp_pool, scale_pool]), "gather_small")
    parts = [_unpack(hello[d], [(D,), (2, 2, 256), (2, 4, 128), (2, 512)]) for d in range(8)]
    cs = jnp.concatenate([jnp.stack([parts[d][0] for d in range(8)]), c_ctx[None], jnp.zeros((7, D), F32)])
    chips_of = [parts[2 * s] for s in range(NSH)]
    sp = dict(loc)
    sp["lb_logits"] = jnp.concatenate([q[1] for q in chips_of], axis=2)
    sp["b_grp_pool"] = jnp.concatenate([q[2] for q in chips_of], axis=2)
    sp["scale_pool"] = jnp.concatenate([q[3] for q in chips_of], axis=1)

    b_loc = lax.dynamic_slice_in_dim(b_ada, chip * 3072, 3072, axis=1)[:, None, :]
    mods_loc = ada_mods(cs, w_ada, b_loc, "ada_mods")
    mods_all = allgather8(mods_loc.reshape(-1, 128), "gather_mods").reshape(8, 4, 16, 3072)
    mods_full = jnp.concatenate([mods_all[2 * s] for s in range(NSH)], axis=2)
    mine = lax.dynamic_index_in_dim(mods_full, dev, axis=1, keepdims=False)
    mods = [jnp.stack([mods_full[l, 8].reshape(6, D), mine[l].reshape(6, D)]) for l in range(4)]

    first_a = start_gather(0, "a", mods_all)
    gathers = {(0, "a"): first_a, (0, "b"): start_gather(0, "b", first_a[4])}

    def weights_for(layer):
        def fetch(key, after):
            part = "b" if layer == 0 and key != "in" else "a"
            _, got = ici_wait(gathers[(layer, part)], after, False, f"gather_wait{layer}{part}")
            token = None
            if layer < 3 and part == list(gather_parts(layer))[-1]:
                gathers[(layer + 1, "a")] = start_gather(layer + 1, "a", got[0])
                token = gathers[(layer + 1, "a")][4]
            return dict(zip(layer_keys(layer)[gather_parts(layer)[part]], got)), token
        return LayerWeights(fetch)

    exchanges = {}

    def grads_done(layer, part, gs, after):
        lands = [lax.empty((8,) + g.shape[1:], BF16) for g in gs]
        exchanges[(layer, part)] = ici_start(gs, lands, after, True, f"exchange_start{layer}{part}")
        return exchanges[(layer, part)][4]

    xs = jnp.concatenate([ctx[0], x[0]], axis=0)
    loss_lanes, dxs, dmods, sg = device_step(xs, loss_target[0], mods, sp, weights_for, grads_done)
    grad_x = dxs[CTX:][None]

    out = {}

    def finish(layer, part, after):
        gs, rbs = ici_wait(exchanges[(layer, part)], after, True, f"exchange_wait{layer}{part}")
        names = layer_tensors(layer)[slice(2, 4) if part == "ffn" else slice(0, 2)]
        ps = [sum_blocks(g, rb, f"sum_{n}{i}") for (n, i), g, rb in zip(names, gs, rbs)]
        qs = swap_sibling(ps, f"swap{layer}{part}")
        for (n, i), p, q in zip(names, ps, qs):
            out[n] = adamw_big(p, q, loc[n], mom[n], var[n], i, out.get(n), f"adamw_{n}{i}")

    for layer in (3, 2, 1):
        finish(layer, "ffn", dxs)
        finish(layer, "mix", dxs)
    finish(0, "ffn", dxs)

    dm_lat = jnp.stack([dmods[l][1].reshape(6 * D) for l in range(4)])
    dm_ctx = jnp.stack([dmods[l][0].reshape(6 * D) for l in range(4)])
    small_shapes = [(4, 6 * D), (4, 6 * D), (4, D), (4, D), (2, 2, AW), (2, AW), (2, NH, 128, 128), (2, NH, 128),
                    (2, AW), (2, 4, 512), (2, D), (D,), (128,)]
    mine_small = _pack([dm_lat, dm_ctx, jnp.stack(sg["g_norm_mix"]), jnp.stack(sg["g_norm_ffn"]),
                        jnp.stack([jnp.stack(sg["dlbs"][0]), jnp.stack(sg["dlbs"][1])]),
                        jnp.stack(sg["g_hgrn_out"]), jnp.stack(sg["w_spatial"]), jnp.stack(sg["b_spatial"]),
                        jnp.stack(sg["g_spatial_v"]), jnp.stack(sg["b_grp_pool"]), jnp.stack(sg["scale_pool"]),
                        sg["g_norm_final"], loss_lanes[0]])
    all_small = allgather8(mine_small, "gather_small_grads")
    tot = _unpack(sum8(all_small, "sum_small_grads"), small_shapes)
    (_, dm_ctx_tot, g_mix, g_ffn, dlbs, g_hg, g_ws, g_bs, g_gv, g_bg, g_sc, g_fin, loss_v) = tot
    loss = loss_v[0]
    dm_lat_all = jnp.stack([_unpack(all_small[d], small_shapes[:1])[0] for d in range(8)])
    g_b_ada = jnp.sum(dm_lat_all, axis=0) + dm_ctx_tot
    _, lb_vjp = jax.vjp(_lower_bounds, sp["lb_logits"])
    g_lb_full = lb_vjp(dlbs)[0]
    grads = {"b_ada": g_b_ada, "g_norm_mix": g_mix, "g_norm_ffn": g_ffn,
             "lb_logits": lax.dynamic_slice_in_dim(g_lb_full, chip * 256, 256, axis=2),
             "g_hgrn_out": g_hg, "w_spatial": g_ws, "b_spatial": g_bs, "g_spatial_v": g_gv,
             "b_grp_pool": lax.dynamic_slice_in_dim(g_bg, chip * 128, 128, axis=2),
             "scale_pool": lax.dynamic_slice_in_dim(g_sc, chip * 512, 512, axis=1), "g_norm_final": g_fin}

    dm_rows = jnp.concatenate([dm_lat_all.transpose(1, 0, 2), dm_ctx_tot[:, None, :], jnp.zeros((4, 7, 6 * D), F32)],
                              axis=1)
    dm_loc = lax.dynamic_slice_in_dim(dm_rows, chip * 3072, 3072, axis=2)
    g_wa, d_wa, nm_wa, nv_wa, dc_part = ada_update(cs, dm_loc, w_ada, m_w_ada, v_w_ada, "ada_update")
    out["w_ada"] = [g_wa, d_wa, nm_wa, nv_wa]
    dc_all = allgather8(dc_part[8].reshape(16, 128), "gather_dc")
    dpre = dc_all[0] + dc_all[2] + dc_all[4] + dc_all[6]
    sig = jax.nn.sigmoid(c_ctx)
    grads["c_ctx"] = dpre.reshape(D) * (sig * (1.0 + c_ctx * (1.0 - sig)))

    names = ("c_ctx",) + SMALL
    shapes = [loc[n].shape for n in names]
    d_s, nm_s, nv_s = adamw_small(_pack([grads[n] for n in names]), _pack([loc[n] for n in names]),
                                  _pack([mom[n] for n in names]), _pack([var[n] for n in names]), "adamw_small")
    for n, dl, nm, nv in zip(names, _unpack(d_s, shapes), _unpack(nm_s, shapes), _unpack(nv_s, shapes)):
        out[n] = [grads[n], dl, nm, nv]

    finish(0, "mix", d_s)
    for n in BIG:
        out[n] = [t.reshape(loc[n].shape) for t in out[n]]

    return (loss, grad_x, *[out[n][0] for n in WEIGHTS], *[out[n][1] for n in WEIGHTS],
            *[out[n][2] for n in WEIGHTS], *[out[n][3] for n in WEIGHTS])
```

```python
import functools

import jax
import jax.numpy as jnp
from jax import lax
from jax.experimental import pallas as pl
from jax.experimental.pallas import tpu as pltpu

F32 = jnp.float32
BF16 = jnp.bfloat16
MESH = pl.DeviceIdType.MESH
ANY = pl.BlockSpec(memory_space=pl.ANY)
HIGHEST = lax.Precision.HIGHEST

D = 2048
DFF = 8192
CTX = 256
TR = 256
GRID_W = 64
EPS = 1e-6
LOG_FLOOR = 1e-30
NH = 8
DK = 128
SUB_FWD = 16
SUB_BWD = 32
B_CHUNK = 128
AW = NH * DK
POOL_WINDOWS = (2, 4, 8, 16)
NSH = 4
ADAM_LR, ADAM_B1, ADAM_B2, ADAM_EPS, ADAM_WD, ADAM_STEP = 0.001, 0.9, 0.999, 1e-08, 0.01, 10
MIB = 1024 * 1024


def _cp(vmem_mib):
    return pltpu.CompilerParams(vmem_limit_bytes=vmem_mib * MIB)


def _bf(v):
    return v.astype(BF16)


def _nn(a, b):
    return lax.dot_general(a, b, (((1,), (0,)), ((), ())), preferred_element_type=F32)


def _nt(a, b):
    return lax.dot_general(a, b, (((1,), (1,)), ((), ())), preferred_element_type=F32)


def _tn(a, b):
    return lax.dot_general(a, b, (((0,), (0,)), ((), ())), preferred_element_type=F32)


def _mesh_pos():
    return lax.axis_index("x"), lax.axis_index("y"), lax.axis_index("c")


HBM = pl.BlockSpec(memory_space=pltpu.HBM)
SEM = pl.BlockSpec(memory_space=pltpu.SEMAPHORE)
EFFECT = pltpu.SideEffectType.DATAFLOW_SIDE_EFFECTING


def _peer_copies(exchange, srcs, lands, send_sems, recv_sems):
    x, y, c = _mesh_pos()
    me = 2 * x + y
    pairs = []
    for t in range(len(srcs)):
        for j, (px, py) in enumerate([(1 - x, y), (x, 1 - y), (1 - x, 1 - y)]):
            peer = 2 * px + py
            src = srcs[t].at[peer] if exchange else srcs[t]
            out_slot, in_slot = (2 * me + c, 2 * peer + c) if exchange else (me, peer)

            def mk(slot, t=t, j=j, px=px, py=py, src=src):
                return pltpu.make_async_remote_copy(
                    src_ref=src, dst_ref=lands[t].at[slot], send_sem=send_sems.at[3 * t + j],
                    recv_sem=recv_sems.at[3 * t + j], device_id=(px, py, c), device_id_type=MESH)

            pairs.append((mk(out_slot), mk(in_slot)))
    return pairs


def ici_start(srcs, lands, after, exchange, name):
    n = len(srcs)

    def body(*refs):
        send_sems, recv_sems = refs[2 * n + 1], refs[2 * n + 2]
        for out_copy, _ in _peer_copies(exchange, refs[:n], refs[n:2 * n], send_sems, recv_sems):
            out_copy.start()
        refs[-1][...] = jnp.zeros_like(refs[-1])

    arrs = list(srcs) + list(lands)
    res = pl.pallas_call(
        body, name=name,
        out_shape=(pltpu.SemaphoreType.DMA((3 * n,)), pltpu.SemaphoreType.DMA((3 * n,)),
                   *[pltpu.HBM(a.shape, a.dtype) for a in arrs], jax.ShapeDtypeStruct((8, 128), F32)),
        in_specs=[HBM] * (2 * n) + [ANY],
        out_specs=(SEM, SEM, *[HBM] * (2 * n), pl.BlockSpec(memory_space=pltpu.VMEM)),
        input_output_aliases={t: 2 + t for t in range(2 * n)},
        compiler_params=pltpu.CompilerParams(has_side_effects=EFFECT),
    )(*[pltpu.with_memory_space_constraint(a, pltpu.HBM) for a in arrs], after)
    return res[0], res[1], list(res[2:2 + n]), list(res[2 + n:2 + 2 * n]), res[-1]


def ici_wait(started, after, exchange, name):
    send_sems, recv_sems, srcs, lands, _ = started
    n = len(srcs)

    def body(*refs):
        for out_copy, in_copy in _peer_copies(exchange, refs[:n], refs[n:2 * n], refs[2 * n], refs[2 * n + 1]):
            out_copy.wait_send()
            in_copy.wait_recv()

    arrs = list(srcs) + list(lands)
    res = pl.pallas_call(
        body, name=name,
        out_shape=tuple(pltpu.HBM(a.shape, a.dtype) for a in arrs),
        in_specs=[HBM] * (2 * n) + [SEM, SEM, ANY], out_specs=tuple([HBM] * (2 * n)),
        input_output_aliases={t: t for t in range(2 * n)},
        compiler_params=pltpu.CompilerParams(has_side_effects=EFFECT),
    )(*arrs, send_sems, recv_sems, after)
    return list(res[:n]), list(res[n:])


def sum_blocks(g, rb, name):
    cols = g.shape[-1]
    rows = g.size // (NSH * cols)
    tr = min(rows, 256)

    def body(g_ref, r1_ref, r2_ref, r3_ref, o_ref):
        acc = g_ref[...].astype(F32) + r1_ref[...].astype(F32) + r2_ref[...].astype(F32) + r3_ref[...].astype(F32)
        o_ref[...] = acc.astype(BF16)

    def mine(i):
        x, y, _ = _mesh_pos()
        return (2 * x + y, i, 0)

    def peer(fx, fy):
        def index(i):
            x, y, c = _mesh_pos()
            return (2 * (2 * (x ^ fx) + (y ^ fy)) + c, i, 0)
        return pl.BlockSpec((None, tr, cols), index)

    return pl.pallas_call(
        body, name=name, grid=(rows // tr,),
        in_specs=[pl.BlockSpec((None, tr, cols), mine), peer(1, 0), peer(0, 1), peer(1, 1)],
        out_specs=pl.BlockSpec((tr, cols), lambda i: (i, 0)),
        out_shape=jax.ShapeDtypeStruct((rows, cols), BF16), compiler_params=_cp(32),
    )(g.reshape(NSH, rows, cols), *[rb.reshape(8, rows, cols)] * 3)


def swap_sibling(ps, name):
    n = len(ps)

    def body(*refs):
        ins, outs, send_sems, recv_sems = refs[:n], refs[n:2 * n], refs[2 * n], refs[2 * n + 1]
        x, y, c = _mesh_pos()
        copies = [pltpu.make_async_remote_copy(
            src_ref=ins[t], dst_ref=outs[t], send_sem=send_sems.at[t], recv_sem=recv_sems.at[t],
            device_id=(x, y, 1 - c), device_id_type=MESH) for t in range(n)]
        for cp in copies:
            cp.start()
        for cp in copies:
            cp.wait_recv()
            cp.wait_send()

    return pl.pallas_call(
        body, name=name, out_shape=[jax.ShapeDtypeStruct(p.shape, p.dtype) for p in ps],
        in_specs=[ANY] * n, out_specs=[ANY] * n,
        scratch_shapes=[pltpu.SemaphoreType.DMA((n,)), pltpu.SemaphoreType.DMA((n,))],
    )(*ps)


def allgather8(v, name):
    m, n = v.shape

    def body(x_ref, out_ref, send_sems, recv_sems, local_sem):
        x, y, c = _mesh_pos()
        me, sibling = (x, y, c), (x, y, 1 - c)
        chips = [(1 - x, y), (x, 1 - y), (1 - x, 1 - y)]

        def rows(px, py, pc):
            return out_ref.at[4 * px + 2 * py + pc]

        def copy(k, block, to, src=None):
            return pltpu.make_async_remote_copy(
                src_ref=rows(*block) if src is None else src, dst_ref=rows(*block),
                send_sem=send_sems.at[k], recv_sem=recv_sems.at[k], device_id=to, device_id_type=MESH)

        mine = pltpu.make_async_copy(x_ref, rows(*me), local_sem)
        mine.start()
        first = [copy(0, me, sibling, src=x_ref)]
        first += [copy(1 + j, me, (*chip, c), src=x_ref) for j, chip in enumerate(chips)]
        for cp in first:
            cp.start()
        passed = [copy(4 + j, (*chip, c), sibling) for j, chip in enumerate(chips)]
        for j, chip in enumerate(chips):
            copy(1 + j, (*chip, c), me).wait_recv()
            passed[j].start()
        copy(0, sibling, me).wait_recv()
        for j, chip in enumerate(chips):
            copy(4 + j, (*chip, 1 - c), me).wait_recv()
        for cp in first + passed:
            cp.wait_send()
        mine.wait()

    return pl.pallas_call(
        body, name=name,
        out_shape=jax.ShapeDtypeStruct((8, m, n), v.dtype),
        in_specs=[pl.BlockSpec(memory_space=pltpu.VMEM)],
        out_specs=pl.BlockSpec(memory_space=pltpu.VMEM),
        scratch_shapes=[pltpu.SemaphoreType.DMA((7,)), pltpu.SemaphoreType.DMA((7,)), pltpu.SemaphoreType.DMA],
        compiler_params=_cp(40),
    )(v)


def _row_spec(width=D, off=0):
    return pl.BlockSpec((TR, width), lambda i, off=off: (i, off))


def _vec_spec(width=D):
    return pl.BlockSpec((1, width), lambda i: (0, 0))


def _mod_spec():
    return pl.BlockSpec((None, 6, D), lambda i: (jnp.minimum(i, 1), 0, 0))


def _pair_spec(width=D):
    return pl.BlockSpec((None, 1, width), lambda i: (jnp.minimum(i, 1), 0, 0))


def _accum(ref, val, first):
    @pl.when(first)
    def _():
        ref[...] = val

    @pl.when(jnp.logical_not(first))
    def _():
        ref[...] += val


def norm_mod(xs, g, mod, si, name):
    r = xs.shape[0]

    def body(x_ref, g_ref, m_ref, o_ref):
        x = x_ref[...]
        rstd = lax.rsqrt(jnp.mean(x * x, axis=-1, keepdims=True) + EPS)
        n = x * rstd * g_ref[...]
        o_ref[...] = (n * (1.0 + m_ref[si + 1:si + 2, :]) + m_ref[si:si + 1, :]).astype(BF16)

    return pl.pallas_call(
        body, name=name, grid=(r // TR,),
        in_specs=[_row_spec(), _vec_spec(), _mod_spec()], out_specs=_row_spec(),
        out_shape=jax.ShapeDtypeStruct((r, D), BF16), compiler_params=_cp(32),
    )(xs, g, mod)


def gate_in(dx, f, mod, gi, name):
    r = dx.shape[0]

    def body(dx_ref, f_ref, m_ref, o_ref, dg_ref):
        i = pl.program_id(0)
        dxv = dx_ref[...]
        o_ref[...] = (dxv * m_ref[gi:gi + 1, :]).astype(BF16)
        _accum(dg_ref, jnp.sum(dxv * f_ref[...].astype(F32), axis=0, keepdims=True), i <= 1)

    return pl.pallas_call(
        body, name=name, grid=(r // TR,),
        in_specs=[_row_spec(), _row_spec(), _mod_spec()], out_specs=[_row_spec(), _pair_spec()],
        out_shape=[jax.ShapeDtypeStruct((r, D), BF16), jax.ShapeDtypeStruct((2, 1, D), F32)],
        compiler_params=_cp(32),
    )(dx, f, mod)


def gate_in_pool(dx, ypre, mod, scale, gi, name):
    r = dx.shape[0]

    def body(dx_ref, y_ref, m_ref, s_ref, o_ref, dg_ref, ds_ref, db_ref):
        i = pl.program_id(0)
        dxv = dx_ref[...]
        yp = y_ref[...].astype(F32)
        sc = s_ref[...]
        dy = dxv * m_ref[gi:gi + 1, :]
        dyp = dy * sc
        o_ref[...] = dyp.astype(BF16)
        _accum(dg_ref, jnp.sum(dxv * (yp * sc), axis=0, keepdims=True), i <= 1)
        _accum(ds_ref, jnp.sum(dy * yp, axis=0, keepdims=True), i == 0)
        _accum(db_ref, jnp.sum(dyp, axis=0, keepdims=True), i == 0)

    return pl.pallas_call(
        body, name=name, grid=(r // TR,),
        in_specs=[_row_spec(), _row_spec(), _mod_spec(), _vec_spec()],
        out_specs=[_row_spec(), _pair_spec(), _vec_spec(), _vec_spec()],
        out_shape=[jax.ShapeDtypeStruct((r, D), BF16), jax.ShapeDtypeStruct((2, 1, D), F32),
                   jax.ShapeDtypeStruct((1, D), F32), jax.ShapeDtypeStruct((1, D), F32)],
        compiler_params=_cp(32),
    )(dx, ypre, mod, scale)


def normmod_bwd(dh, x, dxo, g, mod, si, name):
    r = x.shape[0]

    def body(dh_ref, x_ref, dxo_ref, g_ref, m_ref, dx_ref, dsh_ref, dsc_ref, dg_ref):
        i = pl.program_id(0)
        xv = x_ref[...]
        dhv = dh_ref[...]
        gv = g_ref[...]
        rstd = lax.rsqrt(jnp.mean(xv * xv, axis=-1, keepdims=True) + EPS)
        xhat = xv * rstd
        dn = dhv * (1.0 + m_ref[si + 1:si + 2, :])
        dxh = dn * gv
        dx_ref[...] = rstd * (dxh - xhat * jnp.mean(dxh * xhat, axis=-1, keepdims=True)) + dxo_ref[...]
        _accum(dsh_ref, jnp.sum(dhv, axis=0, keepdims=True), i <= 1)
        _accum(dsc_ref, jnp.sum(dhv * (xhat * gv), axis=0, keepdims=True), i <= 1)
        _accum(dg_ref, jnp.sum(dn * xhat, axis=0, keepdims=True), i == 0)

    return pl.pallas_call(
        body, name=name, grid=(r // TR,),
        in_specs=[_row_spec(), _row_spec(), _row_spec(), _vec_spec(), _mod_spec()],
        out_specs=[_row_spec(), _pair_spec(), _pair_spec(), _vec_spec()],
        out_shape=[jax.ShapeDtypeStruct((r, D), F32), jax.ShapeDtypeStruct((2, 1, D), F32),
                   jax.ShapeDtypeStruct((2, 1, D), F32), jax.ShapeDtypeStruct((1, D), F32)],
        compiler_params=_cp(48),
    )(dh, x, dxo, g, mod)


def final_loss(xs, g, target, name):
    r = xs.shape[0]

    def body(x_ref, g_ref, t_ref, dx_ref, loss_ref, dg_ref):
        i = pl.program_id(0)

        @pl.when(i == 0)
        def _():
            dx_ref[...] = jnp.zeros_like(dx_ref)
            loss_ref[...] = jnp.zeros_like(loss_ref)
            dg_ref[...] = jnp.zeros_like(dg_ref)

        @pl.when(i > 0)
        def _():
            xv = x_ref[...]
            gv = g_ref[...]
            rstd = lax.rsqrt(jnp.mean(xv * xv, axis=-1, keepdims=True) + EPS)
            xhat = xv * rstd
            err = xhat * gv - t_ref[...]
            part = 0.5 * jnp.sum(jnp.mean(err * err, axis=-1, keepdims=True), axis=0, keepdims=True)
            lane = lax.broadcasted_iota(jnp.int32, (1, 128), 1)
            loss_ref[...] += jnp.where(lane == 0, part, 0.0)
            dy = err * (1.0 / D)
            dg_ref[...] += jnp.sum(dy * xhat, axis=0, keepdims=True)
            dxh = dy * gv
            dx_ref[...] = rstd * (dxh - xhat * jnp.mean(dxh * xhat, axis=-1, keepdims=True))

    return pl.pallas_call(
        body, name=name, grid=(r // TR,),
        in_specs=[_row_spec(), _vec_spec(), pl.BlockSpec((TR, D), lambda i: (jnp.maximum(i - 1, 0), 0))],
        out_specs=[_row_spec(), pl.BlockSpec((1, 128), lambda i: (0, 0)), _vec_spec()],
        out_shape=[jax.ShapeDtypeStruct((r, D), F32), jax.ShapeDtypeStruct((1, 128), F32),
                   jax.ShapeDtypeStruct((1, D), F32)],
        compiler_params=_cp(32),
    )(xs, g, target)


def _mm(name, mode, a, b, grid, a_spec, b_spec, out_shapes, out_specs, epi, kaxis=None, acc=None,
        extras=(), vmem=48):
    ne, no = len(extras), len(out_shapes)
    dot = {"nn": _nn, "nt": _nt, "tn": _tn}[mode]
    nk = grid[kaxis] if kaxis is not None else 1

    def body(*refs):
        a_ref, b_ref = refs[0], refs[1]
        ex = refs[2:2 + ne]
        outs = refs[2 + ne:2 + ne + no]
        ids = [pl.program_id(ax) for ax in range(len(grid))]
        part = dot(a_ref[...], b_ref[...])
        if kaxis is None:
            epi(part, ids, ex, outs)
        else:
            acc_ref = refs[-1]
            k = ids[kaxis]

            @pl.when(k == 0)
            def _():
                acc_ref[...] = part

            @pl.when(k > 0)
            def _():
                acc_ref[...] += part

            @pl.when(k == nk - 1)
            def _():
                epi(acc_ref[...], ids, ex, outs)

    operands = [a, b] + [e[0] for e in extras]
    in_specs = [a_spec, b_spec] + [e[1] for e in extras]
    return pl.pallas_call(
        body, name=name, grid=grid, in_specs=in_specs, out_specs=out_specs, out_shape=out_shapes,
        scratch_shapes=[] if kaxis is None else [pltpu.VMEM(acc, F32)], compiler_params=_cp(vmem),
    )(*operands)


def _epi_store(acc, ids, ex, outs):
    outs[0][...] = acc.astype(outs[0].dtype)


def _epi_relu2(acc, ids, ex, outs):
    rl = jnp.maximum(acc, 0.0)
    outs[0][...] = (rl * rl).astype(BF16)


def _epi_2sqrt(acc, ids, ex, outs):
    outs[0][...] = (acc * (2.0 * jnp.sqrt(ex[0][...].astype(F32)))).astype(BF16)


def _gate_rows(ids, tm, shape, mod_ref, gi):
    rid = ids[0] * tm + lax.broadcasted_iota(jnp.int32, shape, 0)
    return jnp.where(rid < CTX, mod_ref[0, gi:gi + 1, :], mod_ref[1, gi:gi + 1, :])


def _epi_res(gi, tm):
    def epi(acc, ids, ex, outs):
        outs[0][...] = ex[0][...] + _gate_rows(ids, tm, acc.shape, ex[1], gi) * acc
        outs[1][...] = acc.astype(BF16)
    return epi


def _epi_pool(gi, tm):
    def epi(acc, ids, ex, outs):
        ypre = acc + ex[2][...]
        outs[0][...] = ex[0][...] + _gate_rows(ids, tm, acc.shape, ex[1], gi) * (ypre * ex[3][...])
        outs[1][...] = ypre.astype(BF16)
    return epi


def _tm(r):
    return 768 if r % 768 == 0 else TR


def _tm_wide(r):
    return 1408 if r % 1408 == 0 else _tm(r)


def mm_cols(a, w, l, name, epi=_epi_store, out_dtype=F32, per=2):
    r, k = a.shape
    c = w.shape[3]
    tm, tn = _tm_wide(r), c // per
    return _mm(name, "nn", a, w, (r // tm, NSH * per),
               pl.BlockSpec((tm, k), lambda i, j: (i, 0)),
               pl.BlockSpec((None, None, k, tn), lambda i, j: (j // per, l, 0, j % per)),
               [jax.ShapeDtypeStruct((r, NSH * c), out_dtype)], [pl.BlockSpec((tm, tn), lambda i, j: (i, j))],
               epi, vmem=56)[0]


def mm_rows_res(a, w, l, res, mod, gi, name):
    r = a.shape[0]
    kc = w.shape[2]
    tm, tn = _tm(r), 1024
    return _mm(name, "nn", a, w, (r // tm, D // tn, NSH),
               pl.BlockSpec((tm, kc), lambda i, j, k: (i, k)),
               pl.BlockSpec((None, None, kc, tn), lambda i, j, k: (k, l, 0, j)),
               [jax.ShapeDtypeStruct((r, D), F32), jax.ShapeDtypeStruct((r, D), BF16)],
               [pl.BlockSpec((tm, tn), lambda i, j, k: (i, j))] * 2,
               _epi_res(gi, tm), kaxis=2, acc=(tm, tn),
               extras=[(res, pl.BlockSpec((tm, tn), lambda i, j, k: (i, j))),
                       (mod, pl.BlockSpec((2, 6, tn), lambda i, j, k: (0, 0, j)))], vmem=56)


def mm_rows(a, w, l, name):
    r = a.shape[0]
    kc = w.shape[2]
    tm, tn = _tm(r), 1024
    return _mm(name, "nn", a, w, (r // tm, D // tn, NSH),
               pl.BlockSpec((tm, kc), lambda i, j, k: (i, k)),
               pl.BlockSpec((None, None, kc, tn), lambda i, j, k: (k, l, 0, j)),
               [jax.ShapeDtypeStruct((r, D), F32)], [pl.BlockSpec((tm, tn), lambda i, j, k: (i, j))],
               _epi_store, kaxis=2, acc=(tm, tn), vmem=56)[0]


def mm_grp_res(z, w, o, bias, scale, res, mod, gi, name):
    r = z.shape[0]
    tm = _tm(r)
    return _mm(name, "nn", z, w, (r // tm, 4, NSH),
               pl.BlockSpec((tm, 128), lambda i, g, k: (i, 4 * g + k)),
               pl.BlockSpec((None, None, None, 128, 512), lambda i, g, k: (k, o, g, 0, 0)),
               [jax.ShapeDtypeStruct((r, D), F32), jax.ShapeDtypeStruct((r, D), BF16)],
               [pl.BlockSpec((tm, 512), lambda i, g, k: (i, g))] * 2,
               _epi_pool(gi, tm), kaxis=2, acc=(tm, 512),
               extras=[(res, pl.BlockSpec((tm, 512), lambda i, g, k: (i, g))),
                       (mod, pl.BlockSpec((2, 6, 512), lambda i, g, k: (0, 0, g))),
                       (bias, pl.BlockSpec((1, 512), lambda i, g, k: (0, g))),
                       (scale, pl.BlockSpec((1, 512), lambda i, g, k: (0, g)))], vmem=48)


def mm_t_rows(a, w, l, name, epi=_epi_store, out_dtype=F32, extras_of=None, per=1):
    r, n = a.shape
    kc = w.shape[2]
    tm, tn = _tm_wide(r), kc // per
    extras = []
    if extras_of is not None:
        extras = [(extras_of, pl.BlockSpec((tm, tn), lambda i, j: (i, j)))]
    return _mm(name, "nt", a, w, (r // tm, NSH * per),
               pl.BlockSpec((tm, n), lambda i, j: (i, 0)),
               pl.BlockSpec((None, None, tn, n), lambda i, j: (j // per, l, j % per, 0)),
               [jax.ShapeDtypeStruct((r, NSH * kc), out_dtype)], [pl.BlockSpec((tm, tn), lambda i, j: (i, j))],
               epi, extras=extras, vmem=56)[0]


def mm_t_cols(a, w, l, name):
    r = a.shape[0]
    k, c = w.shape[2], w.shape[3]
    tm, tn = _tm_wide(r), 1024
    return _mm(name, "nt", a, w, (r // tm, k // tn, NSH),
               pl.BlockSpec((tm, c), lambda i, j, s: (i, s)),
               pl.BlockSpec((None, None, tn, c), lambda i, j, s: (s, l, j, 0)),
               [jax.ShapeDtypeStruct((r, k), F32)], [pl.BlockSpec((tm, tn), lambda i, j, s: (i, j))],
               _epi_store, kaxis=2, acc=(tm, tn), vmem=56)[0]


def mm_t_grp(dy, w, o, name):
    r = dy.shape[0]
    tm = _tm(r)
    return _mm(name, "nt", dy, w, (r // tm, 4, NSH),
               pl.BlockSpec((tm, 512), lambda i, g, s: (i, g)),
               pl.BlockSpec((None, None, None, 128, 512), lambda i, g, s: (s, o, g, 0, 0)),
               [jax.ShapeDtypeStruct((r, D), F32)], [pl.BlockSpec((tm, 128), lambda i, g, s: (i, 4 * g + s))],
               _epi_store, vmem=40)[0]


def grad_cols(a, b, name, ta=1024, per=2):
    r, k = a.shape
    c = b.shape[1] // NSH
    tk, tn = _tm(r), c // per
    return _mm(name, "tn", a, b, (NSH, k // ta, per, r // tk),
               pl.BlockSpec((tk, ta), lambda s, i, j, t: (t, i)),
               pl.BlockSpec((tk, tn), lambda s, i, j, t: (t, s * per + j)),
               [jax.ShapeDtypeStruct((NSH, 1, k, c), BF16)],
               [pl.BlockSpec((None, None, ta, tn), lambda s, i, j, t: (s, 0, i, j))],
               _epi_store, kaxis=3, acc=(ta, tn), vmem=56)[0]


def grad_rows(a, b, name, per=1, tn=1024):
    r = a.shape[0]
    kc, n = a.shape[1] // NSH, b.shape[1]
    tk, ta = _tm(r), kc // per
    return _mm(name, "tn", a, b, (NSH, per, n // tn, r // tk),
               pl.BlockSpec((tk, ta), lambda s, i, j, t: (t, s * per + i)),
               pl.BlockSpec((tk, tn), lambda s, i, j, t: (t, j)),
               [jax.ShapeDtypeStruct((NSH, 1, kc, n), BF16)],
               [pl.BlockSpec((None, None, ta, tn), lambda s, i, j, t: (s, 0, i, j))],
               _epi_store, kaxis=3, acc=(ta, tn), vmem=56)[0]


def grad_grp(z, dy, name):
    r = z.shape[0]
    tk = _tm(r)
    return _mm(name, "tn", z, dy, (NSH, 4, r // tk),
               pl.BlockSpec((tk, 128), lambda s, g, t: (t, 4 * g + s)),
               pl.BlockSpec((tk, 512), lambda s, g, t: (t, g)),
               [jax.ShapeDtypeStruct((NSH, 1, 4, 128, 512), BF16)],
               [pl.BlockSpec((None, None, None, 128, 512), lambda s, g, t: (s, 0, g, 0, 0))],
               _epi_store, kaxis=2, acc=(128, 512), vmem=40)[0]


def _scan_tile(reverse, nt):
    if reverse:
        return lambda p: jnp.where(p == 0, 0, nt - p)
    return lambda p: p


def _gates(f, lbv):
    sg = jax.nn.sigmoid(f)
    fg = lbv + (1.0 - lbv) * sg
    g = jnp.log(jnp.maximum(fg, LOG_FLOOR))
    kk = (1.0 - lbv) * jax.nn.sigmoid(-f)
    return sg, fg, g, kk


def _chunk_cumsum(g, reverse, sub):
    n = g.shape[0]
    rr = lax.broadcasted_iota(jnp.int32, (n, n), 0)
    cc = lax.broadcasted_iota(jnp.int32, (n, n), 1)
    inside = (rr // sub) == (cc // sub)
    tri = jnp.where(inside & ((cc >= rr) if reverse else (cc <= rr)), 1.0, 0.0).astype(F32)
    return jnp.dot(tri, g, precision=HIGHEST, preferred_element_type=F32)


def _decay(b, s, rows, reverse):
    return jnp.where((rows <= s) if reverse else (rows >= s), jnp.exp(b - b[s:s + 1]), 0.0)


def hgrn_fwd(p, lb, reverse, name):
    SUB = SUB_FWD
    r = p.shape[0]
    nt = r // TR
    nsub = TR // SUB
    fcol = 2 if reverse else 1
    tile = _scan_tile(reverse, nt)

    def body(q_ref, f_ref, v_ref, lb_ref, o_ref, sin_ref, st, k_s, b_s):
        i = pl.program_id(0)

        @pl.when(i == 0)
        def _():
            st[...] = jnp.zeros_like(st)

        sin_ref[...] = st[...]
        _, _, g, kk = _gates(f_ref[...], lb_ref[...])
        k_s[...] = kk
        b_s[...] = _chunk_cumsum(g, reverse, SUB)
        rows = lax.broadcasted_iota(jnp.int32, (SUB, DK), 0)

        def sub(jj, carry):
            j = (nsub - 1 - jj) if reverse else jj
            rs = pl.ds(pl.multiple_of(j * SUB, SUB), SUB)
            for h in range(NH):
                sl = slice(h * DK, (h + 1) * DK)
                q, k, b, v = q_ref[rs, sl], k_s[rs, sl], b_s[rs, sl], v_ref[rs, sl]
                btot = b[0:1] if reverse else b[SUB - 1:SUB]
                o = _nt(_bf(q * jnp.exp(b)), _bf(st[h]))
                for s in range(SUB):
                    col = jnp.sum(q * k[s:s + 1] * _decay(b, s, rows, reverse), axis=-1, keepdims=True)
                    o = o + col * v[s:s + 1]
                o_ref[rs, sl] = o
                st[h] = st[h] * jnp.exp(btot) + _tn(_bf(v), _bf(k * jnp.exp(btot - b)))
            return carry

        lax.fori_loop(0, nsub, sub, 0)

    seg = lambda col: pl.BlockSpec((TR, AW), lambda i, col=col: (tile(i), col))
    return pl.pallas_call(
        body, name=name, grid=(nt,),
        in_specs=[seg(0), seg(fcol), seg(3), _vec_spec(AW)],
        out_specs=[pl.BlockSpec((TR, AW), lambda i: (tile(i), 0)),
                   pl.BlockSpec((None, NH, DK, DK), lambda i: (tile(i), 0, 0, 0))],
        out_shape=[jax.ShapeDtypeStruct((r, AW), F32), jax.ShapeDtypeStruct((nt, NH, DK, DK), F32)],
        scratch_shapes=[pltpu.VMEM((NH, DK, DK), F32), pltpu.VMEM((TR, AW), F32), pltpu.VMEM((TR, AW), F32)],
        compiler_params=_cp(40),
    )(p, p, p, lb)


def hgrn_bwd(p, do, sin, lb, reverse, name, add=None):
    SUB = SUB_BWD
    r = p.shape[0]
    nt = r // TR
    nsub = TR // SUB
    fcol = 2 if reverse else 1
    tile0 = _scan_tile(reverse, nt)
    tile = lambda i: tile0(nt - 1 - i)
    nadd = 0 if add is None else 2
    out_dt = F32 if add is None else BF16

    def body(*refs):
        q_ref, f_ref, v_ref, do_ref, sin_ref, lb_ref = refs[:6]
        adds = refs[6:6 + nadd]
        dq_ref, dv_ref, df_ref, dlb_ref = refs[6 + nadd:10 + nadd]
        dst, srun, ssub, k_s, b_s, sg_s, fg_s = refs[10 + nadd:]
        i = pl.program_id(0)

        @pl.when(i == 0)
        def _():
            dst[...] = jnp.zeros_like(dst)
            dlb_ref[...] = jnp.zeros_like(dlb_ref)

        lbv = lb_ref[...]
        sg, fg, g, kk = _gates(f_ref[...], lbv)
        k_s[...] = kk
        sg_s[...] = sg
        fg_s[...] = fg
        b_s[...] = _chunk_cumsum(g, reverse, SUB)
        srun[...] = sin_ref[...]
        rows = lax.broadcasted_iota(jnp.int32, (SUB, DK), 0)
        r16 = lax.broadcasted_iota(jnp.int32, (SUB, SUB), 0)
        c16 = lax.broadcasted_iota(jnp.int32, (SUB, SUB), 1)
        later = jnp.where((c16 <= r16) if reverse else (c16 >= r16), 1.0, 0.0).astype(F32)

        def recompute(jj, c):
            j = (nsub - 1 - jj) if reverse else jj
            rs = pl.ds(pl.multiple_of(j * SUB, SUB), SUB)
            for h in range(NH):
                sl = slice(h * DK, (h + 1) * DK)
                k, b, v = k_s[rs, sl], b_s[rs, sl], v_ref[rs, sl]
                btot = b[0:1] if reverse else b[SUB - 1:SUB]
                ssub[jj, h] = srun[h]
                srun[h] = srun[h] * jnp.exp(btot) + _tn(_bf(v), _bf(k * jnp.exp(btot - b)))
            return c

        lax.fori_loop(0, nsub, recompute, 0)
        for h in range(NH):
            ssub[nsub, h] = srun[h]

        def back(jj, c):
            pos = nsub - 1 - jj
            j = jj if reverse else pos
            rs = pl.ds(pl.multiple_of(j * SUB, SUB), SUB)
            for h in range(NH):
                sl = slice(h * DK, (h + 1) * DK)
                q, k, b, v, dov = q_ref[rs, sl], k_s[rs, sl], b_s[rs, sl], v_ref[rs, sl], do_ref[rs, sl]
                btot = b[0:1] if reverse else b[SUB - 1:SUB]
                s0 = ssub[pos, h]
                ds = dst[h]
                dg_next = jnp.sum(ds * ssub[pos + 1, h], axis=0, keepdims=True)
                eb = jnp.exp(b)
                ebt = jnp.exp(btot - b)
                ke = k * ebt
                dq = _nn(_bf(dov), _bf(s0)) * eb
                dk = _nn(_bf(v), _bf(ds)) * ebt
                dv = _nt(_bf(ke), _bf(ds))
                for s in range(SUB):
                    dec = _decay(b, s, rows, reverse)
                    dsc = jnp.sum(dov * v[s:s + 1], axis=-1, keepdims=True)
                    qd = q * dec
                    dq = dq + (dsc * dec) * k[s:s + 1]
                    dk_row = jnp.sum(dsc * qd, axis=0, keepdims=True)
                    sc = jnp.sum(qd * k[s:s + 1], axis=-1, keepdims=True)
                    dv_row = jnp.sum(sc * dov, axis=0, keepdims=True)
                    dk = dk + jnp.where(rows == s, dk_row, 0.0)
                    dv = dv + jnp.where(rows == s, dv_row, 0.0)
                dst[h] = ds * jnp.exp(btot) + _tn(_bf(dov), _bf(q * eb))
                dg = jnp.dot(later, q * dq - k * dk, precision=HIGHEST, preferred_element_type=F32) + dg_next
                sgv, fgv, lbh = sg_s[rs, sl], fg_s[rs, sl], lbv[:, sl]
                dfg = jnp.where(fgv > LOG_FLOOR, dg / fgv, 0.0)
                df_ref[rs, sl] = ((1.0 - lbh) * sgv * (1.0 - sgv) * (dfg - dk)).astype(BF16)
                dlb_ref[:, sl] += jnp.sum((dfg - dk) * (1.0 - sgv), axis=0, keepdims=True)
                if add is None:
                    dq_ref[rs, sl] = dq
                    dv_ref[rs, sl] = dv
                else:
                    dq_ref[rs, sl] = (dq + adds[0][rs, sl]).astype(BF16)
                    dv_ref[rs, sl] = (dv + adds[1][rs, sl]).astype(BF16)
            return c

        lax.fori_loop(0, nsub, back, 0)

    seg = lambda col: pl.BlockSpec((TR, AW), lambda i, col=col: (tile(i), col))
    plain = pl.BlockSpec((TR, AW), lambda i: (tile(i), 0))
    operands = [p, p, p, do, sin, lb]
    in_specs = [seg(0), seg(fcol), seg(3), plain,
                pl.BlockSpec((None, NH, DK, DK), lambda i: (tile(i), 0, 0, 0)), _vec_spec(AW)]
    if add is not None:
        operands += list(add)
        in_specs += [plain, plain]
    return pl.pallas_call(
        body, name=name, grid=(nt,), in_specs=in_specs,
        out_specs=[plain, plain, plain, _vec_spec(AW)],
        out_shape=[jax.ShapeDtypeStruct((r, AW), out_dt), jax.ShapeDtypeStruct((r, AW), out_dt),
                   jax.ShapeDtypeStruct((r, AW), BF16), jax.ShapeDtypeStruct((1, AW), F32)],
        scratch_shapes=[pltpu.VMEM((NH, DK, DK), F32), pltpu.VMEM((NH, DK, DK), F32),
                        pltpu.VMEM((nsub + 1, NH, DK, DK), F32)]
        + [pltpu.VMEM((TR, AW), F32)] * 4,
        compiler_params=_cp(56),
    )(*operands)


def _silu(v):
    return v * jax.nn.sigmoid(v)


def readout_fwd(of, ob, p, ng, name):
    r = of.shape[0]

    def body(of_ref, ob_ref, g_ref, ng_ref, y_ref):
        for h in range(NH):
            sl = slice(h * DK, (h + 1) * DK)
            o = of_ref[:, sl] + ob_ref[:, sl]
            on = o * lax.rsqrt(jnp.mean(o * o, axis=-1, keepdims=True) + EPS) * ng_ref[:, sl]
            y_ref[:, sl] = (on * _silu(g_ref[:, sl])).astype(BF16)

    return pl.pallas_call(
        body, name=name, grid=(r // TR,),
        in_specs=[_row_spec(AW), _row_spec(AW), _row_spec(AW, 4), _vec_spec(AW)], out_specs=_row_spec(AW),
        out_shape=jax.ShapeDtypeStruct((r, AW), BF16), compiler_params=_cp(32),
    )(of, ob, p, ng)


def readout_bwd(dy, of, ob, p, ng, name):
    r = of.shape[0]

    def body(dy_ref, of_ref, ob_ref, g_ref, ng_ref, do_ref, dg_ref, dn_ref):
        i = pl.program_id(0)

        @pl.when(i == 0)
        def _():
            dn_ref[...] = jnp.zeros_like(dn_ref)

        for h in range(NH):
            sl = slice(h * DK, (h + 1) * DK)
            o = of_ref[:, sl] + ob_ref[:, sl]
            rstd = lax.rsqrt(jnp.mean(o * o, axis=-1, keepdims=True) + EPS)
            oh = o * rstd
            gv = g_ref[:, sl]
            sig = jax.nn.sigmoid(gv)
            dyv = dy_ref[:, sl]
            don = dyv * (gv * sig)
            dg_ref[:, sl] = (dyv * (oh * ng_ref[:, sl]) * (sig * (1.0 + gv * (1.0 - sig)))).astype(BF16)
            dn_ref[:, sl] += jnp.sum(don * oh, axis=0, keepdims=True)
            doh = don * ng_ref[:, sl]
            do_ref[:, sl] = rstd * (doh - oh * jnp.mean(doh * oh, axis=-1, keepdims=True))

    return pl.pallas_call(
        body, name=name, grid=(r // TR,),
        in_specs=[_row_spec(AW), _row_spec(AW), _row_spec(AW), _row_spec(AW, 4), _vec_spec(AW)],
        out_specs=[_row_spec(AW), _row_spec(AW), _vec_spec(AW)],
        out_shape=[jax.ShapeDtypeStruct((r, AW), F32), jax.ShapeDtypeStruct((r, AW), BF16),
                   jax.ShapeDtypeStruct((1, AW), F32)],
        compiler_params=_cp(32),
    )(dy, of, ob, p, ng)


def _gelu(v):
    return 0.5 * v * (1.0 + lax.erf(v * 0.7071067811865476))


def _gelu_grad(v):
    return 0.5 * (1.0 + lax.erf(v * 0.7071067811865476)) + v * (0.3989422804014327 * jnp.exp(-0.5 * v * v))


def _cmlp_norm(vv, gn):
    vg = _gelu(vv)
    mu = jnp.mean(vg, axis=-1, keepdims=True)
    cen = vg - mu
    rstd = lax.rsqrt(jnp.mean(cen * cen, axis=-1, keepdims=True) + EPS)
    xhat = cen * rstd
    return xhat, rstd, xhat * gn


def chunkmlp_fwd(p, ws, bias, gn, name):
    r = p.shape[0]

    def body(u_ref, v_ref, ws_ref, b_ref, gn_ref, y_ref):
        for ci in range(TR // B_CHUNK):
            rs = slice(ci * B_CHUNK, (ci + 1) * B_CHUNK)
            for gidx in range(NH):
                sl = slice(gidx * DK, (gidx + 1) * DK)
                _, _, vn = _cmlp_norm(v_ref[rs, sl], gn_ref[:, sl])
                mixed = _nn(_bf(ws_ref[gidx]), _bf(vn)) + b_ref[gidx]
                y_ref[rs, sl] = (_gelu(u_ref[rs, sl]) * mixed).astype(BF16)

    return pl.pallas_call(
        body, name=name, grid=(r // TR,),
        in_specs=[_row_spec(AW, 5), _row_spec(AW, 6), pl.BlockSpec((NH, B_CHUNK, B_CHUNK), lambda i: (0, 0, 0)),
                  pl.BlockSpec((NH, B_CHUNK, 1), lambda i: (0, 0, 0)), _vec_spec(AW)],
        out_specs=_row_spec(AW), out_shape=jax.ShapeDtypeStruct((r, AW), BF16), compiler_params=_cp(32),
    )(p, p, ws, bias, gn)


def chunkmlp_bwd(dy, p, ws, bias, gn, name):
    r = p.shape[0]

    def body(dy_ref, u_ref, v_ref, ws_ref, b_ref, gn_ref, du_ref, dv_ref, dws_ref, db_ref, dgn_ref):
        i = pl.program_id(0)

        @pl.when(i == 0)
        def _():
            dws_ref[...] = jnp.zeros_like(dws_ref)
            db_ref[...] = jnp.zeros_like(db_ref)
            dgn_ref[...] = jnp.zeros_like(dgn_ref)

        for ci in range(TR // B_CHUNK):
            rs = slice(ci * B_CHUNK, (ci + 1) * B_CHUNK)
            for gidx in range(NH):
                sl = slice(gidx * DK, (gidx + 1) * DK)
                vv, uv, dyv, gnv = v_ref[rs, sl], u_ref[rs, sl], dy_ref[rs, sl], gn_ref[:, sl]
                xhat, rstd, vn = _cmlp_norm(vv, gnv)
                wg = _bf(ws_ref[gidx])
                mixed = _nn(wg, _bf(vn)) + b_ref[gidx]
                dmixed = dyv * _gelu(uv)
                du_ref[rs, sl] = (dyv * mixed * _gelu_grad(uv)).astype(BF16)
                dws_ref[gidx] += _nt(_bf(dmixed), _bf(vn))
                db_ref[gidx] += jnp.sum(dmixed, axis=-1, keepdims=True)
                dvn = _tn(wg, _bf(dmixed))
                dgn_ref[:, sl] += jnp.sum(dvn * xhat, axis=0, keepdims=True)
                dxh = dvn * gnv
                dvg = rstd * (dxh - jnp.mean(dxh, axis=-1, keepdims=True)
                              - xhat * jnp.mean(dxh * xhat, axis=-1, keepdims=True))
                dv_ref[rs, sl] = (dvg * _gelu_grad(vv)).astype(BF16)

    return pl.pallas_call(
        body, name=name, grid=(r // TR,),
        in_specs=[_row_spec(AW, 1), _row_spec(AW, 5), _row_spec(AW, 6),
                  pl.BlockSpec((NH, B_CHUNK, B_CHUNK), lambda i: (0, 0, 0)),
                  pl.BlockSpec((NH, B_CHUNK, 1), lambda i: (0, 0, 0)), _vec_spec(AW)],
        out_specs=[_row_spec(AW), _row_spec(AW), pl.BlockSpec((NH, B_CHUNK, B_CHUNK), lambda i: (0, 0, 0)),
                   pl.BlockSpec((NH, B_CHUNK, 1), lambda i: (0, 0, 0)), _vec_spec(AW)],
        out_shape=[jax.ShapeDtypeStruct((r, AW), BF16), jax.ShapeDtypeStruct((r, AW), BF16),
                   jax.ShapeDtypeStruct((NH, B_CHUNK, B_CHUNK), F32), jax.ShapeDtypeStruct((NH, B_CHUNK, 1), F32),
                   jax.ShapeDtypeStruct((1, AW), F32)],
        compiler_params=_cp(32),
    )(dy, p, p, ws, bias, gn)


def _win_count(pos, k, n):
    lo = jnp.maximum(pos - k // 2, 0)
    hi = jnp.minimum(pos - k // 2 + k, n)
    return (hi - lo).astype(F32)


POOL_CW = 256
POOL_PAD = (POOL_WINDOWS[-1] // 2) * GRID_W


def pool_op(p, transpose, name):
    r = p.shape[0]
    seq = r - CTX
    grows = seq // GRID_W
    nt = seq // TR

    def body(x_ref, o_ref, y_s):
        j = pl.program_id(0)

        @pl.when(j == 0)
        def _():
            y_s[:POOL_PAD, :] = jnp.zeros((POOL_PAD, POOL_CW), F32)
            y_s[POOL_PAD + seq:, :] = jnp.zeros((POOL_PAD, POOL_CW), F32)

        for gi, k in enumerate(POOL_WINDOWS):
            @pl.when(j // (512 // POOL_CW) == gi)
            def _(k=k):
                offs = list(range(-(k // 2) + 1, k // 2 + 1) if transpose else range(-(k // 2), k // 2))
                tt = lax.broadcasted_iota(jnp.int32, (TR, TR), 0)
                ss = lax.broadcasted_iota(jnp.int32, (TR, TR), 1)
                band = (ss - tt >= offs[0]) & (ss - tt <= offs[-1])
                b_ctx = jnp.where(band, 1.0, 0.0).astype(F32)
                b_grid = jnp.where(band & ((tt >> 6) == (ss >> 6)), 1.0, 0.0).astype(F32)
                trow = lax.broadcasted_iota(jnp.int32, (TR, POOL_CW), 0)

                def count(i):
                    t = i * TR + trow
                    return _win_count(t & (GRID_W - 1), k, GRID_W) * _win_count(t >> 6, k, grows)

                def col_pass(i, carry):
                    xt = x_ref[pl.ds(pl.multiple_of(CTX + i * TR, TR), TR), :]
                    if transpose:
                        xt = xt / count(i)
                    y_s[pl.ds(pl.multiple_of(POOL_PAD + i * TR, TR), TR), :] = jnp.dot(
                        b_grid, xt, precision=HIGHEST, preferred_element_type=F32)
                    return carry

                lax.fori_loop(0, nt, col_pass, 0)

                def row_pass(i, carry):
                    base = POOL_PAD + i * TR
                    acc = y_s[pl.ds(pl.multiple_of(base + offs[0] * GRID_W, GRID_W), TR), :]
                    for d in offs[1:]:
                        acc = acc + y_s[pl.ds(pl.multiple_of(base + d * GRID_W, GRID_W), TR), :]
                    rows = pl.ds(pl.multiple_of(CTX + i * TR, TR), TR)
                    if not transpose:
                        acc = acc / count(i)
                    o_ref[rows, :] = (acc - x_ref[rows, :]).astype(BF16)
                    return carry

                lax.fori_loop(0, nt, row_pass, 0)

                cx = x_ref[:CTX, :]
                cntc = _win_count(trow, k, CTX)
                accc = jnp.dot(b_ctx, cx / cntc if transpose else cx, precision=HIGHEST, preferred_element_type=F32)
                o_ref[:CTX, :] = ((accc if transpose else accc / cntc) - cx).astype(BF16)

    spec = pl.BlockSpec((r, POOL_CW), lambda j: (0, j))
    return pl.pallas_call(
        body, name=name, grid=(D // POOL_CW,), in_specs=[spec], out_specs=spec,
        out_shape=jax.ShapeDtypeStruct((r, D), BF16),
        scratch_shapes=[pltpu.VMEM((seq + 2 * POOL_PAD, POOL_CW), F32)], compiler_params=_cp(56),
    )(p)


def ada_mods(cs, w_ada, b_loc, name):
    nl, _, cl = w_ada.shape
    tn = 1024

    def body(c_ref, w_ref, b_ref, o_ref):
        o_ref[...] = _nn(_bf(_silu(c_ref[...])), _bf(w_ref[...])) + b_ref[...]

    return pl.pallas_call(
        body, name=name, grid=(nl, cl // tn),
        in_specs=[pl.BlockSpec((16, D), lambda l, j: (0, 0)), pl.BlockSpec((None, D, tn), lambda l, j: (l, 0, j)),
                  pl.BlockSpec((None, 1, tn), lambda l, j: (l, 0, j))],
        out_specs=pl.BlockSpec((None, 16, tn), lambda l, j: (l, 0, j)),
        out_shape=jax.ShapeDtypeStruct((nl, 16, cl), F32), compiler_params=_cp(40),
    )(cs, w_ada, b_loc)


def _adamw(w, g, m, v):
    m = ADAM_B1 * m + (1.0 - ADAM_B1) * g
    v = ADAM_B2 * v + (1.0 - ADAM_B2) * (g * g)
    m_hat = m / (1.0 - ADAM_B1 ** ADAM_STEP)
    v_hat = v / (1.0 - ADAM_B2 ** ADAM_STEP)
    delta = -ADAM_LR * (m_hat / (jnp.sqrt(v_hat) + ADAM_EPS) + ADAM_WD * w)
    return delta, m, v


def ada_update(cs, dm, w, m, v, name):
    nl, _, cl = w.shape
    ta, tn = 256, 1024

    def body(c_ref, dm_ref, w_ref, m_ref, v_ref, g_ref, d_ref, nm_ref, nv_ref, dc_ref):
        l, j = pl.program_id(1), pl.program_id(2)
        a = _bf(_silu(c_ref[...]))
        bmat = _bf(dm_ref[...])
        wv = w_ref[...]
        g = _tn(a, bmat)
        g_ref[...] = g
        d_ref[...], nm_ref[...], nv_ref[...] = _adamw(wv, g, m_ref[...], v_ref[...])
        _accum(dc_ref, _nt(bmat, _bf(wv)), (l == 0) & (j == 0))

    wspec = pl.BlockSpec((None, ta, tn), lambda i, l, j: (l, i, j))
    shp = jax.ShapeDtypeStruct(w.shape, F32)
    return pl.pallas_call(
        body, name=name, grid=(D // ta, nl, cl // tn),
        in_specs=[pl.BlockSpec((16, ta), lambda i, l, j: (0, i)),
                  pl.BlockSpec((None, 16, tn), lambda i, l, j: (l, 0, j)), wspec, wspec, wspec],
        out_specs=[wspec, wspec, wspec, wspec, pl.BlockSpec((16, ta), lambda i, l, j: (0, i))],
        out_shape=[shp, shp, shp, shp, jax.ShapeDtypeStruct((16, D), F32)], compiler_params=_cp(40),
    )(cs, dm, w, m, v)


def adamw_big(p, q, w, m, v, l, dsts, name):
    nl = w.shape[0]
    rows, cols = p.shape
    tr = min(rows, 256)
    w3, m3, v3 = (t.reshape(nl, rows, cols) for t in (w, m, v))
    nd = 0 if dsts is None else 4

    def body(p_ref, q_ref, w_ref, m_ref, v_ref, *rest):
        g_ref, d_ref, nm_ref, nv_ref = rest[nd:]
        g = p_ref[...].astype(F32) + q_ref[...].astype(F32)
        g_ref[...] = g
        d_ref[...], nm_ref[...], nv_ref[...] = _adamw(w_ref[...], g, m_ref[...], v_ref[...])

    part = pl.BlockSpec((tr, cols), lambda i: (i, 0))
    spec = pl.BlockSpec((None, tr, cols), lambda i: (l, i, 0))
    shp = jax.ShapeDtypeStruct((nl, rows, cols), F32)
    return pl.pallas_call(
        body, name=name, grid=(rows // tr,),
        in_specs=[part, part, spec, spec, spec] + [ANY] * nd,
        out_specs=[spec] * 4, out_shape=[shp] * 4,
        input_output_aliases={5 + t: t for t in range(nd)}, compiler_params=_cp(48),
    )(p, q, w3, m3, v3, *([] if dsts is None else dsts))


def sum8(g8, name):
    n = g8.shape[1]

    def body(g_ref, o_ref):
        acc = g_ref[0]
        for dev in range(1, 8):
            acc = acc + g_ref[dev]
        o_ref[...] = acc

    return pl.pallas_call(
        body, name=name, grid=(1,), in_specs=[pl.BlockSpec((8, n, 128), lambda i: (0, 0, 0))],
        out_specs=pl.BlockSpec((n, 128), lambda i: (0, 0)),
        out_shape=jax.ShapeDtypeStruct((n, 128), F32), compiler_params=_cp(48),
    )(g8)


def adamw_small(g, w, m, v, name):
    def body(g_ref, w_ref, m_ref, v_ref, d_ref, nm_ref, nv_ref):
        d_ref[...], nm_ref[...], nv_ref[...] = _adamw(w_ref[...], g_ref[...], m_ref[...], v_ref[...])

    spec = pl.BlockSpec(g.shape, lambda i: (0, 0))
    shp = jax.ShapeDtypeStruct(g.shape, F32)
    return pl.pallas_call(
        body, name=name, grid=(1,), in_specs=[spec] * 4, out_specs=[spec] * 3, out_shape=[shp] * 3,
        compiler_params=_cp(48),
    )(g, w, m, v)


def _pack(arrs):
    flat = jnp.concatenate([a.reshape(-1).astype(F32) for a in arrs])
    pad = (-flat.shape[0]) % 1024
    return jnp.pad(flat, (0, pad)).reshape(-1, 128)


def _unpack(packed, shapes):
    flat = packed.reshape(-1)
    out, off = [], 0
    for s in shapes:
        n = 1
        for d in s:
            n *= d
        out.append(flat[off:off + n].reshape(s))
        off += n
    return out


def _lower_bounds(lb_logits):
    pr = jax.nn.softmax(lb_logits.astype(F32), axis=1)
    return jnp.cumsum(pr, axis=1) - pr[:, :1]


class LayerWeights:
    def __init__(self, fetch):
        self.fetch, self.have, self.tokens = fetch, {}, []

    def get(self, key, after):
        if key not in self.have:
            new, token = self.fetch(key, after)
            self.have.update(new)
            if token is not None:
                self.tokens.append(token)
        return self.have[key]

    def tie(self, mod):
        for token in self.tokens:
            mod = mod + token[0, 0]
        self.tokens = []
        return mod


def device_step(xs, target, mods, sp, weights_for, grads_done):
    lbs = _lower_bounds(sp["lb_logits"])
    saved = []
    for layer in range(4):
        wl, md = weights_for(layer), mods[layer]
        w_in = wl.get("in" if layer % 2 == 0 else "pin", xs)
        md = wl.tie(md)
        s = {"xs": xs}
        h1 = norm_mod(xs, sp["g_norm_mix"][layer][None], md, 0, f"norm_mix{layer}")
        s["h1"] = h1
        if layer % 2 == 0:
            e = layer // 2
            p = mm_cols(h1, w_in, 0, f"in_proj{layer}")
            of, sf = hgrn_fwd(p, lbs[0, e][None], False, f"scan_f{layer}")
            ob, sb = hgrn_fwd(p, lbs[1, e][None], True, f"scan_b{layer}")
            ya = readout_fwd(of, ob, p, sp["g_hgrn_out"][e][None], f"readout{layer}")
            yb = chunkmlp_fwd(p, sp["w_spatial"][e], sp["b_spatial"][e][:, :, None], sp["g_spatial_v"][e][None],
                              f"cmlp{layer}")
            ycat = jnp.concatenate([ya, yb], axis=1)
            w_out = wl.get("out", ycat)
            md = wl.tie(md)
            x1, f1 = mm_rows_res(ycat, w_out, 0, xs, md, 2, f"out_proj{layer}")
            s.update(p=p, of=of, ob=ob, sf=sf, sb=sb, ycat=ycat, f1=f1)
        else:
            o = layer // 2
            pp = mm_rows(h1, w_in, 0, f"pool_in{layer}")
            z = pool_op(pp, False, f"pool{layer}")
            x1, ypre = mm_grp_res(z, wl.get("grp", z), 0, sp["b_grp_pool"][o].reshape(1, D),
                                  sp["scale_pool"][o][None], xs, md, 2, f"pool_grp{layer}")
            s.update(z=z, ypre=ypre)
        h2 = norm_mod(x1, sp["g_norm_ffn"][layer][None], md, 3, f"norm_ffn{layer}")
        u = mm_cols(h2, wl.get("up", h2), 0, f"ffn_up{layer}", epi=_epi_relu2, out_dtype=BF16)
        x2, f2 = mm_rows_res(u, wl.get("down", u), 0, x1, md, 5, f"ffn_down{layer}")
        s.update(x1=x1, h2=h2, u=u, f2=f2, w=wl.have, md=md)
        saved.append(s)
        xs = x2

    dx, loss_lanes, dg_final = final_loss(xs, sp["g_norm_final"][None], target, "final_loss")

    token = jnp.zeros((8, 128), F32)
    dmods = [None] * 4
    sg = {"g_norm_final": dg_final[0], "g_norm_mix": [None] * 4, "g_norm_ffn": [None] * 4,
          "dlbs": [[None, None], [None, None]], "g_hgrn_out": [None] * 2, "w_spatial": [None] * 2,
          "b_spatial": [None] * 2, "g_spatial_v": [None] * 2, "b_grp_pool": [None] * 2, "scale_pool": [None] * 2}
    for layer in reversed(range(4)):
        s = saved[layer]
        wl, md = s["w"], s["md"] + token[0, 0]
        df2, dgt2 = gate_in(dx, s["f2"], md, 5, f"gate_ffn{layer}")
        da = mm_t_rows(df2, wl["down"], 0, f"ffn_down_t{layer}", epi=_epi_2sqrt, out_dtype=BF16,
                       extras_of=s["u"], per=2)
        g_down = grad_rows(s["u"], df2, f"g_ffn_down{layer}", per=2, tn=2048)
        dh2 = mm_t_cols(da, wl["up"], 0, f"ffn_up_t{layer}")
        g_up = grad_cols(s["h2"], da, f"g_ffn_up{layer}", per=1)
        dx1, dsh2, dsc2, dgf = normmod_bwd(dh2, s["x1"], dx, sp["g_norm_ffn"][layer][None], md, 3,
                                           f"norm_ffn_b{layer}")
        sg["g_norm_ffn"][layer] = dgf[0]
        md = md + grads_done(layer, "ffn", [g_up, g_down], dx1)[0, 0]
        if layer % 2 == 0:
            e = layer // 2
            df1, dgt1 = gate_in(dx1, s["f1"], md, 2, f"gate_mix{layer}")
            dycat = mm_t_rows(df1, wl["out"], 0, f"out_proj_t{layer}")
            g_b = grad_rows(s["ycat"], df1, f"g_out_proj{layer}")
            do, dpg, dng = readout_bwd(dycat, s["of"], s["ob"], s["p"], sp["g_hgrn_out"][e][None],
                                       f"readout_b{layer}")
            du, dv, dws, dbs, dgn = chunkmlp_bwd(dycat, s["p"], sp["w_spatial"][e], sp["b_spatial"][e][:, :, None],
                                                 sp["g_spatial_v"][e][None], f"cmlp_b{layer}")
            dq_f, di_f, dff, dlb_f = hgrn_bwd(s["p"], do, s["sf"], lbs[0, e][None], False, f"scan_f_b{layer}")
            dq, di, dfb, dlb_b = hgrn_bwd(s["p"], do, s["sb"], lbs[1, e][None], True, f"scan_b_b{layer}",
                                          add=(dq_f, di_f))
            dp = jnp.concatenate([dq, dff, dfb, di, dpg, du, dv], axis=1)
            dh1 = mm_t_cols(dp, wl["in"], 0, f"in_proj_t{layer}")
            g_a = grad_cols(s["h1"], dp, f"g_in_proj{layer}")
            sg["dlbs"][0][e], sg["dlbs"][1][e] = dlb_f[0], dlb_b[0]
            sg["g_hgrn_out"][e], sg["w_spatial"][e] = dng[0], dws
            sg["b_spatial"][e], sg["g_spatial_v"][e] = dbs[:, :, 0], dgn[0]
        else:
            o = layer // 2
            dyp, dgt1, dscale, dbias = gate_in_pool(dx1, s["ypre"], md, sp["scale_pool"][o][None], 2,
                                                    f"gate_mix{layer}")
            dz = mm_t_grp(dyp, wl["grp"], 0, f"pool_grp_t{layer}")
            g_b = grad_grp(s["z"], dyp, f"g_pool_grp{layer}")
            dpp = pool_op(dz, True, f"pool_t{layer}")
            dh1 = mm_t_rows(dpp, wl["pin"], 0, f"pool_in_t{layer}")
            g_a = grad_rows(s["h1"], dpp, f"g_pool_in{layer}")
            sg["b_grp_pool"][o], sg["scale_pool"][o] = dbias[0].reshape(4, 512), dscale[0]
        dx, dsh1, dsc1, dgm = normmod_bwd(dh1, s["xs"], dx1, sp["g_norm_mix"][layer][None], md, 0,
                                          f"norm_mix_b{layer}")
        sg["g_norm_mix"][layer] = dgm[0]
        dmods[layer] = jnp.concatenate([dsh1, dsc1, dgt1, dsh2, dsc2, dgt2], axis=1)
        token = grads_done(layer, "mix", [g_a, g_b], dx)
    return loss_lanes, dx, dmods, sg


BIG = ("w_in_even", "w_out_even", "w_in_pool", "w_grp_pool", "w_ffn_up", "w_ffn_down")
SMALL = ("b_ada", "g_norm_mix", "g_norm_ffn", "lb_logits", "g_hgrn_out", "w_spatial", "b_spatial", "g_spatial_v",
         "b_grp_pool", "scale_pool", "g_norm_final")
WEIGHTS = ("c_ctx", "w_ada", "b_ada", "g_norm_mix", "g_norm_ffn", "w_in_even", "w_out_even", "lb_logits",
           "g_hgrn_out", "w_spatial", "b_spatial", "g_spatial_v", "w_in_pool", "w_grp_pool", "b_grp_pool",
           "scale_pool", "w_ffn_up", "w_ffn_down", "g_norm_final")


def kernel(x, c, ctx, c_ctx, w_ada, b_ada, g_norm_mix, g_norm_ffn, w_in_even, w_out_even, lb_logits, g_hgrn_out, w_spatial, b_spatial, g_spatial_v, w_in_pool, w_grp_pool, b_grp_pool, scale_pool, w_ffn_up, w_ffn_down, g_norm_final, loss_target, m_c_ctx, m_w_ada, m_b_ada, m_g_norm_mix, m_g_norm_ffn, m_w_in_even, m_w_out_even, m_lb_logits, m_g_hgrn_out, m_w_spatial, m_b_spatial, m_g_spatial_v, m_w_in_pool, m_w_grp_pool, m_b_grp_pool, m_scale_pool, m_w_ffn_up, m_w_ffn_down, m_g_norm_final, v_c_ctx, v_w_ada, v_b_ada, v_g_norm_mix, v_g_norm_ffn, v_w_in_even, v_w_out_even, v_lb_logits, v_g_hgrn_out, v_w_spatial, v_b_spatial, v_g_spatial_v, v_w_in_pool, v_w_grp_pool, v_b_grp_pool, v_scale_pool, v_w_ffn_up, v_w_ffn_down, v_g_norm_final):
    loc = dict(c_ctx=c_ctx, w_ada=w_ada, b_ada=b_ada, g_norm_mix=g_norm_mix, g_norm_ffn=g_norm_ffn,
               w_in_even=w_in_even, w_out_even=w_out_even, lb_logits=lb_logits, g_hgrn_out=g_hgrn_out,
               w_spatial=w_spatial, b_spatial=b_spatial, g_spatial_v=g_spatial_v, w_in_pool=w_in_pool,
               w_grp_pool=w_grp_pool, b_grp_pool=b_grp_pool, scale_pool=scale_pool, w_ffn_up=w_ffn_up,
               w_ffn_down=w_ffn_down, g_norm_final=g_norm_final)
    mom = dict(c_ctx=m_c_ctx, w_ada=m_w_ada, b_ada=m_b_ada, g_norm_mix=m_g_norm_mix, g_norm_ffn=m_g_norm_ffn,
               w_in_even=m_w_in_even, w_out_even=m_w_out_even, lb_logits=m_lb_logits, g_hgrn_out=m_g_hgrn_out,
               w_spatial=m_w_spatial, b_spatial=m_b_spatial, g_spatial_v=m_g_spatial_v, w_in_pool=m_w_in_pool,
               w_grp_pool=m_w_grp_pool, b_grp_pool=m_b_grp_pool, scale_pool=m_scale_pool, w_ffn_up=m_w_ffn_up,
               w_ffn_down=m_w_ffn_down, g_norm_final=m_g_norm_final)
    var = dict(c_ctx=v_c_ctx, w_ada=v_w_ada, b_ada=v_b_ada, g_norm_mix=v_g_norm_mix, g_norm_ffn=v_g_norm_ffn,
               w_in_even=v_w_in_even, w_out_even=v_w_out_even, lb_logits=v_lb_logits, g_hgrn_out=v_g_hgrn_out,
               w_spatial=v_w_spatial, b_spatial=v_b_spatial, g_spatial_v=v_g_spatial_v, w_in_pool=v_w_in_pool,
               w_grp_pool=v_w_grp_pool, b_grp_pool=v_b_grp_pool, scale_pool=v_scale_pool, w_ffn_up=v_w_ffn_up,
               w_ffn_down=v_w_ffn_down, g_norm_final=v_g_norm_final)
    mx, my, mc = _mesh_pos()
    chip = 2 * mx + my
    dev = 2 * chip + mc

    def layer_tensors(layer):
        i = layer // 2
        pair = ("w_in_even", "w_out_even") if layer % 2 == 0 else ("w_in_pool", "w_grp_pool")
        return [(pair[0], i), (pair[1], i), ("w_ffn_up", layer), ("w_ffn_down", layer)]

    def layer_keys(layer):
        return ("in", "out", "up", "down") if layer % 2 == 0 else ("pin", "grp", "up", "down")

    def gather_parts(layer):
        return {"a": slice(0, 1), "b": slice(1, 4)} if layer == 0 else {"a": slice(0, 4)}

    def start_gather(layer, part, after):
        srcs = [loc[n][i][None].astype(BF16) for n, i in layer_tensors(layer)[gather_parts(layer)[part]]]
        lands = [lax.dynamic_update_slice(lax.empty((NSH,) + s.shape, BF16), s[None], (chip,) + (0,) * s.ndim)
                 for s in srcs]
        return ici_start(srcs, lands, after, False, f"gather_start{layer}{part}")

    hello = allgather8(_pack([c[0], lb_logits, b_grp_pool, scale_pool]), "gather_small")
    parts = [_unpack(hello[d], [(D,), (2, 2, 256), (2, 4, 128), (2, 512)]) for d in range(8)]
    cs = jnp.concatenate([jnp.stack([parts[d][0] for d in range(8)]), c_ctx[None], jnp.zeros((7, D), F32)])
    chips_of = [parts[2 * s] for s in range(NSH)]
    sp = dict(loc)
    sp["lb_logits"] = jnp.concatenate([q[1] for q in chips_of], axis=2)
    sp["b_grp_pool"] = jnp.concatenate([q[2] for q in chips_of], axis=2)
    sp["scale_pool"] = jnp.concatenate([q[3] for q in chips_of], axis=1)

    b_loc = lax.dynamic_slice_in_dim(b_ada, chip * 3072, 3072, axis=1)[:, None, :]
    mods_loc = ada_mods(cs, w_ada, b_loc, "ada_mods")
    mods_all = allgather8(mods_loc.reshape(-1, 128), "gather_mods").reshape(8, 4, 16, 3072)
    mods_full = jnp.concatenate([mods_all[2 * s] for s in range(NSH)], axis=2)
    mine = lax.dynamic_index_in_dim(mods_full, dev, axis=1, keepdims=False)
    mods = [jnp.stack([mods_full[l, 8].reshape(6, D), mine[l].reshape(6, D)]) for l in range(4)]

    first_a = start_gather(0, "a", mods_all)
    gathers = {(0, "a"): first_a, (0, "b"): start_gather(0, "b", first_a[4])}
    mods[0] = mods[0] + gathers[(0, "b")][4][0, 0]

    def weights_for(layer):
        def fetch(key, after):
            part = "b" if layer == 0 and key != "in" else "a"
            _, got = ici_wait(gathers[(layer, part)], after, False, f"gather_wait{layer}{part}")
            token = None
            if layer < 3 and part == list(gather_parts(layer))[-1]:
                gathers[(layer + 1, "a")] = start_gather(layer + 1, "a", got[0])
                token = gathers[(layer + 1, "a")][4]
            return dict(zip(layer_keys(layer)[gather_parts(layer)[part]], got)), token
        return LayerWeights(fetch)

    exchanges = {}

    def grads_done(layer, part, gs, after):
        lands = [lax.empty((8,) + g.shape[1:], BF16) for g in gs]
        exchanges[(layer, part)] = ici_start(gs, lands, after, True, f"exchange_start{layer}{part}")
        return exchanges[(layer, part)][4]

    xs = jnp.concatenate([ctx[0], x[0]], axis=0)
    loss_lanes, dxs, dmods, sg = device_step(xs, loss_target[0], mods, sp, weights_for, grads_done)
    grad_x = dxs[CTX:][None]

    out = {}

    def finish(layer, part, after):
        gs, rbs = ici_wait(exchanges[(layer, part)], after, True, f"exchange_wait{layer}{part}")
        names = layer_tensors(layer)[slice(2, 4) if part == "ffn" else slice(0, 2)]
        ps = [sum_blocks(g, rb, f"sum_{n}{i}") for (n, i), g, rb in zip(names, gs, rbs)]
        qs = swap_sibling(ps, f"swap{layer}{part}")
        for (n, i), p, q in zip(names, ps, qs):
            out[n] = adamw_big(p, q, loc[n], mom[n], var[n], i, out.get(n), f"adamw_{n}{i}")

    done = dmods[0] + exchanges[(0, "mix")][4][0, 0]
    dmods[0] = done
    for layer in (3, 2, 1):
        finish(layer, "ffn", done)
        finish(layer, "mix", done)
    finish(0, "ffn", done)

    dm_lat = jnp.stack([dmods[l][1].reshape(6 * D) for l in range(4)])
    dm_ctx = jnp.stack([dmods[l][0].reshape(6 * D) for l in range(4)])
    small_shapes = [(4, 6 * D), (4, 6 * D), (4, D), (4, D), (2, 2, AW), (2, AW), (2, NH, 128, 128), (2, NH, 128),
                    (2, AW), (2, 4, 512), (2, D), (D,), (128,)]
    mine_small = _pack([dm_lat, dm_ctx, jnp.stack(sg["g_norm_mix"]), jnp.stack(sg["g_norm_ffn"]),
                        jnp.stack([jnp.stack(sg["dlbs"][0]), jnp.stack(sg["dlbs"][1])]),
                        jnp.stack(sg["g_hgrn_out"]), jnp.stack(sg["w_spatial"]), jnp.stack(sg["b_spatial"]),
                        jnp.stack(sg["g_spatial_v"]), jnp.stack(sg["b_grp_pool"]), jnp.stack(sg["scale_pool"]),
                        sg["g_norm_final"], loss_lanes[0]])
    all_small = allgather8(mine_small, "gather_small_grads")
    tot = _unpack(sum8(all_small, "sum_small_grads"), small_shapes)
    (_, dm_ctx_tot, g_mix, g_ffn, dlbs, g_hg, g_ws, g_bs, g_gv, g_bg, g_sc, g_fin, loss_v) = tot
    loss = loss_v[0]
    dm_lat_all = jnp.stack([_unpack(all_small[d], small_shapes[:1])[0] for d in range(8)])
    g_b_ada = jnp.sum(dm_lat_all, axis=0) + dm_ctx_tot
    _, lb_vjp = jax.vjp(_lower_bounds, sp["lb_logits"])
    g_lb_full = lb_vjp(dlbs)[0]
    grads = {"b_ada": g_b_ada, "g_norm_mix": g_mix, "g_norm_ffn": g_ffn,
             "lb_logits": lax.dynamic_slice_in_dim(g_lb_full, chip * 256, 256, axis=2),
             "g_hgrn_out": g_hg, "w_spatial": g_ws, "b_spatial": g_bs, "g_spatial_v": g_gv,
             "b_grp_pool": lax.dynamic_slice_in_dim(g_bg, chip * 128, 128, axis=2),
             "scale_pool": lax.dynamic_slice_in_dim(g_sc, chip * 512, 512, axis=1), "g_norm_final": g_fin}

    dm_rows = jnp.concatenate([dm_lat_all.transpose(1, 0, 2), dm_ctx_tot[:, None, :], jnp.zeros((4, 7, 6 * D), F32)],
                              axis=1)
    dm_loc = lax.dynamic_slice_in_dim(dm_rows, chip * 3072, 3072, axis=2)
    g_wa, d_wa, nm_wa, nv_wa, dc_part = ada_update(cs, dm_loc, w_ada, m_w_ada, v_w_ada, "ada_update")
    out["w_ada"] = [g_wa, d_wa, nm_wa, nv_wa]
    dc_all = allgather8(dc_part[8].reshape(16, 128), "gather_dc")
    dpre = dc_all[0] + dc_all[2] + dc_all[4] + dc_all[6]
    sig = jax.nn.sigmoid(c_ctx)
    grads["c_ctx"] = dpre.reshape(D) * (sig * (1.0 + c_ctx * (1.0 - sig)))

    names = ("c_ctx",) + SMALL
    shapes = [loc[n].shape for n in names]
    d_s, nm_s, nv_s = adamw_small(_pack([grads[n] for n in names]), _pack([loc[n] for n in names]),
                                  _pack([mom[n] for n in names]), _pack([var[n] for n in names]), "adamw_small")
    for n, dl, nm, nv in zip(names, _unpack(d_s, shapes), _unpack(nm_s, shapes), _unpack(nv_s, shapes)):
        out[n] = [grads[n], dl, nm, nv]

    finish(0, "mix", d_s)
    for n in BIG:
        out[n] = [t.reshape(loc[n].shape) for t in out[n]]

    return (loss, grad_x, *[out[n][0] for n in WEIGHTS], *[out[n][1] for n in WEIGHTS],
            *[out[n][2] for n in WEIGHTS], *[out[n][3] for n in WEIGHTS])
```

```python
import functools

import jax
import jax.numpy as jnp
from jax import lax
from jax.experimental import pallas as pl
from jax.experimental.pallas import tpu as pltpu

F32 = jnp.float32
BF16 = jnp.bfloat16
MESH = pl.DeviceIdType.MESH
ANY = pl.BlockSpec(memory_space=pl.ANY)
HIGHEST = lax.Precision.HIGHEST

D = 2048
DFF = 8192
CTX = 256
TR = 256
GRID_W = 64
EPS = 1e-6
LOG_FLOOR = 1e-30
NH = 8
DK = 128
SUB_FWD = 16
SUB_BWD = 32
B_CHUNK = 128
AW = NH * DK
POOL_WINDOWS = (2, 4, 8, 16)
NSH = 4
ADAM_LR, ADAM_B1, ADAM_B2, ADAM_EPS, ADAM_WD, ADAM_STEP = 0.001, 0.9, 0.999, 1e-08, 0.01, 10
MIB = 1024 * 1024


def _cp(vmem_mib):
    return pltpu.CompilerParams(vmem_limit_bytes=vmem_mib * MIB)


def _bf(v):
    return v.astype(BF16)


def _nn(a, b):
    return lax.dot_general(a, b, (((1,), (0,)), ((), ())), preferred_element_type=F32)


def _nt(a, b):
    return lax.dot_general(a, b, (((1,), (1,)), ((), ())), preferred_element_type=F32)


def _tn(a, b):
    return lax.dot_general(a, b, (((0,), (0,)), ((), ())), preferred_element_type=F32)


def _mesh_pos():
    return lax.axis_index("x"), lax.axis_index("y"), lax.axis_index("c")


HBM = pl.BlockSpec(memory_space=pltpu.HBM)
SEM = pl.BlockSpec(memory_space=pltpu.SEMAPHORE)
EFFECT = pltpu.SideEffectType.DATAFLOW_SIDE_EFFECTING


def _peer_copies(exchange, srcs, lands, send_sems, recv_sems):
    x, y, c = _mesh_pos()
    me = 2 * x + y
    pairs = []
    for t in range(len(srcs)):
        for j, (px, py) in enumerate([(1 - x, y), (x, 1 - y), (1 - x, 1 - y)]):
            peer = 2 * px + py
            src = srcs[t].at[peer] if exchange else srcs[t]
            out_slot, in_slot = (2 * me + c, 2 * peer + c) if exchange else (me, peer)

            def mk(slot, t=t, j=j, px=px, py=py, src=src):
                return pltpu.make_async_remote_copy(
                    src_ref=src, dst_ref=lands[t].at[slot], send_sem=send_sems.at[3 * t + j],
                    recv_sem=recv_sems.at[3 * t + j], device_id=(px, py, c), device_id_type=MESH)

            pairs.append((mk(out_slot), mk(in_slot)))
    return pairs


def ici_start(srcs, lands, after, exchange, name):
    n = len(srcs)

    def body(*refs):
        send_sems, recv_sems = refs[2 * n + 1], refs[2 * n + 2]
        for out_copy, _ in _peer_copies(exchange, refs[:n], refs[n:2 * n], send_sems, recv_sems):
            out_copy.start()
        refs[-1][...] = jnp.zeros_like(refs[-1])

    arrs = list(srcs) + list(lands)
    res = pl.pallas_call(
        body, name=name,
        out_shape=(pltpu.SemaphoreType.DMA((3 * n,)), pltpu.SemaphoreType.DMA((3 * n,)),
                   *[pltpu.HBM(a.shape, a.dtype) for a in arrs], jax.ShapeDtypeStruct((8, 128), F32)),
        in_specs=[HBM] * (2 * n) + [ANY],
        out_specs=(SEM, SEM, *[HBM] * (2 * n), pl.BlockSpec(memory_space=pltpu.VMEM)),
        input_output_aliases={t: 2 + t for t in range(2 * n)},
        compiler_params=pltpu.CompilerParams(has_side_effects=EFFECT),
    )(*[pltpu.with_memory_space_constraint(a, pltpu.HBM) for a in arrs], after)
    return res[0], res[1], list(res[2:2 + n]), list(res[2 + n:2 + 2 * n]), res[-1]


def ici_wait(started, after, exchange, name):
    send_sems, recv_sems, srcs, lands, _ = started
    n = len(srcs)

    def body(*refs):
        for out_copy, in_copy in _peer_copies(exchange, refs[:n], refs[n:2 * n], refs[2 * n], refs[2 * n + 1]):
            out_copy.wait_send()
            in_copy.wait_recv()

    arrs = list(srcs) + list(lands)
    res = pl.pallas_call(
        body, name=name,
        out_shape=tuple(pltpu.HBM(a.shape, a.dtype) for a in arrs),
        in_specs=[HBM] * (2 * n) + [SEM, SEM, ANY], out_specs=tuple([HBM] * (2 * n)),
        input_output_aliases={t: t for t in range(2 * n)},
        compiler_params=pltpu.CompilerParams(has_side_effects=EFFECT),
    )(*arrs, send_sems, recv_sems, after)
    return list(res[:n]), list(res[n:])


def sum_blocks(g, rb, name):
    cols = g.shape[-1]
    rows = g.size // (NSH * cols)
    tr = min(rows, 256)

    def body(g_ref, r1_ref, r2_ref, r3_ref, o_ref):
        acc = g_ref[...].astype(F32) + r1_ref[...].astype(F32) + r2_ref[...].astype(F32) + r3_ref[...].astype(F32)
        o_ref[...] = acc.astype(BF16)

    def mine(i):
        x, y, _ = _mesh_pos()
        return (2 * x + y, i, 0)

    def peer(fx, fy):
        def index(i):
            x, y, c = _mesh_pos()
            return (2 * (2 * (x ^ fx) + (y ^ fy)) + c, i, 0)
        return pl.BlockSpec((None, tr, cols), index)

    return pl.pallas_call(
        body, name=name, grid=(rows // tr,),
        in_specs=[pl.BlockSpec((None, tr, cols), mine), peer(1, 0), peer(0, 1), peer(1, 1)],
        out_specs=pl.BlockSpec((tr, cols), lambda i: (i, 0)),
        out_shape=jax.ShapeDtypeStruct((rows, cols), BF16), compiler_params=_cp(32),
    )(g.reshape(NSH, rows, cols), *[rb.reshape(8, rows, cols)] * 3)


def swap_sibling(ps, name):
    n = len(ps)

    def body(*refs):
        ins, outs, send_sems, recv_sems = refs[:n], refs[n:2 * n], refs[2 * n], refs[2 * n + 1]
        x, y, c = _mesh_pos()
        copies = [pltpu.make_async_remote_copy(
            src_ref=ins[t], dst_ref=outs[t], send_sem=send_sems.at[t], recv_sem=recv_sems.at[t],
            device_id=(x, y, 1 - c), device_id_type=MESH) for t in range(n)]
        for cp in copies:
            cp.start()
        for cp in copies:
            cp.wait_recv()
            cp.wait_send()

    return pl.pallas_call(
        body, name=name, out_shape=[jax.ShapeDtypeStruct(p.shape, p.dtype) for p in ps],
        in_specs=[ANY] * n, out_specs=[ANY] * n,
        scratch_shapes=[pltpu.SemaphoreType.DMA((n,)), pltpu.SemaphoreType.DMA((n,))],
    )(*ps)


def allgather8(v, name):
    m, n = v.shape

    def body(x_ref, out_ref, send_sems, recv_sems, local_sem):
        x, y, c = _mesh_pos()
        me, sibling = (x, y, c), (x, y, 1 - c)
        chips = [(1 - x, y), (x, 1 - y), (1 - x, 1 - y)]

        def rows(px, py, pc):
            return out_ref.at[4 * px + 2 * py + pc]

        def copy(k, block, to, src=None):
            return pltpu.make_async_remote_copy(
                src_ref=rows(*block) if src is None else src, dst_ref=rows(*block),
                send_sem=send_sems.at[k], recv_sem=recv_sems.at[k], device_id=to, device_id_type=MESH)

        mine = pltpu.make_async_copy(x_ref, rows(*me), local_sem)
        mine.start()
        first = [copy(0, me, sibling, src=x_ref)]
        first += [copy(1 + j, me, (*chip, c), src=x_ref) for j, chip in enumerate(chips)]
        for cp in first:
            cp.start()
        passed = [copy(4 + j, (*chip, c), sibling) for j, chip in enumerate(chips)]
        for j, chip in enumerate(chips):
            copy(1 + j, (*chip, c), me).wait_recv()
            passed[j].start()
        copy(0, sibling, me).wait_recv()
        for j, chip in enumerate(chips):
            copy(4 + j, (*chip, 1 - c), me).wait_recv()
        for cp in first + passed:
            cp.wait_send()
        mine.wait()

    return pl.pallas_call(
        body, name=name,
        out_shape=jax.ShapeDtypeStruct((8, m, n), v.dtype),
        in_specs=[pl.BlockSpec(memory_space=pltpu.VMEM)],
        out_specs=pl.BlockSpec(memory_space=pltpu.VMEM),
        scratch_shapes=[pltpu.SemaphoreType.DMA((7,)), pltpu.SemaphoreType.DMA((7,)), pltpu.SemaphoreType.DMA],
        compiler_params=_cp(40),
    )(v)


def _row_spec(width=D, off=0):
    return pl.BlockSpec((TR, width), lambda i, off=off: (i, off))


def _vec_spec(width=D):
    return pl.BlockSpec((1, width), lambda i: (0, 0))


def _mod_spec():
    return pl.BlockSpec((None, 6, D), lambda i: (jnp.minimum(i, 1), 0, 0))


def _pair_spec(width=D):
    return pl.BlockSpec((None, 1, width), lambda i: (jnp.minimum(i, 1), 0, 0))


def _accum(ref, val, first):
    @pl.when(first)
    def _():
        ref[...] = val

    @pl.when(jnp.logical_not(first))
    def _():
        ref[...] += val


def norm_mod(xs, g, mod, si, name):
    r = xs.shape[0]

    def body(x_ref, g_ref, m_ref, o_ref):
        x = x_ref[...]
        rstd = lax.rsqrt(jnp.mean(x * x, axis=-1, keepdims=True) + EPS)
        n = x * rstd * g_ref[...]
        o_ref[...] = (n * (1.0 + m_ref[si + 1:si + 2, :]) + m_ref[si:si + 1, :]).astype(BF16)

    return pl.pallas_call(
        body, name=name, grid=(r // TR,),
        in_specs=[_row_spec(), _vec_spec(), _mod_spec()], out_specs=_row_spec(),
        out_shape=jax.ShapeDtypeStruct((r, D), BF16), compiler_params=_cp(32),
    )(xs, g, mod)


def gate_in(dx, f, mod, gi, name):
    r = dx.shape[0]

    def body(dx_ref, f_ref, m_ref, o_ref, dg_ref):
        i = pl.program_id(0)
        dxv = dx_ref[...]
        o_ref[...] = (dxv * m_ref[gi:gi + 1, :]).astype(BF16)
        _accum(dg_ref, jnp.sum(dxv * f_ref[...].astype(F32), axis=0, keepdims=True), i <= 1)

    return pl.pallas_call(
        body, name=name, grid=(r // TR,),
        in_specs=[_row_spec(), _row_spec(), _mod_spec()], out_specs=[_row_spec(), _pair_spec()],
        out_shape=[jax.ShapeDtypeStruct((r, D), BF16), jax.ShapeDtypeStruct((2, 1, D), F32)],
        compiler_params=_cp(32),
    )(dx, f, mod)


def gate_in_pool(dx, ypre, mod, scale, gi, name):
    r = dx.shape[0]

    def body(dx_ref, y_ref, m_ref, s_ref, o_ref, dg_ref, ds_ref, db_ref):
        i = pl.program_id(0)
        dxv = dx_ref[...]
        yp = y_ref[...].astype(F32)
        sc = s_ref[...]
        dy = dxv * m_ref[gi:gi + 1, :]
        dyp = dy * sc
        o_ref[...] = dyp.astype(BF16)
        _accum(dg_ref, jnp.sum(dxv * (yp * sc), axis=0, keepdims=True), i <= 1)
        _accum(ds_ref, jnp.sum(dy * yp, axis=0, keepdims=True), i == 0)
        _accum(db_ref, jnp.sum(dyp, axis=0, keepdims=True), i == 0)

    return pl.pallas_call(
        body, name=name, grid=(r // TR,),
        in_specs=[_row_spec(), _row_spec(), _mod_spec(), _vec_spec()],
        out_specs=[_row_spec(), _pair_spec(), _vec_spec(), _vec_spec()],
        out_shape=[jax.ShapeDtypeStruct((r, D), BF16), jax.ShapeDtypeStruct((2, 1, D), F32),
                   jax.ShapeDtypeStruct((1, D), F32), jax.ShapeDtypeStruct((1, D), F32)],
        compiler_params=_cp(32),
    )(dx, ypre, mod, scale)


def normmod_bwd(dh, x, dxo, g, mod, si, name):
    r = x.shape[0]

    def body(dh_ref, x_ref, dxo_ref, g_ref, m_ref, dx_ref, dsh_ref, dsc_ref, dg_ref):
        i = pl.program_id(0)
        xv = x_ref[...]
        dhv = dh_ref[...]
        gv = g_ref[...]
        rstd = lax.rsqrt(jnp.mean(xv * xv, axis=-1, keepdims=True) + EPS)
        xhat = xv * rstd
        dn = dhv * (1.0 + m_ref[si + 1:si + 2, :])
        dxh = dn * gv
        dx_ref[...] = rstd * (dxh - xhat * jnp.mean(dxh * xhat, axis=-1, keepdims=True)) + dxo_ref[...]
        _accum(dsh_ref, jnp.sum(dhv, axis=0, keepdims=True), i <= 1)
        _accum(dsc_ref, jnp.sum(dhv * (xhat * gv), axis=0, keepdims=True), i <= 1)
        _accum(dg_ref, jnp.sum(dn * xhat, axis=0, keepdims=True), i == 0)

    return pl.pallas_call(
        body, name=name, grid=(r // TR,),
        in_specs=[_row_spec(), _row_spec(), _row_spec(), _vec_spec(), _mod_spec()],
        out_specs=[_row_spec(), _pair_spec(), _pair_spec(), _vec_spec()],
        out_shape=[jax.ShapeDtypeStruct((r, D), F32), jax.ShapeDtypeStruct((2, 1, D), F32),
                   jax.ShapeDtypeStruct((2, 1, D), F32), jax.ShapeDtypeStruct((1, D), F32)],
        compiler_params=_cp(48),
    )(dh, x, dxo, g, mod)


def final_loss(xs, g, target, name):
    r = xs.shape[0]

    def body(x_ref, g_ref, t_ref, dx_ref, loss_ref, dg_ref):
        i = pl.program_id(0)

        @pl.when(i == 0)
        def _():
            dx_ref[...] = jnp.zeros_like(dx_ref)
            loss_ref[...] = jnp.zeros_like(loss_ref)
            dg_ref[...] = jnp.zeros_like(dg_ref)

        @pl.when(i > 0)
        def _():
            xv = x_ref[...]
            gv = g_ref[...]
            rstd = lax.rsqrt(jnp.mean(xv * xv, axis=-1, keepdims=True) + EPS)
            xhat = xv * rstd
            err = xhat * gv - t_ref[...]
            part = 0.5 * jnp.sum(jnp.mean(err * err, axis=-1, keepdims=True), axis=0, keepdims=True)
            lane = lax.broadcasted_iota(jnp.int32, (1, 128), 1)
            loss_ref[...] += jnp.where(lane == 0, part, 0.0)
            dy = err * (1.0 / D)
            dg_ref[...] += jnp.sum(dy * xhat, axis=0, keepdims=True)
            dxh = dy * gv
            dx_ref[...] = rstd * (dxh - xhat * jnp.mean(dxh * xhat, axis=-1, keepdims=True))

    return pl.pallas_call(
        body, name=name, grid=(r // TR,),
        in_specs=[_row_spec(), _vec_spec(), pl.BlockSpec((TR, D), lambda i: (jnp.maximum(i - 1, 0), 0))],
        out_specs=[_row_spec(), pl.BlockSpec((1, 128), lambda i: (0, 0)), _vec_spec()],
        out_shape=[jax.ShapeDtypeStruct((r, D), F32), jax.ShapeDtypeStruct((1, 128), F32),
                   jax.ShapeDtypeStruct((1, D), F32)],
        compiler_params=_cp(32),
    )(xs, g, target)


def _mm(name, mode, a, b, grid, a_spec, b_spec, out_shapes, out_specs, epi, kaxis=None, acc=None,
        extras=(), merge_b=False, vmem=48):
    ne, no = len(extras), len(out_shapes)
    dot2 = {"nn": _nn, "nt": _nt, "tn": _tn}[mode]
    nk = grid[kaxis] if kaxis is not None else 1

    def dot(av, bv):
        return dot2(av, bv.reshape(-1, bv.shape[-1]) if merge_b else bv)

    def body(*refs):
        a_ref, b_ref = refs[0], refs[1]
        ex = refs[2:2 + ne]
        outs = refs[2 + ne:2 + ne + no]
        ids = [pl.program_id(ax) for ax in range(len(grid))]
        if kaxis is None:
            epi(dot(a_ref[...], b_ref[...]), ids, ex, outs)
        else:
            acc_ref = refs[-1]
            k = ids[kaxis]

            @pl.when(k == 0)
            def _():
                acc_ref[...] = jnp.zeros_like(acc_ref)

            acc_ref[...] += dot(a_ref[...], b_ref[...])

            @pl.when(k == nk - 1)
            def _():
                epi(acc_ref[...], ids, ex, outs)

    operands = [a, b] + [e[0] for e in extras]
    in_specs = [a_spec, b_spec] + [e[1] for e in extras]
    return pl.pallas_call(
        body, name=name, grid=grid, in_specs=in_specs, out_specs=out_specs, out_shape=out_shapes,
        scratch_shapes=[] if kaxis is None else [pltpu.VMEM(acc, F32)], compiler_params=_cp(vmem),
    )(*operands)


def _epi_store(acc, ids, ex, outs):
    outs[0][...] = acc.astype(outs[0].dtype)


def _epi_relu2(acc, ids, ex, outs):
    rl = jnp.maximum(acc, 0.0)
    outs[0][...] = (rl * rl).astype(BF16)


def _epi_2sqrt(acc, ids, ex, outs):
    outs[0][...] = (acc * (2.0 * jnp.sqrt(ex[0][...].astype(F32)))).astype(BF16)


def _gate_rows(ids, tm, shape, mod_ref, gi):
    rid = ids[0] * tm + lax.broadcasted_iota(jnp.int32, shape, 0)
    return jnp.where(rid < CTX, mod_ref[0, gi:gi + 1, :], mod_ref[1, gi:gi + 1, :])


def _epi_res(gi, tm):
    def epi(acc, ids, ex, outs):
        outs[0][...] = ex[0][...] + _gate_rows(ids, tm, acc.shape, ex[1], gi) * acc
        outs[1][...] = acc.astype(BF16)
    return epi


def _epi_pool(gi, tm):
    def epi(acc, ids, ex, outs):
        ypre = acc + ex[2][...]
        outs[0][...] = ex[0][...] + _gate_rows(ids, tm, acc.shape, ex[1], gi) * (ypre * ex[3][...])
        outs[1][...] = ypre.astype(BF16)
    return epi


def _tm(r):
    return 768 if r % 768 == 0 else TR


def _tm_wide(r):
    return 1408 if r % 1408 == 0 else _tm(r)


def mm_cols(a, w, l, name, epi=_epi_store, out_dtype=F32, per=2):
    r, k = a.shape
    c = w.shape[3]
    tm, tn = _tm_wide(r), c // per
    return _mm(name, "nn", a, w, (r // tm, NSH * per),
               pl.BlockSpec((tm, k), lambda i, j: (i, 0)),
               pl.BlockSpec((None, None, k, tn), lambda i, j: (j // per, l, 0, j % per)),
               [jax.ShapeDtypeStruct((r, NSH * c), out_dtype)], [pl.BlockSpec((tm, tn), lambda i, j: (i, j))],
               epi, vmem=56)[0]


def mm_rows_res(a, w, l, res, mod, gi, name):
    r = a.shape[0]
    kc = w.shape[2]
    tm, tn = _tm(r), 1024
    return _mm(name, "nn", a, w, (r // tm, D // tn, NSH),
               pl.BlockSpec((tm, kc), lambda i, j, k: (i, k)),
               pl.BlockSpec((None, None, kc, tn), lambda i, j, k: (k, l, 0, j)),
               [jax.ShapeDtypeStruct((r, D), F32), jax.ShapeDtypeStruct((r, D), BF16)],
               [pl.BlockSpec((tm, tn), lambda i, j, k: (i, j))] * 2,
               _epi_res(gi, tm), kaxis=2, acc=(tm, tn),
               extras=[(res, pl.BlockSpec((tm, tn), lambda i, j, k: (i, j))),
                       (mod, pl.BlockSpec((2, 6, tn), lambda i, j, k: (0, 0, j)))], vmem=56)


def mm_rows(a, w, l, name):
    r = a.shape[0]
    kc = w.shape[2]
    tm, tn = _tm(r), 1024
    return _mm(name, "nn", a, w, (r // tm, D // tn, NSH),
               pl.BlockSpec((tm, kc), lambda i, j, k: (i, k)),
               pl.BlockSpec((None, None, kc, tn), lambda i, j, k: (k, l, 0, j)),
               [jax.ShapeDtypeStruct((r, D), F32)], [pl.BlockSpec((tm, tn), lambda i, j, k: (i, j))],
               _epi_store, kaxis=2, acc=(tm, tn), vmem=56)[0]


def mm_grp_res(z, w, o, bias, scale, res, mod, gi, name):
    r = z.shape[0]
    tm = _tm(r)
    return _mm(name, "nn", z, w, (r // tm, 4),
               pl.BlockSpec((tm, 512), lambda i, g: (i, g)), _grp_spec(o),
               [jax.ShapeDtypeStruct((r, D), F32), jax.ShapeDtypeStruct((r, D), BF16)],
               [pl.BlockSpec((tm, 512), lambda i, g: (i, g))] * 2,
               _epi_pool(gi, tm), merge_b=True,
               extras=[(res, pl.BlockSpec((tm, 512), lambda i, g: (i, g))),
                       (mod, pl.BlockSpec((2, 6, 512), lambda i, g: (0, 0, g))),
                       (bias, pl.BlockSpec((1, 512), lambda i, g: (0, g))),
                       (scale, pl.BlockSpec((1, 512), lambda i, g: (0, g)))], vmem=48)


def _grp_spec(o):
    return pl.BlockSpec((NSH, None, None, 128, 512), lambda i, g: (0, o, g, 0, 0))


def mm_t_rows(a, w, l, name, epi=_epi_store, out_dtype=F32, extras_of=None, per=1):
    r, n = a.shape
    kc = w.shape[2]
    tm, tn = _tm_wide(r), kc // per
    extras = []
    if extras_of is not None:
        extras = [(extras_of, pl.BlockSpec((tm, tn), lambda i, j: (i, j)))]
    return _mm(name, "nt", a, w, (r // tm, NSH * per),
               pl.BlockSpec((tm, n), lambda i, j: (i, 0)),
               pl.BlockSpec((None, None, tn, n), lambda i, j: (j // per, l, j % per, 0)),
               [jax.ShapeDtypeStruct((r, NSH * kc), out_dtype)], [pl.BlockSpec((tm, tn), lambda i, j: (i, j))],
               epi, extras=extras, vmem=56)[0]


def mm_t_cols(a, w, l, name):
    r = a.shape[0]
    k, c = w.shape[2], w.shape[3]
    tm, tn = _tm_wide(r), 1024
    return _mm(name, "nt", a, w, (r // tm, k // tn, NSH),
               pl.BlockSpec((tm, c), lambda i, j, s: (i, s)),
               pl.BlockSpec((None, None, tn, c), lambda i, j, s: (s, l, j, 0)),
               [jax.ShapeDtypeStruct((r, k), F32)], [pl.BlockSpec((tm, tn), lambda i, j, s: (i, j))],
               _epi_store, kaxis=2, acc=(tm, tn), vmem=56)[0]


def mm_t_grp(dy, w, o, name):
    r = dy.shape[0]
    tm = _tm(r)
    return _mm(name, "nt", dy, w, (r // tm, 4),
               pl.BlockSpec((tm, 512), lambda i, g: (i, g)), _grp_spec(o),
               [jax.ShapeDtypeStruct((r, D), F32)], [pl.BlockSpec((tm, 512), lambda i, g: (i, g))],
               _epi_store, merge_b=True, vmem=40)[0]


def grad_cols(a, b, name, ta=1024, per=2):
    r, k = a.shape
    c = b.shape[1] // NSH
    tk, tn = _tm(r), c // per
    return _mm(name, "tn", a, b, (NSH, k // ta, per, r // tk),
               pl.BlockSpec((tk, ta), lambda s, i, j, t: (t, i)),
               pl.BlockSpec((tk, tn), lambda s, i, j, t: (t, s * per + j)),
               [jax.ShapeDtypeStruct((NSH, 1, k, c), BF16)],
               [pl.BlockSpec((None, None, ta, tn), lambda s, i, j, t: (s, 0, i, j))],
               _epi_store, kaxis=3, acc=(ta, tn), vmem=56)[0]


def grad_rows(a, b, name, per=1, tn=1024):
    r = a.shape[0]
    kc, n = a.shape[1] // NSH, b.shape[1]
    tk, ta = _tm(r), kc // per
    return _mm(name, "tn", a, b, (NSH, per, n // tn, r // tk),
               pl.BlockSpec((tk, ta), lambda s, i, j, t: (t, s * per + i)),
               pl.BlockSpec((tk, tn), lambda s, i, j, t: (t, j)),
               [jax.ShapeDtypeStruct((NSH, 1, kc, n), BF16)],
               [pl.BlockSpec((None, None, ta, tn), lambda s, i, j, t: (s, 0, i, j))],
               _epi_store, kaxis=3, acc=(ta, tn), vmem=56)[0]


def grad_grp(z, dy, name):
    r = z.shape[0]
    tk = _tm(r)

    def epi(acc, ids, ex, outs):
        for s in range(NSH):
            outs[0][s] = acc[s * 128:(s + 1) * 128].astype(BF16)

    return _mm(name, "tn", z, dy, (4, r // tk),
               pl.BlockSpec((tk, 512), lambda g, t: (t, g)), pl.BlockSpec((tk, 512), lambda g, t: (t, g)),
               [jax.ShapeDtypeStruct((NSH, 1, 4, 128, 512), BF16)],
               [pl.BlockSpec((NSH, None, None, 128, 512), lambda g, t: (0, 0, g, 0, 0))],
               epi, kaxis=1, acc=(512, 512), vmem=40)[0]


def _scan_tile(reverse, nt):
    if reverse:
        return lambda p: jnp.where(p == 0, 0, nt - p)
    return lambda p: p


def _gates(f, lbv):
    sg = jax.nn.sigmoid(f)
    fg = lbv + (1.0 - lbv) * sg
    g = jnp.log(jnp.maximum(fg, LOG_FLOOR))
    kk = (1.0 - lbv) * jax.nn.sigmoid(-f)
    return sg, fg, g, kk


def _chunk_cumsum(g, reverse, sub):
    n = g.shape[0]
    rr = lax.broadcasted_iota(jnp.int32, (n, n), 0)
    cc = lax.broadcasted_iota(jnp.int32, (n, n), 1)
    inside = (rr // sub) == (cc // sub)
    tri = jnp.where(inside & ((cc >= rr) if reverse else (cc <= rr)), 1.0, 0.0).astype(F32)
    return jnp.dot(tri, g, precision=HIGHEST, preferred_element_type=F32)


def _decay(b, s, rows, reverse):
    return jnp.where((rows <= s) if reverse else (rows >= s), jnp.exp(b - b[s:s + 1]), 0.0)


def hgrn_fwd(p, lb, reverse, name):
    SUB = SUB_FWD
    r = p.shape[0]
    nt = r // TR
    nsub = TR // SUB
    fcol = 2 if reverse else 1
    tile = _scan_tile(reverse, nt)

    def body(q_ref, f_ref, v_ref, lb_ref, o_ref, sin_ref, st, k_s, b_s):
        i = pl.program_id(0)

        @pl.when(i == 0)
        def _():
            st[...] = jnp.zeros_like(st)

        sin_ref[...] = st[...]
        _, _, g, kk = _gates(f_ref[...], lb_ref[...])
        k_s[...] = kk
        b_s[...] = _chunk_cumsum(g, reverse, SUB)
        rows = lax.broadcasted_iota(jnp.int32, (SUB, DK), 0)

        def sub(jj, carry):
            j = (nsub - 1 - jj) if reverse else jj
            rs = pl.ds(pl.multiple_of(j * SUB, SUB), SUB)
            for h in range(NH):
                sl = slice(h * DK, (h + 1) * DK)
                q, k, b, v = q_ref[rs, sl], k_s[rs, sl], b_s[rs, sl], v_ref[rs, sl]
                btot = b[0:1] if reverse else b[SUB - 1:SUB]
                o = _nt(_bf(q * jnp.exp(b)), _bf(st[h]))
                for s in range(SUB):
                    col = jnp.sum(q * k[s:s + 1] * _decay(b, s, rows, reverse), axis=-1, keepdims=True)
                    o = o + col * v[s:s + 1]
                o_ref[rs, sl] = o
                st[h] = st[h] * jnp.exp(btot) + _tn(_bf(v), _bf(k * jnp.exp(btot - b)))
            return carry

        lax.fori_loop(0, nsub, sub, 0)

    seg = lambda col: pl.BlockSpec((TR, AW), lambda i, col=col: (tile(i), col))
    return pl.pallas_call(
        body, name=name, grid=(nt,),
        in_specs=[seg(0), seg(fcol), seg(3), _vec_spec(AW)],
        out_specs=[pl.BlockSpec((TR, AW), lambda i: (tile(i), 0)),
                   pl.BlockSpec((None, NH, DK, DK), lambda i: (tile(i), 0, 0, 0))],
        out_shape=[jax.ShapeDtypeStruct((r, AW), F32), jax.ShapeDtypeStruct((nt, NH, DK, DK), F32)],
        scratch_shapes=[pltpu.VMEM((NH, DK, DK), F32), pltpu.VMEM((TR, AW), F32), pltpu.VMEM((TR, AW), F32)],
        compiler_params=_cp(40),
    )(p, p, p, lb)


def hgrn_bwd(p, do, sin, lb, reverse, name, add=None):
    SUB = SUB_BWD
    r = p.shape[0]
    nt = r // TR
    nsub = TR // SUB
    fcol = 2 if reverse else 1
    tile0 = _scan_tile(reverse, nt)
    tile = lambda i: tile0(nt - 1 - i)
    nadd = 0 if add is None else 2
    out_dt = F32 if add is None else BF16

    def body(*refs):
        q_ref, f_ref, v_ref, do_ref, sin_ref, lb_ref = refs[:6]
        adds = refs[6:6 + nadd]
        dq_ref, dv_ref, df_ref, dlb_ref = refs[6 + nadd:10 + nadd]
        dst, srun, ssub, k_s, b_s, sg_s, fg_s = refs[10 + nadd:]
        i = pl.program_id(0)

        @pl.when(i == 0)
        def _():
            dst[...] = jnp.zeros_like(dst)
            dlb_ref[...] = jnp.zeros_like(dlb_ref)

        lbv = lb_ref[...]
        sg, fg, g, kk = _gates(f_ref[...], lbv)
        k_s[...] = kk
        sg_s[...] = sg
        fg_s[...] = fg
        b_s[...] = _chunk_cumsum(g, reverse, SUB)
        srun[...] = sin_ref[...]
        rows = lax.broadcasted_iota(jnp.int32, (SUB, DK), 0)
        r16 = lax.broadcasted_iota(jnp.int32, (SUB, SUB), 0)
        c16 = lax.broadcasted_iota(jnp.int32, (SUB, SUB), 1)
        later = jnp.where((c16 <= r16) if reverse else (c16 >= r16), 1.0, 0.0).astype(F32)

        def recompute(jj, c):
            j = (nsub - 1 - jj) if reverse else jj
            rs = pl.ds(pl.multiple_of(j * SUB, SUB), SUB)
            for h in range(NH):
                sl = slice(h * DK, (h + 1) * DK)
                k, b, v = k_s[rs, sl], b_s[rs, sl], v_ref[rs, sl]
                btot = b[0:1] if reverse else b[SUB - 1:SUB]
                ssub[jj, h] = srun[h]
                srun[h] = srun[h] * jnp.exp(btot) + _tn(_bf(v), _bf(k * jnp.exp(btot - b)))
            return c

        lax.fori_loop(0, nsub, recompute, 0)
        for h in range(NH):
            ssub[nsub, h] = srun[h]

        def back(jj, c):
            pos = nsub - 1 - jj
            j = jj if reverse else pos
            rs = pl.ds(pl.multiple_of(j * SUB, SUB), SUB)
            for h in range(NH):
                sl = slice(h * DK, (h + 1) * DK)
                q, k, b, v, dov = q_ref[rs, sl], k_s[rs, sl], b_s[rs, sl], v_ref[rs, sl], do_ref[rs, sl]
                btot = b[0:1] if reverse else b[SUB - 1:SUB]
                s0 = ssub[pos, h]
                ds = dst[h]
                dg_next = jnp.sum(ds * ssub[pos + 1, h], axis=0, keepdims=True)
                eb = jnp.exp(b)
                ebt = jnp.exp(btot - b)
                ke = k * ebt
                dq = _nn(_bf(dov), _bf(s0)) * eb
                dk = _nn(_bf(v), _bf(ds)) * ebt
                dv = _nt(_bf(ke), _bf(ds))
                for s in range(SUB):
                    dec = _decay(b, s, rows, reverse)
                    dsc = jnp.sum(dov * v[s:s + 1], axis=-1, keepdims=True)
                    qd = q * dec
                    dq = dq + (dsc * dec) * k[s:s + 1]
                    dk_row = jnp.sum(dsc * qd, axis=0, keepdims=True)
                    sc = jnp.sum(qd * k[s:s + 1], axis=-1, keepdims=True)
                    dv_row = jnp.sum(sc * dov, axis=0, keepdims=True)
                    dk = dk + jnp.where(rows == s, dk_row, 0.0)
                    dv = dv + jnp.where(rows == s, dv_row, 0.0)
                dst[h] = ds * jnp.exp(btot) + _tn(_bf(dov), _bf(q * eb))
                dg = jnp.dot(later, q * dq - k * dk, precision=HIGHEST, preferred_element_type=F32) + dg_next
                sgv, fgv, lbh = sg_s[rs, sl], fg_s[rs, sl], lbv[:, sl]
                dfg = jnp.where(fgv > LOG_FLOOR, dg / fgv, 0.0)
                df_ref[rs, sl] = ((1.0 - lbh) * sgv * (1.0 - sgv) * (dfg - dk)).astype(BF16)
                dlb_ref[:, sl] += jnp.sum((dfg - dk) * (1.0 - sgv), axis=0, keepdims=True)
                if add is None:
                    dq_ref[rs, sl] = dq
                    dv_ref[rs, sl] = dv
                else:
                    dq_ref[rs, sl] = (dq + adds[0][rs, sl]).astype(BF16)
                    dv_ref[rs, sl] = (dv + adds[1][rs, sl]).astype(BF16)
            return c

        lax.fori_loop(0, nsub, back, 0)

    seg = lambda col: pl.BlockSpec((TR, AW), lambda i, col=col: (tile(i), col))
    plain = pl.BlockSpec((TR, AW), lambda i: (tile(i), 0))
    operands = [p, p, p, do, sin, lb]
    in_specs = [seg(0), seg(fcol), seg(3), plain,
                pl.BlockSpec((None, NH, DK, DK), lambda i: (tile(i), 0, 0, 0)), _vec_spec(AW)]
    if add is not None:
        operands += list(add)
        in_specs += [plain, plain]
    return pl.pallas_call(
        body, name=name, grid=(nt,), in_specs=in_specs,
        out_specs=[plain, plain, plain, _vec_spec(AW)],
        out_shape=[jax.ShapeDtypeStruct((r, AW), out_dt), jax.ShapeDtypeStruct((r, AW), out_dt),
                   jax.ShapeDtypeStruct((r, AW), BF16), jax.ShapeDtypeStruct((1, AW), F32)],
        scratch_shapes=[pltpu.VMEM((NH, DK, DK), F32), pltpu.VMEM((NH, DK, DK), F32),
                        pltpu.VMEM((nsub + 1, NH, DK, DK), F32)]
        + [pltpu.VMEM((TR, AW), F32)] * 4,
        compiler_params=_cp(56),
    )(*operands)


def _silu(v):
    return v * jax.nn.sigmoid(v)


def readout_fwd(of, ob, p, ng, name):
    r = of.shape[0]

    def body(of_ref, ob_ref, g_ref, ng_ref, y_ref):
        for h in range(NH):
            sl = slice(h * DK, (h + 1) * DK)
            o = of_ref[:, sl] + ob_ref[:, sl]
            on = o * lax.rsqrt(jnp.mean(o * o, axis=-1, keepdims=True) + EPS) * ng_ref[:, sl]
            y_ref[:, sl] = (on * _silu(g_ref[:, sl])).astype(BF16)

    return pl.pallas_call(
        body, name=name, grid=(r // TR,),
        in_specs=[_row_spec(AW), _row_spec(AW), _row_spec(AW, 4), _vec_spec(AW)], out_specs=_row_spec(AW),
        out_shape=jax.ShapeDtypeStruct((r, AW), BF16), compiler_params=_cp(32),
    )(of, ob, p, ng)


def readout_bwd(dy, of, ob, p, ng, name):
    r = of.shape[0]

    def body(dy_ref, of_ref, ob_ref, g_ref, ng_ref, do_ref, dg_ref, dn_ref):
        i = pl.program_id(0)

        @pl.when(i == 0)
        def _():
            dn_ref[...] = jnp.zeros_like(dn_ref)

        for h in range(NH):
            sl = slice(h * DK, (h + 1) * DK)
            o = of_ref[:, sl] + ob_ref[:, sl]
            rstd = lax.rsqrt(jnp.mean(o * o, axis=-1, keepdims=True) + EPS)
            oh = o * rstd
            gv = g_ref[:, sl]
            sig = jax.nn.sigmoid(gv)
            dyv = dy_ref[:, sl]
            don = dyv * (gv * sig)
            dg_ref[:, sl] = (dyv * (oh * ng_ref[:, sl]) * (sig * (1.0 + gv * (1.0 - sig)))).astype(BF16)
            dn_ref[:, sl] += jnp.sum(don * oh, axis=0, keepdims=True)
            doh = don * ng_ref[:, sl]
            do_ref[:, sl] = rstd * (doh - oh * jnp.mean(doh * oh, axis=-1, keepdims=True))

    return pl.pallas_call(
        body, name=name, grid=(r // TR,),
        in_specs=[_row_spec(AW), _row_spec(AW), _row_spec(AW), _row_spec(AW, 4), _vec_spec(AW)],
        out_specs=[_row_spec(AW), _row_spec(AW), _vec_spec(AW)],
        out_shape=[jax.ShapeDtypeStruct((r, AW), F32), jax.ShapeDtypeStruct((r, AW), BF16),
                   jax.ShapeDtypeStruct((1, AW), F32)],
        compiler_params=_cp(32),
    )(dy, of, ob, p, ng)


def _gelu(v):
    return 0.5 * v * (1.0 + lax.erf(v * 0.7071067811865476))


def _gelu_grad(v):
    return 0.5 * (1.0 + lax.erf(v * 0.7071067811865476)) + v * (0.3989422804014327 * jnp.exp(-0.5 * v * v))


def _cmlp_norm(vv, gn):
    vg = _gelu(vv)
    mu = jnp.mean(vg, axis=-1, keepdims=True)
    cen = vg - mu
    rstd = lax.rsqrt(jnp.mean(cen * cen, axis=-1, keepdims=True) + EPS)
    xhat = cen * rstd
    return xhat, rstd, xhat * gn


def chunkmlp_fwd(p, ws, bias, gn, name):
    r = p.shape[0]

    def body(u_ref, v_ref, ws_ref, b_ref, gn_ref, y_ref):
        for ci in range(TR // B_CHUNK):
            rs = slice(ci * B_CHUNK, (ci + 1) * B_CHUNK)
            for gidx in range(NH):
                sl = slice(gidx * DK, (gidx + 1) * DK)
                _, _, vn = _cmlp_norm(v_ref[rs, sl], gn_ref[:, sl])
                mixed = _nn(_bf(ws_ref[gidx]), _bf(vn)) + b_ref[gidx]
                y_ref[rs, sl] = (_gelu(u_ref[rs, sl]) * mixed).astype(BF16)

    return pl.pallas_call(
        body, name=name, grid=(r // TR,),
        in_specs=[_row_spec(AW, 5), _row_spec(AW, 6), pl.BlockSpec((NH, B_CHUNK, B_CHUNK), lambda i: (0, 0, 0)),
                  pl.BlockSpec((NH, B_CHUNK, 1), lambda i: (0, 0, 0)), _vec_spec(AW)],
        out_specs=_row_spec(AW), out_shape=jax.ShapeDtypeStruct((r, AW), BF16), compiler_params=_cp(32),
    )(p, p, ws, bias, gn)


def chunkmlp_bwd(dy, p, ws, bias, gn, name):
    r = p.shape[0]

    def body(dy_ref, u_ref, v_ref, ws_ref, b_ref, gn_ref, du_ref, dv_ref, dws_ref, db_ref, dgn_ref):
        i = pl.program_id(0)

        @pl.when(i == 0)
        def _():
            dws_ref[...] = jnp.zeros_like(dws_ref)
            db_ref[...] = jnp.zeros_like(db_ref)
            dgn_ref[...] = jnp.zeros_like(dgn_ref)

        for ci in range(TR // B_CHUNK):
            rs = slice(ci * B_CHUNK, (ci + 1) * B_CHUNK)
            for gidx in range(NH):
                sl = slice(gidx * DK, (gidx + 1) * DK)
                vv, uv, dyv, gnv = v_ref[rs, sl], u_ref[rs, sl], dy_ref[rs, sl], gn_ref[:, sl]
                xhat, rstd, vn = _cmlp_norm(vv, gnv)
                wg = _bf(ws_ref[gidx])
                mixed = _nn(wg, _bf(vn)) + b_ref[gidx]
                dmixed = dyv * _gelu(uv)
                du_ref[rs, sl] = (dyv * mixed * _gelu_grad(uv)).astype(BF16)
                dws_ref[gidx] += _nt(_bf(dmixed), _bf(vn))
                db_ref[gidx] += jnp.sum(dmixed, axis=-1, keepdims=True)
                dvn = _tn(wg, _bf(dmixed))
                dgn_ref[:, sl] += jnp.sum(dvn * xhat, axis=0, keepdims=True)
                dxh = dvn * gnv
                dvg = rstd * (dxh - jnp.mean(dxh, axis=-1, keepdims=True)
                              - xhat * jnp.mean(dxh * xhat, axis=-1, keepdims=True))
                dv_ref[rs, sl] = (dvg * _gelu_grad(vv)).astype(BF16)

    return pl.pallas_call(
        body, name=name, grid=(r // TR,),
        in_specs=[_row_spec(AW, 1), _row_spec(AW, 5), _row_spec(AW, 6),
                  pl.BlockSpec((NH, B_CHUNK, B_CHUNK), lambda i: (0, 0, 0)),
                  pl.BlockSpec((NH, B_CHUNK, 1), lambda i: (0, 0, 0)), _vec_spec(AW)],
        out_specs=[_row_spec(AW), _row_spec(AW), pl.BlockSpec((NH, B_CHUNK, B_CHUNK), lambda i: (0, 0, 0)),
                   pl.BlockSpec((NH, B_CHUNK, 1), lambda i: (0, 0, 0)), _vec_spec(AW)],
        out_shape=[jax.ShapeDtypeStruct((r, AW), BF16), jax.ShapeDtypeStruct((r, AW), BF16),
                   jax.ShapeDtypeStruct((NH, B_CHUNK, B_CHUNK), F32), jax.ShapeDtypeStruct((NH, B_CHUNK, 1), F32),
                   jax.ShapeDtypeStruct((1, AW), F32)],
        compiler_params=_cp(32),
    )(dy, p, p, ws, bias, gn)


def _win_count(pos, k, n):
    lo = jnp.maximum(pos - k // 2, 0)
    hi = jnp.minimum(pos - k // 2 + k, n)
    return (hi - lo).astype(F32)


POOL_CW = 256
POOL_PAD = (POOL_WINDOWS[-1] // 2) * GRID_W


def pool_op(p, transpose, name):
    r = p.shape[0]
    seq = r - CTX
    grows = seq // GRID_W
    nt = seq // TR

    def body(x_ref, o_ref, y_s):
        j = pl.program_id(0)

        @pl.when(j == 0)
        def _():
            y_s[:POOL_PAD, :] = jnp.zeros((POOL_PAD, POOL_CW), F32)
            y_s[POOL_PAD + seq:, :] = jnp.zeros((POOL_PAD, POOL_CW), F32)

        for gi, k in enumerate(POOL_WINDOWS):
            @pl.when(j // (512 // POOL_CW) == gi)
            def _(k=k):
                offs = list(range(-(k // 2) + 1, k // 2 + 1) if transpose else range(-(k // 2), k // 2))
                tt = lax.broadcasted_iota(jnp.int32, (TR, TR), 0)
                ss = lax.broadcasted_iota(jnp.int32, (TR, TR), 1)
                band = (ss - tt >= offs[0]) & (ss - tt <= offs[-1])
                b_ctx = jnp.where(band, 1.0, 0.0).astype(F32)
                b_grid = jnp.where(band & ((tt >> 6) == (ss >> 6)), 1.0, 0.0).astype(F32)
                trow = lax.broadcasted_iota(jnp.int32, (TR, POOL_CW), 0)

                def count(i):
                    t = i * TR + trow
                    return _win_count(t & (GRID_W - 1), k, GRID_W) * _win_count(t >> 6, k, grows)

                def col_pass(i, carry):
                    xt = x_ref[pl.ds(pl.multiple_of(CTX + i * TR, TR), TR), :]
                    if transpose:
                        xt = xt / count(i)
                    y_s[pl.ds(pl.multiple_of(POOL_PAD + i * TR, TR), TR), :] = jnp.dot(
                        b_grid, xt, precision=HIGHEST, preferred_element_type=F32)
                    return carry

                lax.fori_loop(0, nt, col_pass, 0)

                def row_pass(i, carry):
                    base = POOL_PAD + i * TR
                    acc = y_s[pl.ds(pl.multiple_of(base + offs[0] * GRID_W, GRID_W), TR), :]
                    for d in offs[1:]:
                        acc = acc + y_s[pl.ds(pl.multiple_of(base + d * GRID_W, GRID_W), TR), :]
                    rows = pl.ds(pl.multiple_of(CTX + i * TR, TR), TR)
                    if not transpose:
                        acc = acc / count(i)
                    o_ref[rows, :] = (acc - x_ref[rows, :]).astype(BF16)
                    return carry

                lax.fori_loop(0, nt, row_pass, 0)

                cx = x_ref[:CTX, :]
                cntc = _win_count(trow, k, CTX)
                accc = jnp.dot(b_ctx, cx / cntc if transpose else cx, precision=HIGHEST, preferred_element_type=F32)
                o_ref[:CTX, :] = ((accc if transpose else accc / cntc) - cx).astype(BF16)

    spec = pl.BlockSpec((r, POOL_CW), lambda j: (0, j))
    return pl.pallas_call(
        body, name=name, grid=(D // POOL_CW,), in_specs=[spec], out_specs=spec,
        out_shape=jax.ShapeDtypeStruct((r, D), BF16),
        scratch_shapes=[pltpu.VMEM((seq + 2 * POOL_PAD, POOL_CW), F32)], compiler_params=_cp(56),
    )(p)


def ada_mods(cs, w_ada, b_loc, name):
    nl, _, cl = w_ada.shape
    tn = 1024

    def body(c_ref, w_ref, b_ref, o_ref):
        o_ref[...] = _nn(_bf(_silu(c_ref[...])), _bf(w_ref[...])) + b_ref[...]

    return pl.pallas_call(
        body, name=name, grid=(nl, cl // tn),
        in_specs=[pl.BlockSpec((16, D), lambda l, j: (0, 0)), pl.BlockSpec((None, D, tn), lambda l, j: (l, 0, j)),
                  pl.BlockSpec((None, 1, tn), lambda l, j: (l, 0, j))],
        out_specs=pl.BlockSpec((None, 16, tn), lambda l, j: (l, 0, j)),
        out_shape=jax.ShapeDtypeStruct((nl, 16, cl), F32), compiler_params=_cp(40),
    )(cs, w_ada, b_loc)


def _adamw(w, g, m, v):
    m = ADAM_B1 * m + (1.0 - ADAM_B1) * g
    v = ADAM_B2 * v + (1.0 - ADAM_B2) * (g * g)
    m_hat = m / (1.0 - ADAM_B1 ** ADAM_STEP)
    v_hat = v / (1.0 - ADAM_B2 ** ADAM_STEP)
    delta = -ADAM_LR * (m_hat / (jnp.sqrt(v_hat) + ADAM_EPS) + ADAM_WD * w)
    return delta, m, v


def ada_update(cs, dm, w, m, v, name):
    nl, _, cl = w.shape
    ta, tn = 256, 1024

    def body(c_ref, dm_ref, w_ref, m_ref, v_ref, g_ref, d_ref, nm_ref, nv_ref, dc_ref):
        l, j = pl.program_id(1), pl.program_id(2)
        a = _bf(_silu(c_ref[...]))
        bmat = _bf(dm_ref[...])
        wv = w_ref[...]
        g = _tn(a, bmat)
        g_ref[...] = g
        d_ref[...], nm_ref[...], nv_ref[...] = _adamw(wv, g, m_ref[...], v_ref[...])
        _accum(dc_ref, _nt(bmat, _bf(wv)), (l == 0) & (j == 0))

    wspec = pl.BlockSpec((None, ta, tn), lambda i, l, j: (l, i, j))
    shp = jax.ShapeDtypeStruct(w.shape, F32)
    return pl.pallas_call(
        body, name=name, grid=(D // ta, nl, cl // tn),
        in_specs=[pl.BlockSpec((16, ta), lambda i, l, j: (0, i)),
                  pl.BlockSpec((None, 16, tn), lambda i, l, j: (l, 0, j)), wspec, wspec, wspec],
        out_specs=[wspec, wspec, wspec, wspec, pl.BlockSpec((16, ta), lambda i, l, j: (0, i))],
        out_shape=[shp, shp, shp, shp, jax.ShapeDtypeStruct((16, D), F32)], compiler_params=_cp(40),
    )(cs, dm, w, m, v)


def adamw_big(p, q, w, m, v, l, dsts, name):
    nl = w.shape[0]
    rows, cols = p.shape
    tr = min(rows, 256)
    w3, m3, v3 = (t.reshape(nl, rows, cols) for t in (w, m, v))
    nd = 0 if dsts is None else 4

    def body(p_ref, q_ref, w_ref, m_ref, v_ref, *rest):
        g_ref, d_ref, nm_ref, nv_ref = rest[nd:]
        g = p_ref[...].astype(F32) + q_ref[...].astype(F32)
        g_ref[...] = g
        d_ref[...], nm_ref[...], nv_ref[...] = _adamw(w_ref[...], g, m_ref[...], v_ref[...])

    part = pl.BlockSpec((tr, cols), lambda i: (i, 0))
    spec = pl.BlockSpec((None, tr, cols), lambda i: (l, i, 0))
    shp = jax.ShapeDtypeStruct((nl, rows, cols), F32)
    return pl.pallas_call(
        body, name=name, grid=(rows // tr,),
        in_specs=[part, part, spec, spec, spec] + [ANY] * nd,
        out_specs=[spec] * 4, out_shape=[shp] * 4,
        input_output_aliases={5 + t: t for t in range(nd)}, compiler_params=_cp(48),
    )(p, q, w3, m3, v3, *([] if dsts is None else dsts))


def sum8(g8, name):
    n = g8.shape[1]

    def body(g_ref, o_ref):
        acc = g_ref[0]
        for dev in range(1, 8):
            acc = acc + g_ref[dev]
        o_ref[...] = acc

    return pl.pallas_call(
        body, name=name, grid=(1,), in_specs=[pl.BlockSpec((8, n, 128), lambda i: (0, 0, 0))],
        out_specs=pl.BlockSpec((n, 128), lambda i: (0, 0)),
        out_shape=jax.ShapeDtypeStruct((n, 128), F32), compiler_params=_cp(48),
    )(g8)


def adamw_small(g, w, m, v, name):
    def body(g_ref, w_ref, m_ref, v_ref, d_ref, nm_ref, nv_ref):
        d_ref[...], nm_ref[...], nv_ref[...] = _adamw(w_ref[...], g_ref[...], m_ref[...], v_ref[...])

    spec = pl.BlockSpec(g.shape, lambda i: (0, 0))
    shp = jax.ShapeDtypeStruct(g.shape, F32)
    return pl.pallas_call(
        body, name=name, grid=(1,), in_specs=[spec] * 4, out_specs=[spec] * 3, out_shape=[shp] * 3,
        compiler_params=_cp(48),
    )(g, w, m, v)


def _pack(arrs):
    flat = jnp.concatenate([a.reshape(-1).astype(F32) for a in arrs])
    pad = (-flat.shape[0]) % 1024
    return jnp.pad(flat, (0, pad)).reshape(-1, 128)


def _unpack(packed, shapes):
    flat = packed.reshape(-1)
    out, off = [], 0
    for s in shapes:
        n = 1
        for d in s:
            n *= d
        out.append(flat[off:off + n].reshape(s))
        off += n
    return out


def _lower_bounds(lb_logits):
    pr = jax.nn.softmax(lb_logits.astype(F32), axis=1)
    return jnp.cumsum(pr, axis=1) - pr[:, :1]


class LayerWeights:
    def __init__(self, fetch):
        self.fetch, self.have, self.tokens = fetch, {}, []

    def get(self, key, after):
        if key not in self.have:
            new, token = self.fetch(key, after)
            self.have.update(new)
            if token is not None:
                self.tokens.append(token)
        return self.have[key]

    def tie(self, mod):
        for token in self.tokens:
            mod = mod + token[0, 0]
        self.tokens = []
        return mod


def device_step(xs, target, mods, sp, weights_for, grads_done):
    lbs = _lower_bounds(sp["lb_logits"])
    saved = []
    for layer in range(4):
        wl, md = weights_for(layer), mods[layer]
        w_in = wl.get("in" if layer % 2 == 0 else "pin", xs)
        md = wl.tie(md)
        s = {"xs": xs}
        h1 = norm_mod(xs, sp["g_norm_mix"][layer][None], md, 0, f"norm_mix{layer}")
        s["h1"] = h1
        if layer % 2 == 0:
            e = layer // 2
            p = mm_cols(h1, w_in, 0, f"in_proj{layer}")
            of, sf = hgrn_fwd(p, lbs[0, e][None], False, f"scan_f{layer}")
            ob, sb = hgrn_fwd(p, lbs[1, e][None], True, f"scan_b{layer}")
            ya = readout_fwd(of, ob, p, sp["g_hgrn_out"][e][None], f"readout{layer}")
            yb = chunkmlp_fwd(p, sp["w_spatial"][e], sp["b_spatial"][e][:, :, None], sp["g_spatial_v"][e][None],
                              f"cmlp{layer}")
            ycat = jnp.concatenate([ya, yb], axis=1)
            w_out = wl.get("out", ycat)
            md = wl.tie(md)
            x1, f1 = mm_rows_res(ycat, w_out, 0, xs, md, 2, f"out_proj{layer}")
            s.update(p=p, of=of, ob=ob, sf=sf, sb=sb, ycat=ycat, f1=f1)
        else:
            o = layer // 2
            pp = mm_rows(h1, w_in, 0, f"pool_in{layer}")
            z = pool_op(pp, False, f"pool{layer}")
            x1, ypre = mm_grp_res(z, wl.get("grp", z), 0, sp["b_grp_pool"][o].reshape(1, D),
                                  sp["scale_pool"][o][None], xs, md, 2, f"pool_grp{layer}")
            s.update(z=z, ypre=ypre)
        h2 = norm_mod(x1, sp["g_norm_ffn"][layer][None], md, 3, f"norm_ffn{layer}")
        u = mm_cols(h2, wl.get("up", h2), 0, f"ffn_up{layer}", epi=_epi_relu2, out_dtype=BF16)
        x2, f2 = mm_rows_res(u, wl.get("down", u), 0, x1, md, 5, f"ffn_down{layer}")
        s.update(x1=x1, h2=h2, u=u, f2=f2, w=wl.have, md=md)
        saved.append(s)
        xs = x2

    dx, loss_lanes, dg_final = final_loss(xs, sp["g_norm_final"][None], target, "final_loss")

    token = jnp.zeros((8, 128), F32)
    dmods = [None] * 4
    sg = {"g_norm_final": dg_final[0], "g_norm_mix": [None] * 4, "g_norm_ffn": [None] * 4,
          "dlbs": [[None, None], [None, None]], "g_hgrn_out": [None] * 2, "w_spatial": [None] * 2,
          "b_spatial": [None] * 2, "g_spatial_v": [None] * 2, "b_grp_pool": [None] * 2, "scale_pool": [None] * 2}
    for layer in reversed(range(4)):
        s = saved[layer]
        wl, md = s["w"], s["md"] + token[0, 0]
        df2, dgt2 = gate_in(dx, s["f2"], md, 5, f"gate_ffn{layer}")
        da = mm_t_rows(df2, wl["down"], 0, f"ffn_down_t{layer}", epi=_epi_2sqrt, out_dtype=BF16,
                       extras_of=s["u"], per=2)
        g_down = grad_rows(s["u"], df2, f"g_ffn_down{layer}", per=2, tn=2048)
        dh2 = mm_t_cols(da, wl["up"], 0, f"ffn_up_t{layer}")
        g_up = grad_cols(s["h2"], da, f"g_ffn_up{layer}", per=1)
        dx1, dsh2, dsc2, dgf = normmod_bwd(dh2, s["x1"], dx, sp["g_norm_ffn"][layer][None], md, 3,
                                           f"norm_ffn_b{layer}")
        sg["g_norm_ffn"][layer] = dgf[0]
        md = md + grads_done(layer, "ffn", [g_up, g_down], dx1)[0, 0]
        if layer % 2 == 0:
            e = layer // 2
            df1, dgt1 = gate_in(dx1, s["f1"], md, 2, f"gate_mix{layer}")
            dycat = mm_t_rows(df1, wl["out"], 0, f"out_proj_t{layer}")
            g_b = grad_rows(s["ycat"], df1, f"g_out_proj{layer}", tn=2048)
            do, dpg, dng = readout_bwd(dycat, s["of"], s["ob"], s["p"], sp["g_hgrn_out"][e][None],
                                       f"readout_b{layer}")
            du, dv, dws, dbs, dgn = chunkmlp_bwd(dycat, s["p"], sp["w_spatial"][e], sp["b_spatial"][e][:, :, None],
                                                 sp["g_spatial_v"][e][None], f"cmlp_b{layer}")
            dq_f, di_f, dff, dlb_f = hgrn_bwd(s["p"], do, s["sf"], lbs[0, e][None], False, f"scan_f_b{layer}")
            dq, di, dfb, dlb_b = hgrn_bwd(s["p"], do, s["sb"], lbs[1, e][None], True, f"scan_b_b{layer}",
                                          add=(dq_f, di_f))
            dp = jnp.concatenate([dq, dff, dfb, di, dpg, du, dv], axis=1)
            dh1 = mm_t_cols(dp, wl["in"], 0, f"in_proj_t{layer}")
            g_a = grad_cols(s["h1"], dp, f"g_in_proj{layer}", per=1)
            sg["dlbs"][0][e], sg["dlbs"][1][e] = dlb_f[0], dlb_b[0]
            sg["g_hgrn_out"][e], sg["w_spatial"][e] = dng[0], dws
            sg["b_spatial"][e], sg["g_spatial_v"][e] = dbs[:, :, 0], dgn[0]
        else:
            o = layer // 2
            dyp, dgt1, dscale, dbias = gate_in_pool(dx1, s["ypre"], md, sp["scale_pool"][o][None], 2,
                                                    f"gate_mix{layer}")
            dz = mm_t_grp(dyp, wl["grp"], 0, f"pool_grp_t{layer}")
            g_b = grad_grp(s["z"], dyp, f"g_pool_grp{layer}")
            dpp = pool_op(dz, True, f"pool_t{layer}")
            dh1 = mm_t_rows(dpp, wl["pin"], 0, f"pool_in_t{layer}")
            g_a = grad_rows(s["h1"], dpp, f"g_pool_in{layer}", tn=2048)
            sg["b_grp_pool"][o], sg["scale_pool"][o] = dbias[0].reshape(4, 512), dscale[0]
        dx, dsh1, dsc1, dgm = normmod_bwd(dh1, s["xs"], dx1, sp["g_norm_mix"][layer][None], md, 0,
                                          f"norm_mix_b{layer}")
        sg["g_norm_mix"][layer] = dgm[0]
        dmods[layer] = jnp.concatenate([dsh1, dsc1, dgt1, dsh2, dsc2, dgt2], axis=1)
        token = grads_done(layer, "mix", [g_a, g_b], dx)
    return loss_lanes, dx, dmods, sg


BIG = ("w_in_even", "w_out_even", "w_in_pool", "w_grp_pool", "w_ffn_up", "w_ffn_down")
SMALL = ("b_ada", "g_norm_mix", "g_norm_ffn", "lb_logits", "g_hgrn_out", "w_spatial", "b_spatial", "g_spatial_v",
         "b_grp_pool", "scale_pool", "g_norm_final")
WEIGHTS = ("c_ctx", "w_ada", "b_ada", "g_norm_mix", "g_norm_ffn", "w_in_even", "w_out_even", "lb_logits",
           "g_hgrn_out", "w_spatial", "b_spatial", "g_spatial_v", "w_in_pool", "w_grp_pool", "b_grp_pool",
           "scale_pool", "w_ffn_up", "w_ffn_down", "g_norm_final")


def kernel(x, c, ctx, c_ctx, w_ada, b_ada, g_norm_mix, g_norm_ffn, w_in_even, w_out_even, lb_logits, g_hgrn_out, w_spatial, b_spatial, g_spatial_v, w_in_pool, w_grp_pool, b_grp_pool, scale_pool, w_ffn_up, w_ffn_down, g_norm_final, loss_target, m_c_ctx, m_w_ada, m_b_ada, m_g_norm_mix, m_g_norm_ffn, m_w_in_even, m_w_out_even, m_lb_logits, m_g_hgrn_out, m_w_spatial, m_b_spatial, m_g_spatial_v, m_w_in_pool, m_w_grp_pool, m_b_grp_pool, m_scale_pool, m_w_ffn_up, m_w_ffn_down, m_g_norm_final, v_c_ctx, v_w_ada, v_b_ada, v_g_norm_mix, v_g_norm_ffn, v_w_in_even, v_w_out_even, v_lb_logits, v_g_hgrn_out, v_w_spatial, v_b_spatial, v_g_spatial_v, v_w_in_pool, v_w_grp_pool, v_b_grp_pool, v_scale_pool, v_w_ffn_up, v_w_ffn_down, v_g_norm_final):
    loc = dict(c_ctx=c_ctx, w_ada=w_ada, b_ada=b_ada, g_norm_mix=g_norm_mix, g_norm_ffn=g_norm_ffn,
               w_in_even=w_in_even, w_out_even=w_out_even, lb_logits=lb_logits, g_hgrn_out=g_hgrn_out,
               w_spatial=w_spatial, b_spatial=b_spatial, g_spatial_v=g_spatial_v, w_in_pool=w_in_pool,
               w_grp_pool=w_grp_pool, b_grp_pool=b_grp_pool, scale_pool=scale_pool, w_ffn_up=w_ffn_up,
               w_ffn_down=w_ffn_down, g_norm_final=g_norm_final)
    mom = dict(c_ctx=m_c_ctx, w_ada=m_w_ada, b_ada=m_b_ada, g_norm_mix=m_g_norm_mix, g_norm_ffn=m_g_norm_ffn,
               w_in_even=m_w_in_even, w_out_even=m_w_out_even, lb_logits=m_lb_logits, g_hgrn_out=m_g_hgrn_out,
               w_spatial=m_w_spatial, b_spatial=m_b_spatial, g_spatial_v=m_g_spatial_v, w_in_pool=m_w_in_pool,
               w_grp_pool=m_w_grp_pool, b_grp_pool=m_b_grp_pool, scale_pool=m_scale_pool, w_ffn_up=m_w_ffn_up,
               w_ffn_down=m_w_ffn_down, g_norm_final=m_g_norm_final)
    var = dict(c_ctx=v_c_ctx, w_ada=v_w_ada, b_ada=v_b_ada, g_norm_mix=v_g_norm_mix, g_norm_ffn=v_g_norm_ffn,
               w_in_even=v_w_in_even, w_out_even=v_w_out_even, lb_logits=v_lb_logits, g_hgrn_out=v_g_hgrn_out,
               w_spatial=v_w_spatial, b_spatial=v_b_spatial, g_spatial_v=v_g_spatial_v, w_in_pool=v_w_in_pool,
               w_grp_pool=v_w_grp_pool, b_grp_pool=v_b_grp_pool, scale_pool=v_scale_pool, w_ffn_up=v_w_ffn_up,
               w_ffn_down=v_w_ffn_down, g_norm_final=v_g_norm_final)
    mx, my, mc = _mesh_pos()
    chip = 2 * mx + my
    dev = 2 * chip + mc

    def layer_tensors(layer):
        i = layer // 2
        pair = ("w_in_even", "w_out_even") if layer % 2 == 0 else ("w_in_pool", "w_grp_pool")
        return [(pair[0], i), (pair[1], i), ("w_ffn_up", layer), ("w_ffn_down", layer)]

    def layer_keys(layer):
        return ("in", "out", "up", "down") if layer % 2 == 0 else ("pin", "grp", "up", "down")

    def gather_parts(layer):
        return {"a": slice(0, 1), "b": slice(1, 4)} if layer == 0 else {"a": slice(0, 4)}

    def start_gather(layer, part, after):
        srcs = [loc[n][i][None].astype(BF16) for n, i in layer_tensors(layer)[gather_parts(layer)[part]]]
        lands = [lax.dynamic_update_slice(lax.empty((NSH,) + s.shape, BF16), s[None], (chip,) + (0,) * s.ndim)
                 for s in srcs]
        return ici_start(srcs, lands, after, False, f"gather_start{layer}{part}")

    hello = allgather8(_pack([c[0], lb_logits, b_grp_pool, scale_pool]), "gather_small")
    parts = [_unpack(hello[d], [(D,), (2, 2, 256), (2, 4, 128), (2, 512)]) for d in range(8)]
    cs = jnp.concatenate([jnp.stack([parts[d][0] for d in range(8)]), c_ctx[None], jnp.zeros((7, D), F32)])
    chips_of = [parts[2 * s] for s in range(NSH)]
    sp = dict(loc)
    sp["lb_logits"] = jnp.concatenate([q[1] for q in chips_of], axis=2)
    sp["b_grp_pool"] = jnp.concatenate([q[2] for q in chips_of], axis=2)
    sp["scale_pool"] = jnp.concatenate([q[3] for q in chips_of], axis=1)

    b_loc = lax.dynamic_slice_in_dim(b_ada, chip * 3072, 3072, axis=1)[:, None, :]
    mods_loc = ada_mods(cs, w_ada, b_loc, "ada_mods")
    mods_all = allgather8(mods_loc.reshape(-1, 128), "gather_mods").reshape(8, 4, 16, 3072)
    mods_full = jnp.concatenate([mods_all[2 * s] for s in range(NSH)], axis=2)
    mine = lax.dynamic_index_in_dim(mods_full, dev, axis=1, keepdims=False)
    mods = [jnp.stack([mods_full[l, 8].reshape(6, D), mine[l].reshape(6, D)]) for l in range(4)]

    first_a = start_gather(0, "a", mods_all)
    gathers = {(0, "a"): first_a, (0, "b"): start_gather(0, "b", first_a[4])}
    mods[0] = mods[0] + gathers[(0, "b")][4][0, 0]

    def weights_for(layer):
        def fetch(key, after):
            part = "b" if layer == 0 and key != "in" else "a"
            _, got = ici_wait(gathers[(layer, part)], after, False, f"gather_wait{layer}{part}")
            token = None
            if layer < 3 and part == list(gather_parts(layer))[-1]:
                gathers[(layer + 1, "a")] = start_gather(layer + 1, "a", got[0])
                token = gathers[(layer + 1, "a")][4]
            return dict(zip(layer_keys(layer)[gather_parts(layer)[part]], got)), token
        return LayerWeights(fetch)

    exchanges = {}

    def grads_done(layer, part, gs, after):
        lands = [lax.empty((8,) + g.shape[1:], BF16) for g in gs]
        exchanges[(layer, part)] = ici_start(gs, lands, after, True, f"exchange_start{layer}{part}")
        return exchanges[(layer, part)][4]

    xs = jnp.concatenate([ctx[0], x[0]], axis=0)
    loss_lanes, dxs, dmods, sg = device_step(xs, loss_target[0], mods, sp, weights_for, grads_done)
    grad_x = dxs[CTX:][None]

    out = {}

    def finish(layer, part, after):
        gs, rbs = ici_wait(exchanges[(layer, part)], after, True, f"exchange_wait{layer}{part}")
        names = layer_tensors(layer)[slice(2, 4) if part == "ffn" else slice(0, 2)]
        ps = [sum_blocks(g, rb, f"sum_{n}{i}") for (n, i), g, rb in zip(names, gs, rbs)]
        qs = swap_sibling(ps, f"swap{layer}{part}")
        for (n, i), p, q in zip(names, ps, qs):
            out[n] = adamw_big(p, q, loc[n], mom[n], var[n], i, out.get(n), f"adamw_{n}{i}")

    done = dmods[0] + exchanges[(0, "mix")][4][0, 0]
    dmods[0] = done
    for layer in (3, 2, 1):
        finish(layer, "ffn", done)
        finish(layer, "mix", done)
    finish(0, "ffn", done)

    dm_lat = jnp.stack([dmods[l][1].reshape(6 * D) for l in range(4)])
    dm_ctx = jnp.stack([dmods[l][0].reshape(6 * D) for l in range(4)])
    small_shapes = [(4, 6 * D), (4, 6 * D), (4, D), (4, D), (2, 2, AW), (2, AW), (2, NH, 128, 128), (2, NH, 128),
                    (2, AW), (2, 4, 512), (2, D), (D,), (128,)]
    mine_small = _pack([dm_lat, dm_ctx, jnp.stack(sg["g_norm_mix"]), jnp.stack(sg["g_norm_ffn"]),
                        jnp.stack([jnp.stack(sg["dlbs"][0]), jnp.stack(sg["dlbs"][1])]),
                        jnp.stack(sg["g_hgrn_out"]), jnp.stack(sg["w_spatial"]), jnp.stack(sg["b_spatial"]),
                        jnp.stack(sg["g_spatial_v"]), jnp.stack(sg["b_grp_pool"]), jnp.stack(sg["scale_pool"]),
                        sg["g_norm_final"], loss_lanes[0]])
    all_small = allgather8(mine_small, "gather_small_grads")
    tot = _unpack(sum8(all_small, "sum_small_grads"), small_shapes)
    (_, dm_ctx_tot, g_mix, g_ffn, dlbs, g_hg, g_ws, g_bs, g_gv, g_bg, g_sc, g_fin, loss_v) = tot
    loss = loss_v[0]
    dm_lat_all = jnp.stack([_unpack(all_small[d], small_shapes[:1])[0] for d in range(8)])
    g_b_ada = jnp.sum(dm_lat_all, axis=0) + dm_ctx_tot
    _, lb_vjp = jax.vjp(_lower_bounds, sp["lb_logits"])
    g_lb_full = lb_vjp(dlbs)[0]
    grads = {"b_ada": g_b_ada, "g_norm_mix": g_mix, "g_norm_ffn": g_ffn,
             "lb_logits": lax.dynamic_slice_in_dim(g_lb_full, chip * 256, 256, axis=2),
             "g_hgrn_out": g_hg, "w_spatial": g_ws, "b_spatial": g_bs, "g_spatial_v": g_gv,
             "b_grp_pool": lax.dynamic_slice_in_dim(g_bg, chip * 128, 128, axis=2),
             "scale_pool": lax.dynamic_slice_in_dim(g_sc, chip * 512, 512, axis=1), "g_norm_final": g_fin}

    dm_rows = jnp.concatenate([dm_lat_all.transpose(1, 0, 2), dm_ctx_tot[:, None, :], jnp.zeros((4, 7, 6 * D), F32)],
                              axis=1)
    dm_loc = lax.dynamic_slice_in_dim(dm_rows, chip * 3072, 3072, axis=2)
    g_wa, d_wa, nm_wa, nv_wa, dc_part = ada_update(cs, dm_loc, w_ada, m_w_ada, v_w_ada, "ada_update")
    out["w_ada"] = [g_wa, d_wa, nm_wa, nv_wa]
    dc_all = allgather8(dc_part[8].reshape(16, 128), "gather_dc")
    dpre = dc_all[0] + dc_all[2] + dc_all[4] + dc_all[6]
    sig = jax.nn.sigmoid(c_ctx)
    grads["c_ctx"] = dpre.reshape(D) * (sig * (1.0 + c_ctx * (1.0 - sig)))

    names = ("c_ctx",) + SMALL
    shapes = [loc[n].shape for n in names]
    d_s, nm_s, nv_s = adamw_small(_pack([grads[n] for n in names]), _pack([loc[n] for n in names]),
                                  _pack([mom[n] for n in names]), _pack([var[n] for n in names]), "adamw_small")
    for n, dl, nm, nv in zip(names, _unpack(d_s, shapes), _unpack(nm_s, shapes), _unpack(nv_s, shapes)):
        out[n] = [grads[n], dl, nm, nv]

    finish(0, "mix", d_s)
    for n in BIG:
        out[n] = [t.reshape(loc[n].shape) for t in out[n]]

    return (loss, grad_x, *[out[n][0] for n in WEIGHTS], *[out[n][1] for n in WEIGHTS],
            *[out[n][2] for n in WEIGHTS], *[out[n][3] for n in WEIGHTS])
```

```python
import functools

import jax
import jax.numpy as jnp
from jax import lax
from jax.experimental import pallas as pl
from jax.experimental.pallas import tpu as pltpu

F32 = jnp.float32
BF16 = jnp.bfloat16
MESH = pl.DeviceIdType.MESH
ANY = pl.BlockSpec(memory_space=pl.ANY)
HIGHEST = lax.Precision.HIGHEST

D = 2048
DFF = 8192
CTX = 256
TR = 256
GRID_W = 64
EPS = 1e-6
LOG_FLOOR = 1e-30
NH = 8
DK = 128
SUB_FWD = 16
SUB_BWD = 32
B_CHUNK = 128
AW = NH * DK
POOL_WINDOWS = (2, 4, 8, 16)
NSH = 4
ADAM_LR, ADAM_B1, ADAM_B2, ADAM_EPS, ADAM_WD, ADAM_STEP = 0.001, 0.9, 0.999, 1e-08, 0.01, 10
MIB = 1024 * 1024


def _cp(vmem_mib):
    return pltpu.CompilerParams(vmem_limit_bytes=vmem_mib * MIB)


def _bf(v):
    return v.astype(BF16)


def _nn(a, b):
    return lax.dot_general(a, b, (((1,), (0,)), ((), ())), preferred_element_type=F32)


def _nt(a, b):
    return lax.dot_general(a, b, (((1,), (1,)), ((), ())), preferred_element_type=F32)


def _tn(a, b):
    return lax.dot_general(a, b, (((0,), (0,)), ((), ())), preferred_element_type=F32)


def _mesh_pos():
    return lax.axis_index("x"), lax.axis_index("y"), lax.axis_index("c")


HBM = pl.BlockSpec(memory_space=pltpu.HBM)
SEM = pl.BlockSpec(memory_space=pltpu.SEMAPHORE)
EFFECT = pltpu.SideEffectType.DATAFLOW_SIDE_EFFECTING


def _peer_copies(exchange, srcs, lands, send_sems, recv_sems):
    x, y, c = _mesh_pos()
    me = 2 * x + y
    pairs = []
    for t in range(len(srcs)):
        for j, (px, py) in enumerate([(1 - x, y), (x, 1 - y), (1 - x, 1 - y)]):
            peer = 2 * px + py
            src = srcs[t].at[peer] if exchange else srcs[t]
            out_slot, in_slot = (2 * me + c, 2 * peer + c) if exchange else (me, peer)

            def mk(slot, t=t, j=j, px=px, py=py, src=src):
                return pltpu.make_async_remote_copy(
                    src_ref=src, dst_ref=lands[t].at[slot], send_sem=send_sems.at[3 * t + j],
                    recv_sem=recv_sems.at[3 * t + j], device_id=(px, py, c), device_id_type=MESH)

            pairs.append((mk(out_slot), mk(in_slot)))
    return pairs


def ici_start(srcs, lands, after, exchange, name):
    n = len(srcs)

    def body(*refs):
        send_sems, recv_sems = refs[2 * n + 1], refs[2 * n + 2]
        for out_copy, _ in _peer_copies(exchange, refs[:n], refs[n:2 * n], send_sems, recv_sems):
            out_copy.start()
        refs[-1][...] = jnp.zeros_like(refs[-1])

    arrs = list(srcs) + list(lands)
    res = pl.pallas_call(
        body, name=name,
        out_shape=(pltpu.SemaphoreType.DMA((3 * n,)), pltpu.SemaphoreType.DMA((3 * n,)),
                   *[pltpu.HBM(a.shape, a.dtype) for a in arrs], jax.ShapeDtypeStruct((8, 128), F32)),
        in_specs=[HBM] * (2 * n) + [ANY],
        out_specs=(SEM, SEM, *[HBM] * (2 * n), pl.BlockSpec(memory_space=pltpu.VMEM)),
        input_output_aliases={t: 2 + t for t in range(2 * n)},
        compiler_params=pltpu.CompilerParams(has_side_effects=EFFECT),
    )(*[pltpu.with_memory_space_constraint(a, pltpu.HBM) for a in arrs], after)
    return res[0], res[1], list(res[2:2 + n]), list(res[2 + n:2 + 2 * n]), res[-1]


def ici_wait(started, after, exchange, name):
    send_sems, recv_sems, srcs, lands, _ = started
    n = len(srcs)

    def body(*refs):
        for out_copy, in_copy in _peer_copies(exchange, refs[:n], refs[n:2 * n], refs[2 * n], refs[2 * n + 1]):
            out_copy.wait_send()
            in_copy.wait_recv()

    arrs = list(srcs) + list(lands)
    res = pl.pallas_call(
        body, name=name,
        out_shape=tuple(pltpu.HBM(a.shape, a.dtype) for a in arrs),
        in_specs=[HBM] * (2 * n) + [SEM, SEM, ANY], out_specs=tuple([HBM] * (2 * n)),
        input_output_aliases={t: t for t in range(2 * n)},
        compiler_params=pltpu.CompilerParams(has_side_effects=EFFECT),
    )(*arrs, send_sems, recv_sems, after)
    return list(res[:n]), list(res[n:])


def sum_blocks(g, rb, name):
    cols = g.shape[-1]
    rows = g.size // (NSH * cols)
    tr = min(rows, 256)

    def body(g_ref, r1_ref, r2_ref, r3_ref, o_ref):
        acc = g_ref[...].astype(F32) + r1_ref[...].astype(F32) + r2_ref[...].astype(F32) + r3_ref[...].astype(F32)
        o_ref[...] = acc.astype(BF16)

    def mine(i):
        x, y, _ = _mesh_pos()
        return (2 * x + y, i, 0)

    def peer(fx, fy):
        def index(i):
            x, y, c = _mesh_pos()
            return (2 * (2 * (x ^ fx) + (y ^ fy)) + c, i, 0)
        return pl.BlockSpec((None, tr, cols), index)

    return pl.pallas_call(
        body, name=name, grid=(rows // tr,),
        in_specs=[pl.BlockSpec((None, tr, cols), mine), peer(1, 0), peer(0, 1), peer(1, 1)],
        out_specs=pl.BlockSpec((tr, cols), lambda i: (i, 0)),
        out_shape=jax.ShapeDtypeStruct((rows, cols), BF16), compiler_params=_cp(32),
    )(g.reshape(NSH, rows, cols), *[rb.reshape(8, rows, cols)] * 3)


def swap_sibling(ps, name):
    n = len(ps)

    def body(*refs):
        ins, outs, send_sems, recv_sems = refs[:n], refs[n:2 * n], refs[2 * n], refs[2 * n + 1]
        x, y, c = _mesh_pos()
        copies = [pltpu.make_async_remote_copy(
            src_ref=ins[t], dst_ref=outs[t], send_sem=send_sems.at[t], recv_sem=recv_sems.at[t],
            device_id=(x, y, 1 - c), device_id_type=MESH) for t in range(n)]
        for cp in copies:
            cp.start()
        for cp in copies:
            cp.wait_recv()
            cp.wait_send()

    return pl.pallas_call(
        body, name=name, out_shape=[jax.ShapeDtypeStruct(p.shape, p.dtype) for p in ps],
        in_specs=[ANY] * n, out_specs=[ANY] * n,
        scratch_shapes=[pltpu.SemaphoreType.DMA((n,)), pltpu.SemaphoreType.DMA((n,))],
    )(*ps)


def allgather8(v, name):
    m, n = v.shape

    def body(x_ref, out_ref, send_sems, recv_sems, local_sem):
        x, y, c = _mesh_pos()
        me, sibling = (x, y, c), (x, y, 1 - c)
        chips = [(1 - x, y), (x, 1 - y), (1 - x, 1 - y)]

        def rows(px, py, pc):
            return out_ref.at[4 * px + 2 * py + pc]

        def copy(k, block, to, src=None):
            return pltpu.make_async_remote_copy(
                src_ref=rows(*block) if src is None else src, dst_ref=rows(*block),
                send_sem=send_sems.at[k], recv_sem=recv_sems.at[k], device_id=to, device_id_type=MESH)

        mine = pltpu.make_async_copy(x_ref, rows(*me), local_sem)
        mine.start()
        first = [copy(0, me, sibling, src=x_ref)]
        first += [copy(1 + j, me, (*chip, c), src=x_ref) for j, chip in enumerate(chips)]
        for cp in first:
            cp.start()
        passed = [copy(4 + j, (*chip, c), sibling) for j, chip in enumerate(chips)]
        for j, chip in enumerate(chips):
            copy(1 + j, (*chip, c), me).wait_recv()
            passed[j].start()
        copy(0, sibling, me).wait_recv()
        for j, chip in enumerate(chips):
            copy(4 + j, (*chip, 1 - c), me).wait_recv()
        for cp in first + passed:
            cp.wait_send()
        mine.wait()

    return pl.pallas_call(
        body, name=name,
        out_shape=jax.ShapeDtypeStruct((8, m, n), v.dtype),
        in_specs=[pl.BlockSpec(memory_space=pltpu.VMEM)],
        out_specs=pl.BlockSpec(memory_space=pltpu.VMEM),
        scratch_shapes=[pltpu.SemaphoreType.DMA((7,)), pltpu.SemaphoreType.DMA((7,)), pltpu.SemaphoreType.DMA],
        compiler_params=_cp(40),
    )(v)


def _row_spec(width=D, off=0):
    return pl.BlockSpec((TR, width), lambda i, off=off: (i, off))


def _vec_spec(width=D):
    return pl.BlockSpec((1, width), lambda i: (0, 0))


def _mod_spec():
    return pl.BlockSpec((None, 6, D), lambda i: (jnp.minimum(i, 1), 0, 0))


def _pair_spec(width=D):
    return pl.BlockSpec((None, 1, width), lambda i: (jnp.minimum(i, 1), 0, 0))


def _accum(ref, val, first):
    @pl.when(first)
    def _():
        ref[...] = val

    @pl.when(jnp.logical_not(first))
    def _():
        ref[...] += val


def norm_mod(xs, g, mod, si, name):
    r = xs.shape[0]

    def body(x_ref, g_ref, m_ref, o_ref):
        x = x_ref[...]
        rstd = lax.rsqrt(jnp.mean(x * x, axis=-1, keepdims=True) + EPS)
        n = x * rstd * g_ref[...]
        o_ref[...] = (n * (1.0 + m_ref[si + 1:si + 2, :]) + m_ref[si:si + 1, :]).astype(BF16)

    return pl.pallas_call(
        body, name=name, grid=(r // TR,),
        in_specs=[_row_spec(), _vec_spec(), _mod_spec()], out_specs=_row_spec(),
        out_shape=jax.ShapeDtypeStruct((r, D), BF16), compiler_params=_cp(32),
    )(xs, g, mod)


def gate_in(dx, f, mod, gi, name):
    r = dx.shape[0]

    def body(dx_ref, f_ref, m_ref, o_ref, dg_ref):
        i = pl.program_id(0)
        dxv = dx_ref[...]
        o_ref[...] = (dxv * m_ref[gi:gi + 1, :]).astype(BF16)
        _accum(dg_ref, jnp.sum(dxv * f_ref[...].astype(F32), axis=0, keepdims=True), i <= 1)

    return pl.pallas_call(
        body, name=name, grid=(r // TR,),
        in_specs=[_row_spec(), _row_spec(), _mod_spec()], out_specs=[_row_spec(), _pair_spec()],
        out_shape=[jax.ShapeDtypeStruct((r, D), BF16), jax.ShapeDtypeStruct((2, 1, D), F32)],
        compiler_params=_cp(32),
    )(dx, f, mod)


def gate_in_pool(dx, ypre, mod, scale, gi, name):
    r = dx.shape[0]

    def body(dx_ref, y_ref, m_ref, s_ref, o_ref, dg_ref, ds_ref, db_ref):
        i = pl.program_id(0)
        dxv = dx_ref[...]
        yp = y_ref[...].astype(F32)
        sc = s_ref[...]
        dy = dxv * m_ref[gi:gi + 1, :]
        dyp = dy * sc
        o_ref[...] = dyp.astype(BF16)
        _accum(dg_ref, jnp.sum(dxv * (yp * sc), axis=0, keepdims=True), i <= 1)
        _accum(ds_ref, jnp.sum(dy * yp, axis=0, keepdims=True), i == 0)
        _accum(db_ref, jnp.sum(dyp, axis=0, keepdims=True), i == 0)

    return pl.pallas_call(
        body, name=name, grid=(r // TR,),
        in_specs=[_row_spec(), _row_spec(), _mod_spec(), _vec_spec()],
        out_specs=[_row_spec(), _pair_spec(), _vec_spec(), _vec_spec()],
        out_shape=[jax.ShapeDtypeStruct((r, D), BF16), jax.ShapeDtypeStruct((2, 1, D), F32),
                   jax.ShapeDtypeStruct((1, D), F32), jax.ShapeDtypeStruct((1, D), F32)],
        compiler_params=_cp(32),
    )(dx, ypre, mod, scale)


def normmod_bwd(dh, x, dxo, g, mod, si, name):
    r = x.shape[0]

    def body(dh_ref, x_ref, dxo_ref, g_ref, m_ref, dx_ref, dsh_ref, dsc_ref, dg_ref):
        i = pl.program_id(0)
        xv = x_ref[...]
        dhv = dh_ref[...]
        gv = g_ref[...]
        rstd = lax.rsqrt(jnp.mean(xv * xv, axis=-1, keepdims=True) + EPS)
        xhat = xv * rstd
        dn = dhv * (1.0 + m_ref[si + 1:si + 2, :])
        dxh = dn * gv
        dx_ref[...] = rstd * (dxh - xhat * jnp.mean(dxh * xhat, axis=-1, keepdims=True)) + dxo_ref[...]
        _accum(dsh_ref, jnp.sum(dhv, axis=0, keepdims=True), i <= 1)
        _accum(dsc_ref, jnp.sum(dhv * (xhat * gv), axis=0, keepdims=True), i <= 1)
        _accum(dg_ref, jnp.sum(dn * xhat, axis=0, keepdims=True), i == 0)

    return pl.pallas_call(
        body, name=name, grid=(r // TR,),
        in_specs=[_row_spec(), _row_spec(), _row_spec(), _vec_spec(), _mod_spec()],
        out_specs=[_row_spec(), _pair_spec(), _pair_spec(), _vec_spec()],
        out_shape=[jax.ShapeDtypeStruct((r, D), F32), jax.ShapeDtypeStruct((2, 1, D), F32),
                   jax.ShapeDtypeStruct((2, 1, D), F32), jax.ShapeDtypeStruct((1, D), F32)],
        compiler_params=_cp(48),
    )(dh, x, dxo, g, mod)


def final_loss(xs, g, target, name):
    r = xs.shape[0]

    def body(x_ref, g_ref, t_ref, dx_ref, loss_ref, dg_ref):
        i = pl.program_id(0)

        @pl.when(i == 0)
        def _():
            dx_ref[...] = jnp.zeros_like(dx_ref)
            loss_ref[...] = jnp.zeros_like(loss_ref)
            dg_ref[...] = jnp.zeros_like(dg_ref)

        @pl.when(i > 0)
        def _():
            xv = x_ref[...]
            gv = g_ref[...]
            rstd = lax.rsqrt(jnp.mean(xv * xv, axis=-1, keepdims=True) + EPS)
            xhat = xv * rstd
            err = xhat * gv - t_ref[...]
            part = 0.5 * jnp.sum(jnp.mean(err * err, axis=-1, keepdims=True), axis=0, keepdims=True)
            lane = lax.broadcasted_iota(jnp.int32, (1, 128), 1)
            loss_ref[...] += jnp.where(lane == 0, part, 0.0)
            dy = err * (1.0 / D)
            dg_ref[...] += jnp.sum(dy * xhat, axis=0, keepdims=True)
            dxh = dy * gv
            dx_ref[...] = rstd * (dxh - xhat * jnp.mean(dxh * xhat, axis=-1, keepdims=True))

    return pl.pallas_call(
        body, name=name, grid=(r // TR,),
        in_specs=[_row_spec(), _vec_spec(), pl.BlockSpec((TR, D), lambda i: (jnp.maximum(i - 1, 0), 0))],
        out_specs=[_row_spec(), pl.BlockSpec((1, 128), lambda i: (0, 0)), _vec_spec()],
        out_shape=[jax.ShapeDtypeStruct((r, D), F32), jax.ShapeDtypeStruct((1, 128), F32),
                   jax.ShapeDtypeStruct((1, D), F32)],
        compiler_params=_cp(32),
    )(xs, g, target)


def _mm(name, mode, a, b, grid, a_spec, b_spec, out_shapes, out_specs, epi, kaxis=None, acc=None,
        extras=(), merge_b=False, vmem=48):
    ne, no = len(extras), len(out_shapes)
    dot2 = {"nn": _nn, "nt": _nt, "tn": _tn}[mode]
    nk = grid[kaxis] if kaxis is not None else 1

    def dot(av, bv):
        return dot2(av, bv.reshape(-1, bv.shape[-1]) if merge_b else bv)

    def body(*refs):
        a_ref, b_ref = refs[0], refs[1]
        ex = refs[2:2 + ne]
        outs = refs[2 + ne:2 + ne + no]
        ids = [pl.program_id(ax) for ax in range(len(grid))]
        if kaxis is None:
            epi(dot(a_ref[...], b_ref[...]), ids, ex, outs)
        else:
            acc_ref = refs[-1]
            k = ids[kaxis]

            @pl.when(k == 0)
            def _():
                acc_ref[...] = jnp.zeros_like(acc_ref)

            acc_ref[...] += dot(a_ref[...], b_ref[...])

            @pl.when(k == nk - 1)
            def _():
                epi(acc_ref[...], ids, ex, outs)

    operands = [a, b] + [e[0] for e in extras]
    in_specs = [a_spec, b_spec] + [e[1] for e in extras]
    return pl.pallas_call(
        body, name=name, grid=grid, in_specs=in_specs, out_specs=out_specs, out_shape=out_shapes,
        scratch_shapes=[] if kaxis is None else [pltpu.VMEM(acc, F32)], compiler_params=_cp(vmem),
    )(*operands)


def _epi_store(acc, ids, ex, outs):
    outs[0][...] = acc.astype(outs[0].dtype)


def _epi_relu2(acc, ids, ex, outs):
    rl = jnp.maximum(acc, 0.0)
    outs[0][...] = (rl * rl).astype(BF16)


def _epi_2sqrt(acc, ids, ex, outs):
    outs[0][...] = (acc * (2.0 * jnp.sqrt(ex[0][...].astype(F32)))).astype(BF16)


def _gate_rows(ids, tm, shape, mod_ref, gi):
    rid = ids[0] * tm + lax.broadcasted_iota(jnp.int32, shape, 0)
    return jnp.where(rid < CTX, mod_ref[0, gi:gi + 1, :], mod_ref[1, gi:gi + 1, :])


def _epi_res(gi, tm):
    def epi(acc, ids, ex, outs):
        outs[0][...] = ex[0][...] + _gate_rows(ids, tm, acc.shape, ex[1], gi) * acc
        outs[1][...] = acc.astype(BF16)
    return epi


def _epi_pool(gi, tm):
    def epi(acc, ids, ex, outs):
        ypre = acc + ex[2][...]
        outs[0][...] = ex[0][...] + _gate_rows(ids, tm, acc.shape, ex[1], gi) * (ypre * ex[3][...])
        outs[1][...] = ypre.astype(BF16)
    return epi


def _tm(r):
    return 768 if r % 768 == 0 else TR


def _tm_wide(r):
    return 1408 if r % 1408 == 0 else _tm(r)


def mm_cols(a, w, l, name, epi=_epi_store, out_dtype=F32, per=2):
    r, k = a.shape
    c = w.shape[3]
    tm, tn = _tm_wide(r), c // per
    return _mm(name, "nn", a, w, (r // tm, NSH * per),
               pl.BlockSpec((tm, k), lambda i, j: (i, 0)),
               pl.BlockSpec((None, None, k, tn), lambda i, j: (j // per, l, 0, j % per)),
               [jax.ShapeDtypeStruct((r, NSH * c), out_dtype)], [pl.BlockSpec((tm, tn), lambda i, j: (i, j))],
               epi, vmem=56)[0]


def mm_rows_res(a, w, l, res, mod, gi, name):
    r = a.shape[0]
    kc = w.shape[2]
    tm, tn = (1056 if r % 1056 == 0 else _tm(r)), 1024
    return _mm(name, "nn", a, w, (r // tm, D // tn, NSH),
               pl.BlockSpec((tm, kc), lambda i, j, k: (i, k)),
               pl.BlockSpec((None, None, kc, tn), lambda i, j, k: (k, l, 0, j)),
               [jax.ShapeDtypeStruct((r, D), F32), jax.ShapeDtypeStruct((r, D), BF16)],
               [pl.BlockSpec((tm, tn), lambda i, j, k: (i, j))] * 2,
               _epi_res(gi, tm), kaxis=2, acc=(tm, tn),
               extras=[(res, pl.BlockSpec((tm, tn), lambda i, j, k: (i, j))),
                       (mod, pl.BlockSpec((2, 6, tn), lambda i, j, k: (0, 0, j)))], vmem=56)


def mm_rows(a, w, l, name):
    r = a.shape[0]
    kc = w.shape[2]
    tm, tn = _tm(r), 1024
    return _mm(name, "nn", a, w, (r // tm, D // tn, NSH),
               pl.BlockSpec((tm, kc), lambda i, j, k: (i, k)),
               pl.BlockSpec((None, None, kc, tn), lambda i, j, k: (k, l, 0, j)),
               [jax.ShapeDtypeStruct((r, D), F32)], [pl.BlockSpec((tm, tn), lambda i, j, k: (i, j))],
               _epi_store, kaxis=2, acc=(tm, tn), vmem=56)[0]


def mm_grp_res(z, w, o, bias, scale, res, mod, gi, name):
    r = z.shape[0]
    tm = _tm(r)
    return _mm(name, "nn", z, w, (r // tm, 4),
               pl.BlockSpec((tm, 512), lambda i, g: (i, g)), _grp_spec(o),
               [jax.ShapeDtypeStruct((r, D), F32), jax.ShapeDtypeStruct((r, D), BF16)],
               [pl.BlockSpec((tm, 512), lambda i, g: (i, g))] * 2,
               _epi_pool(gi, tm), merge_b=True,
               extras=[(res, pl.BlockSpec((tm, 512), lambda i, g: (i, g))),
                       (mod, pl.BlockSpec((2, 6, 512), lambda i, g: (0, 0, g))),
                       (bias, pl.BlockSpec((1, 512), lambda i, g: (0, g))),
                       (scale, pl.BlockSpec((1, 512), lambda i, g: (0, g)))], vmem=48)


def _grp_spec(o):
    return pl.BlockSpec((NSH, None, None, 128, 512), lambda i, g: (0, o, g, 0, 0))


def mm_t_rows(a, w, l, name, epi=_epi_store, out_dtype=F32, extras_of=None, per=1):
    r, n = a.shape
    kc = w.shape[2]
    tm, tn = _tm_wide(r), kc // per
    extras = []
    if extras_of is not None:
        extras = [(extras_of, pl.BlockSpec((tm, tn), lambda i, j: (i, j)))]
    return _mm(name, "nt", a, w, (r // tm, NSH * per),
               pl.BlockSpec((tm, n), lambda i, j: (i, 0)),
               pl.BlockSpec((None, None, tn, n), lambda i, j: (j // per, l, j % per, 0)),
               [jax.ShapeDtypeStruct((r, NSH * kc), out_dtype)], [pl.BlockSpec((tm, tn), lambda i, j: (i, j))],
               epi, extras=extras, vmem=56)[0]


def mm_t_cols(a, w, l, name):
    r = a.shape[0]
    k, c = w.shape[2], w.shape[3]
    tm, tn = _tm_wide(r), 1024
    return _mm(name, "nt", a, w, (r // tm, k // tn, NSH),
               pl.BlockSpec((tm, c), lambda i, j, s: (i, s)),
               pl.BlockSpec((None, None, tn, c), lambda i, j, s: (s, l, j, 0)),
               [jax.ShapeDtypeStruct((r, k), F32)], [pl.BlockSpec((tm, tn), lambda i, j, s: (i, j))],
               _epi_store, kaxis=2, acc=(tm, tn), vmem=56)[0]


def mm_t_grp(dy, w, o, name):
    r = dy.shape[0]
    tm = _tm(r)
    return _mm(name, "nt", dy, w, (r // tm, 4),
               pl.BlockSpec((tm, 512), lambda i, g: (i, g)), _grp_spec(o),
               [jax.ShapeDtypeStruct((r, D), F32)], [pl.BlockSpec((tm, 512), lambda i, g: (i, g))],
               _epi_store, merge_b=True, vmem=40)[0]


def grad_cols(a, b, name, ta=1024, per=2):
    r, k = a.shape
    c = b.shape[1] // NSH
    tk, tn = _tm(r), c // per
    return _mm(name, "tn", a, b, (NSH, k // ta, per, r // tk),
               pl.BlockSpec((tk, ta), lambda s, i, j, t: (t, i)),
               pl.BlockSpec((tk, tn), lambda s, i, j, t: (t, s * per + j)),
               [jax.ShapeDtypeStruct((NSH, 1, k, c), BF16)],
               [pl.BlockSpec((None, None, ta, tn), lambda s, i, j, t: (s, 0, i, j))],
               _epi_store, kaxis=3, acc=(ta, tn), vmem=56)[0]


def grad_rows(a, b, name, per=1, tn=1024):
    r = a.shape[0]
    kc, n = a.shape[1] // NSH, b.shape[1]
    tk, ta = _tm(r), kc // per
    return _mm(name, "tn", a, b, (NSH, per, n // tn, r // tk),
               pl.BlockSpec((tk, ta), lambda s, i, j, t: (t, s * per + i)),
               pl.BlockSpec((tk, tn), lambda s, i, j, t: (t, j)),
               [jax.ShapeDtypeStruct((NSH, 1, kc, n), BF16)],
               [pl.BlockSpec((None, None, ta, tn), lambda s, i, j, t: (s, 0, i, j))],
               _epi_store, kaxis=3, acc=(ta, tn), vmem=56)[0]


def grad_grp(z, dy, name):
    r = z.shape[0]
    tk = _tm(r)

    def epi(acc, ids, ex, outs):
        for s in range(NSH):
            outs[0][s] = acc[s * 128:(s + 1) * 128].astype(BF16)

    return _mm(name, "tn", z, dy, (4, r // tk),
               pl.BlockSpec((tk, 512), lambda g, t: (t, g)), pl.BlockSpec((tk, 512), lambda g, t: (t, g)),
               [jax.ShapeDtypeStruct((NSH, 1, 4, 128, 512), BF16)],
               [pl.BlockSpec((NSH, None, None, 128, 512), lambda g, t: (0, 0, g, 0, 0))],
               epi, kaxis=1, acc=(512, 512), vmem=40)[0]


def _scan_tile(reverse, nt):
    if reverse:
        return lambda p: jnp.where(p == 0, 0, nt - p)
    return lambda p: p


def _gates(f, lbv):
    sg = jax.nn.sigmoid(f)
    fg = lbv + (1.0 - lbv) * sg
    g = jnp.log(jnp.maximum(fg, LOG_FLOOR))
    kk = (1.0 - lbv) * jax.nn.sigmoid(-f)
    return sg, fg, g, kk


def _chunk_cumsum(g, reverse, sub):
    n = g.shape[0]
    rr = lax.broadcasted_iota(jnp.int32, (n, n), 0)
    cc = lax.broadcasted_iota(jnp.int32, (n, n), 1)
    inside = (rr // sub) == (cc // sub)
    tri = jnp.where(inside & ((cc >= rr) if reverse else (cc <= rr)), 1.0, 0.0).astype(F32)
    return jnp.dot(tri, g, precision=HIGHEST, preferred_element_type=F32)


def _decay(b, s, rows, reverse):
    return jnp.where((rows <= s) if reverse else (rows >= s), jnp.exp(b - b[s:s + 1]), 0.0)


def hgrn_fwd(p, lb, reverse, name):
    SUB = SUB_FWD
    r = p.shape[0]
    nt = r // TR
    nsub = TR // SUB
    fcol = 2 if reverse else 1
    tile = _scan_tile(reverse, nt)

    def body(q_ref, f_ref, v_ref, lb_ref, o_ref, sin_ref, st, k_s, b_s):
        i = pl.program_id(0)

        @pl.when(i == 0)
        def _():
            st[...] = jnp.zeros_like(st)

        sin_ref[...] = st[...]
        _, _, g, kk = _gates(f_ref[...], lb_ref[...])
        k_s[...] = kk
        b_s[...] = _chunk_cumsum(g, reverse, SUB)
        rows = lax.broadcasted_iota(jnp.int32, (SUB, DK), 0)

        def sub(jj, carry):
            j = (nsub - 1 - jj) if reverse else jj
            rs = pl.ds(pl.multiple_of(j * SUB, SUB), SUB)
            for h in range(NH):
                sl = slice(h * DK, (h + 1) * DK)
                q, k, b, v = q_ref[rs, sl], k_s[rs, sl], b_s[rs, sl], v_ref[rs, sl]
                btot = b[0:1] if reverse else b[SUB - 1:SUB]
                o = _nt(_bf(q * jnp.exp(b)), _bf(st[h]))
                for s in range(SUB):
                    col = jnp.sum(q * k[s:s + 1] * _decay(b, s, rows, reverse), axis=-1, keepdims=True)
                    o = o + col * v[s:s + 1]
                o_ref[rs, sl] = o
                st[h] = st[h] * jnp.exp(btot) + _tn(_bf(v), _bf(k * jnp.exp(btot - b)))
            return carry

        lax.fori_loop(0, nsub, sub, 0)

    seg = lambda col: pl.BlockSpec((TR, AW), lambda i, col=col: (tile(i), col))
    return pl.pallas_call(
        body, name=name, grid=(nt,),
        in_specs=[seg(0), seg(fcol), seg(3), _vec_spec(AW)],
        out_specs=[pl.BlockSpec((TR, AW), lambda i: (tile(i), 0)),
                   pl.BlockSpec((None, NH, DK, DK), lambda i: (tile(i), 0, 0, 0))],
        out_shape=[jax.ShapeDtypeStruct((r, AW), F32), jax.ShapeDtypeStruct((nt, NH, DK, DK), F32)],
        scratch_shapes=[pltpu.VMEM((NH, DK, DK), F32), pltpu.VMEM((TR, AW), F32), pltpu.VMEM((TR, AW), F32)],
        compiler_params=_cp(40),
    )(p, p, p, lb)


def hgrn_bwd(p, do, sin, lb, reverse, name, add=None):
    SUB = SUB_BWD
    r = p.shape[0]
    nt = r // TR
    nsub = TR // SUB
    fcol = 2 if reverse else 1
    tile0 = _scan_tile(reverse, nt)
    tile = lambda i: tile0(nt - 1 - i)
    nadd = 0 if add is None else 2
    out_dt = F32 if add is None else BF16

    def body(*refs):
        q_ref, f_ref, v_ref, do_ref, sin_ref, lb_ref = refs[:6]
        adds = refs[6:6 + nadd]
        dq_ref, dv_ref, df_ref, dlb_ref = refs[6 + nadd:10 + nadd]
        dst, srun, ssub, k_s, b_s, sg_s, fg_s = refs[10 + nadd:]
        i = pl.program_id(0)

        @pl.when(i == 0)
        def _():
            dst[...] = jnp.zeros_like(dst)
            dlb_ref[...] = jnp.zeros_like(dlb_ref)

        lbv = lb_ref[...]
        sg, fg, g, kk = _gates(f_ref[...], lbv)
        k_s[...] = kk
        sg_s[...] = sg
        fg_s[...] = fg
        b_s[...] = _chunk_cumsum(g, reverse, SUB)
        srun[...] = sin_ref[...]
        rows = lax.broadcasted_iota(jnp.int32, (SUB, DK), 0)
        r16 = lax.broadcasted_iota(jnp.int32, (SUB, SUB), 0)
        c16 = lax.broadcasted_iota(jnp.int32, (SUB, SUB), 1)
        later = jnp.where((c16 <= r16) if reverse else (c16 >= r16), 1.0, 0.0).astype(F32)

        def recompute(jj, c):
            j = (nsub - 1 - jj) if reverse else jj
            rs = pl.ds(pl.multiple_of(j * SUB, SUB), SUB)
            for h in range(NH):
                sl = slice(h * DK, (h + 1) * DK)
                k, b, v = k_s[rs, sl], b_s[rs, sl], v_ref[rs, sl]
                btot = b[0:1] if reverse else b[SUB - 1:SUB]
                ssub[jj, h] = srun[h]
                srun[h] = srun[h] * jnp.exp(btot) + _tn(_bf(v), _bf(k * jnp.exp(btot - b)))
            return c

        lax.fori_loop(0, nsub, recompute, 0)
        for h in range(NH):
            ssub[nsub, h] = srun[h]

        def back(jj, c):
            pos = nsub - 1 - jj
            j = jj if reverse else pos
            rs = pl.ds(pl.multiple_of(j * SUB, SUB), SUB)
            for h in range(NH):
                sl = slice(h * DK, (h + 1) * DK)
                q, k, b, v, dov = q_ref[rs, sl], k_s[rs, sl], b_s[rs, sl], v_ref[rs, sl], do_ref[rs, sl]
                btot = b[0:1] if reverse else b[SUB - 1:SUB]
                s0 = ssub[pos, h]
                ds = dst[h]
                dg_next = jnp.sum(ds * ssub[pos + 1, h], axis=0, keepdims=True)
                eb = jnp.exp(b)
                ebt = jnp.exp(btot - b)
                ke = k * ebt
                dq = _nn(_bf(dov), _bf(s0)) * eb
                dk = _nn(_bf(v), _bf(ds)) * ebt
                dv = _nt(_bf(ke), _bf(ds))
                for s in range(SUB):
                    dec = _decay(b, s, rows, reverse)
                    dsc = jnp.sum(dov * v[s:s + 1], axis=-1, keepdims=True)
                    qd = q * dec
                    dq = dq + (dsc * dec) * k[s:s + 1]
                    dk_row = jnp.sum(dsc * qd, axis=0, keepdims=True)
                    sc = jnp.sum(qd * k[s:s + 1], axis=-1, keepdims=True)
                    dv_row = jnp.sum(sc * dov, axis=0, keepdims=True)
                    dk = dk + jnp.where(rows == s, dk_row, 0.0)
                    dv = dv + jnp.where(rows == s, dv_row, 0.0)
                dst[h] = ds * jnp.exp(btot) + _tn(_bf(dov), _bf(q * eb))
                dg = jnp.dot(later, q * dq - k * dk, precision=HIGHEST, preferred_element_type=F32) + dg_next
                sgv, fgv, lbh = sg_s[rs, sl], fg_s[rs, sl], lbv[:, sl]
                dfg = jnp.where(fgv > LOG_FLOOR, dg / fgv, 0.0)
                df_ref[rs, sl] = ((1.0 - lbh) * sgv * (1.0 - sgv) * (dfg - dk)).astype(BF16)
                dlb_ref[:, sl] += jnp.sum((dfg - dk) * (1.0 - sgv), axis=0, keepdims=True)
                if add is None:
                    dq_ref[rs, sl] = dq
                    dv_ref[rs, sl] = dv
                else:
                    dq_ref[rs, sl] = (dq + adds[0][rs, sl]).astype(BF16)
                    dv_ref[rs, sl] = (dv + adds[1][rs, sl]).astype(BF16)
            return c

        lax.fori_loop(0, nsub, back, 0)

    seg = lambda col: pl.BlockSpec((TR, AW), lambda i, col=col: (tile(i), col))
    plain = pl.BlockSpec((TR, AW), lambda i: (tile(i), 0))
    operands = [p, p, p, do, sin, lb]
    in_specs = [seg(0), seg(fcol), seg(3), plain,
                pl.BlockSpec((None, NH, DK, DK), lambda i: (tile(i), 0, 0, 0)), _vec_spec(AW)]
    if add is not None:
        operands += list(add)
        in_specs += [plain, plain]
    return pl.pallas_call(
        body, name=name, grid=(nt,), in_specs=in_specs,
        out_specs=[plain, plain, plain, _vec_spec(AW)],
        out_shape=[jax.ShapeDtypeStruct((r, AW), out_dt), jax.ShapeDtypeStruct((r, AW), out_dt),
                   jax.ShapeDtypeStruct((r, AW), BF16), jax.ShapeDtypeStruct((1, AW), F32)],
        scratch_shapes=[pltpu.VMEM((NH, DK, DK), F32), pltpu.VMEM((NH, DK, DK), F32),
                        pltpu.VMEM((nsub + 1, NH, DK, DK), F32)]
        + [pltpu.VMEM((TR, AW), F32)] * 4,
        compiler_params=_cp(56),
    )(*operands)


def _silu(v):
    return v * jax.nn.sigmoid(v)


def readout_fwd(of, ob, p, ng, name):
    r = of.shape[0]

    def body(of_ref, ob_ref, g_ref, ng_ref, y_ref):
        for h in range(NH):
            sl = slice(h * DK, (h + 1) * DK)
            o = of_ref[:, sl] + ob_ref[:, sl]
            on = o * lax.rsqrt(jnp.mean(o * o, axis=-1, keepdims=True) + EPS) * ng_ref[:, sl]
            y_ref[:, sl] = (on * _silu(g_ref[:, sl])).astype(BF16)

    return pl.pallas_call(
        body, name=name, grid=(r // TR,),
        in_specs=[_row_spec(AW), _row_spec(AW), _row_spec(AW, 4), _vec_spec(AW)], out_specs=_row_spec(AW),
        out_shape=jax.ShapeDtypeStruct((r, AW), BF16), compiler_params=_cp(32),
    )(of, ob, p, ng)


def readout_bwd(dy, of, ob, p, ng, name):
    r = of.shape[0]

    def body(dy_ref, of_ref, ob_ref, g_ref, ng_ref, do_ref, dg_ref, dn_ref):
        i = pl.program_id(0)

        @pl.when(i == 0)
        def _():
            dn_ref[...] = jnp.zeros_like(dn_ref)

        for h in range(NH):
            sl = slice(h * DK, (h + 1) * DK)
            o = of_ref[:, sl] + ob_ref[:, sl]
            rstd = lax.rsqrt(jnp.mean(o * o, axis=-1, keepdims=True) + EPS)
            oh = o * rstd
            gv = g_ref[:, sl]
            sig = jax.nn.sigmoid(gv)
            dyv = dy_ref[:, sl]
            don = dyv * (gv * sig)
            dg_ref[:, sl] = (dyv * (oh * ng_ref[:, sl]) * (sig * (1.0 + gv * (1.0 - sig)))).astype(BF16)
            dn_ref[:, sl] += jnp.sum(don * oh, axis=0, keepdims=True)
            doh = don * ng_ref[:, sl]
            do_ref[:, sl] = rstd * (doh - oh * jnp.mean(doh * oh, axis=-1, keepdims=True))

    return pl.pallas_call(
        body, name=name, grid=(r // TR,),
        in_specs=[_row_spec(AW), _row_spec(AW), _row_spec(AW), _row_spec(AW, 4), _vec_spec(AW)],
        out_specs=[_row_spec(AW), _row_spec(AW), _vec_spec(AW)],
        out_shape=[jax.ShapeDtypeStruct((r, AW), F32), jax.ShapeDtypeStruct((r, AW), BF16),
                   jax.ShapeDtypeStruct((1, AW), F32)],
        compiler_params=_cp(32),
    )(dy, of, ob, p, ng)


def _gelu(v):
    return 0.5 * v * (1.0 + lax.erf(v * 0.7071067811865476))


def _gelu_grad(v):
    return 0.5 * (1.0 + lax.erf(v * 0.7071067811865476)) + v * (0.3989422804014327 * jnp.exp(-0.5 * v * v))


def _cmlp_norm(vv, gn):
    vg = _gelu(vv)
    mu = jnp.mean(vg, axis=-1, keepdims=True)
    cen = vg - mu
    rstd = lax.rsqrt(jnp.mean(cen * cen, axis=-1, keepdims=True) + EPS)
    xhat = cen * rstd
    return xhat, rstd, xhat * gn


def chunkmlp_fwd(p, ws, bias, gn, name):
    r = p.shape[0]

    def body(u_ref, v_ref, ws_ref, b_ref, gn_ref, y_ref):
        for ci in range(TR // B_CHUNK):
            rs = slice(ci * B_CHUNK, (ci + 1) * B_CHUNK)
            for gidx in range(NH):
                sl = slice(gidx * DK, (gidx + 1) * DK)
                _, _, vn = _cmlp_norm(v_ref[rs, sl], gn_ref[:, sl])
                mixed = _nn(_bf(ws_ref[gidx]), _bf(vn)) + b_ref[gidx]
                y_ref[rs, sl] = (_gelu(u_ref[rs, sl]) * mixed).astype(BF16)

    return pl.pallas_call(
        body, name=name, grid=(r // TR,),
        in_specs=[_row_spec(AW, 5), _row_spec(AW, 6), pl.BlockSpec((NH, B_CHUNK, B_CHUNK), lambda i: (0, 0, 0)),
                  pl.BlockSpec((NH, B_CHUNK, 1), lambda i: (0, 0, 0)), _vec_spec(AW)],
        out_specs=_row_spec(AW), out_shape=jax.ShapeDtypeStruct((r, AW), BF16), compiler_params=_cp(32),
    )(p, p, ws, bias, gn)


def chunkmlp_bwd(dy, p, ws, bias, gn, name):
    r = p.shape[0]

    def body(dy_ref, u_ref, v_ref, ws_ref, b_ref, gn_ref, du_ref, dv_ref, dws_ref, db_ref, dgn_ref):
        i = pl.program_id(0)

        @pl.when(i == 0)
        def _():
            dws_ref[...] = jnp.zeros_like(dws_ref)
            db_ref[...] = jnp.zeros_like(db_ref)
            dgn_ref[...] = jnp.zeros_like(dgn_ref)

        for ci in range(TR // B_CHUNK):
            rs = slice(ci * B_CHUNK, (ci + 1) * B_CHUNK)
            for gidx in range(NH):
                sl = slice(gidx * DK, (gidx + 1) * DK)
                vv, uv, dyv, gnv = v_ref[rs, sl], u_ref[rs, sl], dy_ref[rs, sl], gn_ref[:, sl]
                xhat, rstd, vn = _cmlp_norm(vv, gnv)
                wg = _bf(ws_ref[gidx])
                mixed = _nn(wg, _bf(vn)) + b_ref[gidx]
                dmixed = dyv * _gelu(uv)
                du_ref[rs, sl] = (dyv * mixed * _gelu_grad(uv)).astype(BF16)
                dws_ref[gidx] += _nt(_bf(dmixed), _bf(vn))
                db_ref[gidx] += jnp.sum(dmixed, axis=-1, keepdims=True)
                dvn = _tn(wg, _bf(dmixed))
                dgn_ref[:, sl] += jnp.sum(dvn * xhat, axis=0, keepdims=True)
                dxh = dvn * gnv
                dvg = rstd * (dxh - jnp.mean(dxh, axis=-1, keepdims=True)
                              - xhat * jnp.mean(dxh * xhat, axis=-1, keepdims=True))
                dv_ref[rs, sl] = (dvg * _gelu_grad(vv)).astype(BF16)

    return pl.pallas_call(
        body, name=name, grid=(r // TR,),
        in_specs=[_row_spec(AW, 1), _row_spec(AW, 5), _row_spec(AW, 6),
                  pl.BlockSpec((NH, B_CHUNK, B_CHUNK), lambda i: (0, 0, 0)),
                  pl.BlockSpec((NH, B_CHUNK, 1), lambda i: (0, 0, 0)), _vec_spec(AW)],
        out_specs=[_row_spec(AW), _row_spec(AW), pl.BlockSpec((NH, B_CHUNK, B_CHUNK), lambda i: (0, 0, 0)),
                   pl.BlockSpec((NH, B_CHUNK, 1), lambda i: (0, 0, 0)), _vec_spec(AW)],
        out_shape=[jax.ShapeDtypeStruct((r, AW), BF16), jax.ShapeDtypeStruct((r, AW), BF16),
                   jax.ShapeDtypeStruct((NH, B_CHUNK, B_CHUNK), F32), jax.ShapeDtypeStruct((NH, B_CHUNK, 1), F32),
                   jax.ShapeDtypeStruct((1, AW), F32)],
        compiler_params=_cp(32),
    )(dy, p, p, ws, bias, gn)


def _win_count(pos, k, n):
    lo = jnp.maximum(pos - k // 2, 0)
    hi = jnp.minimum(pos - k // 2 + k, n)
    return (hi - lo).astype(F32)


POOL_CW = 256
POOL_PAD = (POOL_WINDOWS[-1] // 2) * GRID_W


def pool_op(p, transpose, name):
    r = p.shape[0]
    seq = r - CTX
    grows = seq // GRID_W
    nt = seq // TR

    def body(x_ref, o_ref, y_s):
        j = pl.program_id(0)

        @pl.when(j == 0)
        def _():
            y_s[:POOL_PAD, :] = jnp.zeros((POOL_PAD, POOL_CW), F32)
            y_s[POOL_PAD + seq:, :] = jnp.zeros((POOL_PAD, POOL_CW), F32)

        for gi, k in enumerate(POOL_WINDOWS):
            @pl.when(j // (512 // POOL_CW) == gi)
            def _(k=k):
                offs = list(range(-(k // 2) + 1, k // 2 + 1) if transpose else range(-(k // 2), k // 2))
                tt = lax.broadcasted_iota(jnp.int32, (TR, TR), 0)
                ss = lax.broadcasted_iota(jnp.int32, (TR, TR), 1)
                band = (ss - tt >= offs[0]) & (ss - tt <= offs[-1])
                b_ctx = jnp.where(band, 1.0, 0.0).astype(F32)
                b_grid = jnp.where(band & ((tt >> 6) == (ss >> 6)), 1.0, 0.0).astype(F32)
                trow = lax.broadcasted_iota(jnp.int32, (TR, POOL_CW), 0)

                def count(i):
                    t = i * TR + trow
                    return _win_count(t & (GRID_W - 1), k, GRID_W) * _win_count(t >> 6, k, grows)

                def col_pass(i, carry):
                    xt = x_ref[pl.ds(pl.multiple_of(CTX + i * TR, TR), TR), :]
                    if transpose:
                        xt = xt / count(i)
                    y_s[pl.ds(pl.multiple_of(POOL_PAD + i * TR, TR), TR), :] = jnp.dot(
                        b_grid, xt, precision=HIGHEST, preferred_element_type=F32)
                    return carry

                lax.fori_loop(0, nt, col_pass, 0)

                def row_pass(i, carry):
                    base = POOL_PAD + i * TR
                    acc = y_s[pl.ds(pl.multiple_of(base + offs[0] * GRID_W, GRID_W), TR), :]
                    for d in offs[1:]:
                        acc = acc + y_s[pl.ds(pl.multiple_of(base + d * GRID_W, GRID_W), TR), :]
                    rows = pl.ds(pl.multiple_of(CTX + i * TR, TR), TR)
                    if not transpose:
                        acc = acc / count(i)
                    o_ref[rows, :] = (acc - x_ref[rows, :]).astype(BF16)
                    return carry

                lax.fori_loop(0, nt, row_pass, 0)

                cx = x_ref[:CTX, :]
                cntc = _win_count(trow, k, CTX)
                accc = jnp.dot(b_ctx, cx / cntc if transpose else cx, precision=HIGHEST, preferred_element_type=F32)
                o_ref[:CTX, :] = ((accc if transpose else accc / cntc) - cx).astype(BF16)

    spec = pl.BlockSpec((r, POOL_CW), lambda j: (0, j))
    return pl.pallas_call(
        body, name=name, grid=(D // POOL_CW,), in_specs=[spec], out_specs=spec,
        out_shape=jax.ShapeDtypeStruct((r, D), BF16),
        scratch_shapes=[pltpu.VMEM((seq + 2 * POOL_PAD, POOL_CW), F32)], compiler_params=_cp(56),
    )(p)


def ada_mods(cs, w_ada, b_loc, name):
    nl, _, cl = w_ada.shape
    tn = 1024

    def body(c_ref, w_ref, b_ref, o_ref):
        o_ref[...] = _nn(_bf(_silu(c_ref[...])), _bf(w_ref[...])) + b_ref[...]

    return pl.pallas_call(
        body, name=name, grid=(nl, cl // tn),
        in_specs=[pl.BlockSpec((16, D), lambda l, j: (0, 0)), pl.BlockSpec((None, D, tn), lambda l, j: (l, 0, j)),
                  pl.BlockSpec((None, 1, tn), lambda l, j: (l, 0, j))],
        out_specs=pl.BlockSpec((None, 16, tn), lambda l, j: (l, 0, j)),
        out_shape=jax.ShapeDtypeStruct((nl, 16, cl), F32), compiler_params=_cp(40),
    )(cs, w_ada, b_loc)


def _adamw(w, g, m, v):
    m = ADAM_B1 * m + (1.0 - ADAM_B1) * g
    v = ADAM_B2 * v + (1.0 - ADAM_B2) * (g * g)
    m_hat = m / (1.0 - ADAM_B1 ** ADAM_STEP)
    v_hat = v / (1.0 - ADAM_B2 ** ADAM_STEP)
    delta = -ADAM_LR * (m_hat / (jnp.sqrt(v_hat) + ADAM_EPS) + ADAM_WD * w)
    return delta, m, v


def ada_update(cs, dm, w, m, v, name):
    nl, _, cl = w.shape
    ta, tn = 256, 1024

    def body(c_ref, dm_ref, w_ref, m_ref, v_ref, g_ref, d_ref, nm_ref, nv_ref, dc_ref):
        l, j = pl.program_id(1), pl.program_id(2)
        a = _bf(_silu(c_ref[...]))
        bmat = _bf(dm_ref[...])
        wv = w_ref[...]
        g = _tn(a, bmat)
        g_ref[...] = g
        d_ref[...], nm_ref[...], nv_ref[...] = _adamw(wv, g, m_ref[...], v_ref[...])
        _accum(dc_ref, _nt(bmat, _bf(wv)), (l == 0) & (j == 0))

    wspec = pl.BlockSpec((None, ta, tn), lambda i, l, j: (l, i, j))
    shp = jax.ShapeDtypeStruct(w.shape, F32)
    return pl.pallas_call(
        body, name=name, grid=(D // ta, nl, cl // tn),
        in_specs=[pl.BlockSpec((16, ta), lambda i, l, j: (0, i)),
                  pl.BlockSpec((None, 16, tn), lambda i, l, j: (l, 0, j)), wspec, wspec, wspec],
        out_specs=[wspec, wspec, wspec, wspec, pl.BlockSpec((16, ta), lambda i, l, j: (0, i))],
        out_shape=[shp, shp, shp, shp, jax.ShapeDtypeStruct((16, D), F32)], compiler_params=_cp(40),
    )(cs, dm, w, m, v)


def adamw_big(p, q, w, m, v, l, dsts, name):
    nl = w.shape[0]
    rows, cols = p.shape
    tr = min(rows, 256)
    w3, m3, v3 = (t.reshape(nl, rows, cols) for t in (w, m, v))
    nd = 0 if dsts is None else 4

    def body(p_ref, q_ref, w_ref, m_ref, v_ref, *rest):
        g_ref, d_ref, nm_ref, nv_ref = rest[nd:]
        g = p_ref[...].astype(F32) + q_ref[...].astype(F32)
        g_ref[...] = g
        d_ref[...], nm_ref[...], nv_ref[...] = _adamw(w_ref[...], g, m_ref[...], v_ref[...])

    part = pl.BlockSpec((tr, cols), lambda i: (i, 0))
    spec = pl.BlockSpec((None, tr, cols), lambda i: (l, i, 0))
    shp = jax.ShapeDtypeStruct((nl, rows, cols), F32)
    return pl.pallas_call(
        body, name=name, grid=(rows // tr,),
        in_specs=[part, part, spec, spec, spec] + [ANY] * nd,
        out_specs=[spec] * 4, out_shape=[shp] * 4,
        input_output_aliases={5 + t: t for t in range(nd)}, compiler_params=_cp(48),
    )(p, q, w3, m3, v3, *([] if dsts is None else dsts))


def sum8(g8, name):
    n = g8.shape[1]

    def body(g_ref, o_ref):
        acc = g_ref[0]
        for dev in range(1, 8):
            acc = acc + g_ref[dev]
        o_ref[...] = acc

    return pl.pallas_call(
        body, name=name, grid=(1,), in_specs=[pl.BlockSpec((8, n, 128), lambda i: (0, 0, 0))],
        out_specs=pl.BlockSpec((n, 128), lambda i: (0, 0)),
        out_shape=jax.ShapeDtypeStruct((n, 128), F32), compiler_params=_cp(48),
    )(g8)


def adamw_small(g, w, m, v, name):
    def body(g_ref, w_ref, m_ref, v_ref, d_ref, nm_ref, nv_ref):
        d_ref[...], nm_ref[...], nv_ref[...] = _adamw(w_ref[...], g_ref[...], m_ref[...], v_ref[...])

    spec = pl.BlockSpec(g.shape, lambda i: (0, 0))
    shp = jax.ShapeDtypeStruct(g.shape, F32)
    return pl.pallas_call(
        body, name=name, grid=(1,), in_specs=[spec] * 4, out_specs=[spec] * 3, out_shape=[shp] * 3,
        compiler_params=_cp(48),
    )(g, w, m, v)


def _pack(arrs):
    flat = jnp.concatenate([a.reshape(-1).astype(F32) for a in arrs])
    pad = (-flat.shape[0]) % 1024
    return jnp.pad(flat, (0, pad)).reshape(-1, 128)


def _unpack(packed, shapes):
    flat = packed.reshape(-1)
    out, off = [], 0
    for s in shapes:
        n = 1
        for d in s:
            n *= d
        out.append(flat[off:off + n].reshape(s))
        off += n
    return out


def _lower_bounds(lb_logits):
    pr = jax.nn.softmax(lb_logits.astype(F32), axis=1)
    return jnp.cumsum(pr, axis=1) - pr[:, :1]


class LayerWeights:
    def __init__(self, fetch):
        self.fetch, self.have, self.tokens = fetch, {}, []

    def get(self, key, after):
        if key not in self.have:
            new, token = self.fetch(key, after)
            self.have.update(new)
            if token is not None:
                self.tokens.append(token)
        return self.have[key]

    def tie(self, mod):
        for token in self.tokens:
            mod = mod + token[0, 0]
        self.tokens = []
        return mod


def device_step(xs, target, mods, sp, weights_for, grads_done):
    lbs = _lower_bounds(sp["lb_logits"])
    saved = []
    for layer in range(4):
        wl, md = weights_for(layer), mods[layer]
        w_in = wl.get("in" if layer % 2 == 0 else "pin", xs)
        md = wl.tie(md)
        s = {"xs": xs}
        h1 = norm_mod(xs, sp["g_norm_mix"][layer][None], md, 0, f"norm_mix{layer}")
        s["h1"] = h1
        if layer % 2 == 0:
            e = layer // 2
            p = mm_cols(h1, w_in, 0, f"in_proj{layer}", per=1)
            of, sf = hgrn_fwd(p, lbs[0, e][None], False, f"scan_f{layer}")
            ob, sb = hgrn_fwd(p, lbs[1, e][None], True, f"scan_b{layer}")
            ya = readout_fwd(of, ob, p, sp["g_hgrn_out"][e][None], f"readout{layer}")
            yb = chunkmlp_fwd(p, sp["w_spatial"][e], sp["b_spatial"][e][:, :, None], sp["g_spatial_v"][e][None],
                              f"cmlp{layer}")
            ycat = jnp.concatenate([ya, yb], axis=1)
            w_out = wl.get("out", ycat)
            md = wl.tie(md)
            x1, f1 = mm_rows_res(ycat, w_out, 0, xs, md, 2, f"out_proj{layer}")
            s.update(p=p, of=of, ob=ob, sf=sf, sb=sb, ycat=ycat, f1=f1)
        else:
            o = layer // 2
            pp = mm_rows(h1, w_in, 0, f"pool_in{layer}")
            z = pool_op(pp, False, f"pool{layer}")
            x1, ypre = mm_grp_res(z, wl.get("grp", z), 0, sp["b_grp_pool"][o].reshape(1, D),
                                  sp["scale_pool"][o][None], xs, md, 2, f"pool_grp{layer}")
            s.update(z=z, ypre=ypre)
        h2 = norm_mod(x1, sp["g_norm_ffn"][layer][None], md, 3, f"norm_ffn{layer}")
        u = mm_cols(h2, wl.get("up", h2), 0, f"ffn_up{layer}", epi=_epi_relu2, out_dtype=BF16)
        x2, f2 = mm_rows_res(u, wl.get("down", u), 0, x1, md, 5, f"ffn_down{layer}")
        s.update(x1=x1, h2=h2, u=u, f2=f2, w=wl.have, md=md)
        saved.append(s)
        xs = x2

    dx, loss_lanes, dg_final = final_loss(xs, sp["g_norm_final"][None], target, "final_loss")

    token = jnp.zeros((8, 128), F32)
    dmods = [None] * 4
    sg = {"g_norm_final": dg_final[0], "g_norm_mix": [None] * 4, "g_norm_ffn": [None] * 4,
          "dlbs": [[None, None], [None, None]], "g_hgrn_out": [None] * 2, "w_spatial": [None] * 2,
          "b_spatial": [None] * 2, "g_spatial_v": [None] * 2, "b_grp_pool": [None] * 2, "scale_pool": [None] * 2}
    for layer in reversed(range(4)):
        s = saved[layer]
        wl, md = s["w"], s["md"] + token[0, 0]
        df2, dgt2 = gate_in(dx, s["f2"], md, 5, f"gate_ffn{layer}")
        da = mm_t_rows(df2, wl["down"], 0, f"ffn_down_t{layer}", epi=_epi_2sqrt, out_dtype=BF16,
                       extras_of=s["u"], per=2)
        g_down = grad_rows(s["u"], df2, f"g_ffn_down{layer}", per=2, tn=2048)
        dh2 = mm_t_cols(da, wl["up"], 0, f"ffn_up_t{layer}")
        g_up = grad_cols(s["h2"], da, f"g_ffn_up{layer}", per=1)
        dx1, dsh2, dsc2, dgf = normmod_bwd(dh2, s["x1"], dx, sp["g_norm_ffn"][layer][None], md, 3,
                                           f"norm_ffn_b{layer}")
        sg["g_norm_ffn"][layer] = dgf[0]
        md = md + grads_done(layer, "ffn", [g_up, g_down], dx1)[0, 0]
        if layer % 2 == 0:
            e = layer // 2
            df1, dgt1 = gate_in(dx1, s["f1"], md, 2, f"gate_mix{layer}")
            dycat = mm_t_rows(df1, wl["out"], 0, f"out_proj_t{layer}")
            g_b = grad_rows(s["ycat"], df1, f"g_out_proj{layer}", tn=2048)
            do, dpg, dng = readout_bwd(dycat, s["of"], s["ob"], s["p"], sp["g_hgrn_out"][e][None],
                                       f"readout_b{layer}")
            du, dv, dws, dbs, dgn = chunkmlp_bwd(dycat, s["p"], sp["w_spatial"][e], sp["b_spatial"][e][:, :, None],
                                                 sp["g_spatial_v"][e][None], f"cmlp_b{layer}")
            dq_f, di_f, dff, dlb_f = hgrn_bwd(s["p"], do, s["sf"], lbs[0, e][None], False, f"scan_f_b{layer}")
            dq, di, dfb, dlb_b = hgrn_bwd(s["p"], do, s["sb"], lbs[1, e][None], True, f"scan_b_b{layer}",
                                          add=(dq_f, di_f))
            dp = jnp.concatenate([dq, dff, dfb, di, dpg, du, dv], axis=1)
            dh1 = mm_t_cols(dp, wl["in"], 0, f"in_proj_t{layer}")
            g_a = grad_cols(s["h1"], dp, f"g_in_proj{layer}", per=1)
            sg["dlbs"][0][e], sg["dlbs"][1][e] = dlb_f[0], dlb_b[0]
            sg["g_hgrn_out"][e], sg["w_spatial"][e] = dng[0], dws
            sg["b_spatial"][e], sg["g_spatial_v"][e] = dbs[:, :, 0], dgn[0]
        else:
            o = layer // 2
            dyp, dgt1, dscale, dbias = gate_in_pool(dx1, s["ypre"], md, sp["scale_pool"][o][None], 2,
                                                    f"gate_mix{layer}")
            dz = mm_t_grp(dyp, wl["grp"], 0, f"pool_grp_t{layer}")
            g_b = grad_grp(s["z"], dyp, f"g_pool_grp{layer}")
            dpp = pool_op(dz, True, f"pool_t{layer}")
            dh1 = mm_t_rows(dpp, wl["pin"], 0, f"pool_in_t{layer}")
            g_a = grad_rows(s["h1"], dpp, f"g_pool_in{layer}", tn=2048)
            sg["b_grp_pool"][o], sg["scale_pool"][o] = dbias[0].reshape(4, 512), dscale[0]
        dx, dsh1, dsc1, dgm = normmod_bwd(dh1, s["xs"], dx1, sp["g_norm_mix"][layer][None], md, 0,
                                          f"norm_mix_b{layer}")
        sg["g_norm_mix"][layer] = dgm[0]
        dmods[layer] = jnp.concatenate([dsh1, dsc1, dgt1, dsh2, dsc2, dgt2], axis=1)
        token = grads_done(layer, "mix", [g_a, g_b], dx)
    return loss_lanes, dx, dmods, sg


BIG = ("w_in_even", "w_out_even", "w_in_pool", "w_grp_pool", "w_ffn_up", "w_ffn_down")
SMALL = ("b_ada", "g_norm_mix", "g_norm_ffn", "lb_logits", "g_hgrn_out", "w_spatial", "b_spatial", "g_spatial_v",
         "b_grp_pool", "scale_pool", "g_norm_final")
WEIGHTS = ("c_ctx", "w_ada", "b_ada", "g_norm_mix", "g_norm_ffn", "w_in_even", "w_out_even", "lb_logits",
           "g_hgrn_out", "w_spatial", "b_spatial", "g_spatial_v", "w_in_pool", "w_grp_pool", "b_grp_pool",
           "scale_pool", "w_ffn_up", "w_ffn_down", "g_norm_final")


def kernel(x, c, ctx, c_ctx, w_ada, b_ada, g_norm_mix, g_norm_ffn, w_in_even, w_out_even, lb_logits, g_hgrn_out, w_spatial, b_spatial, g_spatial_v, w_in_pool, w_grp_pool, b_grp_pool, scale_pool, w_ffn_up, w_ffn_down, g_norm_final, loss_target, m_c_ctx, m_w_ada, m_b_ada, m_g_norm_mix, m_g_norm_ffn, m_w_in_even, m_w_out_even, m_lb_logits, m_g_hgrn_out, m_w_spatial, m_b_spatial, m_g_spatial_v, m_w_in_pool, m_w_grp_pool, m_b_grp_pool, m_scale_pool, m_w_ffn_up, m_w_ffn_down, m_g_norm_final, v_c_ctx, v_w_ada, v_b_ada, v_g_norm_mix, v_g_norm_ffn, v_w_in_even, v_w_out_even, v_lb_logits, v_g_hgrn_out, v_w_spatial, v_b_spatial, v_g_spatial_v, v_w_in_pool, v_w_grp_pool, v_b_grp_pool, v_scale_pool, v_w_ffn_up, v_w_ffn_down, v_g_norm_final):
    loc = dict(c_ctx=c_ctx, w_ada=w_ada, b_ada=b_ada, g_norm_mix=g_norm_mix, g_norm_ffn=g_norm_ffn,
               w_in_even=w_in_even, w_out_even=w_out_even, lb_logits=lb_logits, g_hgrn_out=g_hgrn_out,
               w_spatial=w_spatial, b_spatial=b_spatial, g_spatial_v=g_spatial_v, w_in_pool=w_in_pool,
               w_grp_pool=w_grp_pool, b_grp_pool=b_grp_pool, scale_pool=scale_pool, w_ffn_up=w_ffn_up,
               w_ffn_down=w_ffn_down, g_norm_final=g_norm_final)
    mom = dict(c_ctx=m_c_ctx, w_ada=m_w_ada, b_ada=m_b_ada, g_norm_mix=m_g_norm_mix, g_norm_ffn=m_g_norm_ffn,
               w_in_even=m_w_in_even, w_out_even=m_w_out_even, lb_logits=m_lb_logits, g_hgrn_out=m_g_hgrn_out,
               w_spatial=m_w_spatial, b_spatial=m_b_spatial, g_spatial_v=m_g_spatial_v, w_in_pool=m_w_in_pool,
               w_grp_pool=m_w_grp_pool, b_grp_pool=m_b_grp_pool, scale_pool=m_scale_pool, w_ffn_up=m_w_ffn_up,
               w_ffn_down=m_w_ffn_down, g_norm_final=m_g_norm_final)
    var = dict(c_ctx=v_c_ctx, w_ada=v_w_ada, b_ada=v_b_ada, g_norm_mix=v_g_norm_mix, g_norm_ffn=v_g_norm_ffn,
               w_in_even=v_w_in_even, w_out_even=v_w_out_even, lb_logits=v_lb_logits, g_hgrn_out=v_g_hgrn_out,
               w_spatial=v_w_spatial, b_spatial=v_b_spatial, g_spatial_v=v_g_spatial_v, w_in_pool=v_w_in_pool,
               w_grp_pool=v_w_grp_pool, b_grp_pool=v_b_grp_pool, scale_pool=v_scale_pool, w_ffn_up=v_w_ffn_up,
               w_ffn_down=v_w_ffn_down, g_norm_final=v_g_norm_final)
    mx, my, mc = _mesh_pos()
    chip = 2 * mx + my
    dev = 2 * chip + mc

    def layer_tensors(layer):
        i = layer // 2
        pair = ("w_in_even", "w_out_even") if layer % 2 == 0 else ("w_in_pool", "w_grp_pool")
        return [(pair[0], i), (pair[1], i), ("w_ffn_up", layer), ("w_ffn_down", layer)]

    def layer_keys(layer):
        return ("in", "out", "up", "down") if layer % 2 == 0 else ("pin", "grp", "up", "down")

    def gather_parts(layer):
        return {"a": slice(0, 1), "b": slice(1, 4)} if layer == 0 else {"a": slice(0, 4)}

    def start_gather(layer, part, after):
        srcs = [loc[n][i][None].astype(BF16) for n, i in layer_tensors(layer)[gather_parts(layer)[part]]]
        lands = [lax.dynamic_update_slice(lax.empty((NSH,) + s.shape, BF16), s[None], (chip,) + (0,) * s.ndim)
                 for s in srcs]
        return ici_start(srcs, lands, after, False, f"gather_start{layer}{part}")

    hello = allgather8(_pack([c[0], lb_logits, b_grp_pool, scale_pool]), "gather_small")
    parts = [_unpack(hello[d], [(D,), (2, 2, 256), (2, 4, 128), (2, 512)]) for d in range(8)]
    cs = jnp.concatenate([jnp.stack([parts[d][0] for d in range(8)]), c_ctx[None], jnp.zeros((7, D), F32)])
    chips_of = [parts[2 * s] for s in range(NSH)]
    sp = dict(loc)
    sp["lb_logits"] = jnp.concatenate([q[1] for q in chips_of], axis=2)
    sp["b_grp_pool"] = jnp.concatenate([q[2] for q in chips_of], axis=2)
    sp["scale_pool"] = jnp.concatenate([q[3] for q in chips_of], axis=1)

    b_loc = lax.dynamic_slice_in_dim(b_ada, chip * 3072, 3072, axis=1)[:, None, :]
    mods_loc = ada_mods(cs, w_ada, b_loc, "ada_mods")
    mods_all = allgather8(mods_loc.reshape(-1, 128), "gather_mods").reshape(8, 4, 16, 3072)
    mods_full = jnp.concatenate([mods_all[2 * s] for s in range(NSH)], axis=2)
    mine = lax.dynamic_index_in_dim(mods_full, dev, axis=1, keepdims=False)
    mods = [jnp.stack([mods_full[l, 8].reshape(6, D), mine[l].reshape(6, D)]) for l in range(4)]

    first_a = start_gather(0, "a", mods_all)
    gathers = {(0, "a"): first_a, (0, "b"): start_gather(0, "b", first_a[4])}
    mods[0] = mods[0] + gathers[(0, "b")][4][0, 0]

    def weights_for(layer):
        def fetch(key, after):
            part = "b" if layer == 0 and key != "in" else "a"
            _, got = ici_wait(gathers[(layer, part)], after, False, f"gather_wait{layer}{part}")
            token = None
            if layer < 3 and part == list(gather_parts(layer))[-1]:
                gathers[(layer + 1, "a")] = start_gather(layer + 1, "a", got[0])
                token = gathers[(layer + 1, "a")][4]
            return dict(zip(layer_keys(layer)[gather_parts(layer)[part]], got)), token
        return LayerWeights(fetch)

    exchanges = {}

    def grads_done(layer, part, gs, after):
        lands = [lax.empty((8,) + g.shape[1:], BF16) for g in gs]
        exchanges[(layer, part)] = ici_start(gs, lands, after, True, f"exchange_start{layer}{part}")
        return exchanges[(layer, part)][4]

    xs = jnp.concatenate([ctx[0], x[0]], axis=0)
    loss_lanes, dxs, dmods, sg = device_step(xs, loss_target[0], mods, sp, weights_for, grads_done)
    grad_x = dxs[CTX:][None]

    out = {}

    def finish(layer, part, after):
        gs, rbs = ici_wait(exchanges[(layer, part)], after, True, f"exchange_wait{layer}{part}")
        names = layer_tensors(layer)[slice(2, 4) if part == "ffn" else slice(0, 2)]
        ps = [sum_blocks(g, rb, f"sum_{n}{i}") for (n, i), g, rb in zip(names, gs, rbs)]
        qs = swap_sibling(ps, f"swap{layer}{part}")
        for (n, i), p, q in zip(names, ps, qs):
            out[n] = adamw_big(p, q, loc[n], mom[n], var[n], i, out.get(n), f"adamw_{n}{i}")

    done = dmods[0] + exchanges[(0, "mix")][4][0, 0]
    dmods[0] = done
    for layer in (3, 2, 1):
        finish(layer, "ffn", done)
        finish(layer, "mix", done)
    finish(0, "ffn", done)

    dm_lat = jnp.stack([dmods[l][1].reshape(6 * D) for l in range(4)])
    dm_ctx = jnp.stack([dmods[l][0].reshape(6 * D) for l in range(4)])
    small_shapes = [(4, 6 * D), (4, 6 * D), (4, D), (4, D), (2, 2, AW), (2, AW), (2, NH, 128, 128), (2, NH, 128),
                    (2, AW), (2, 4, 512), (2, D), (D,), (128,)]
    mine_small = _pack([dm_lat, dm_ctx, jnp.stack(sg["g_norm_mix"]), jnp.stack(sg["g_norm_ffn"]),
                        jnp.stack([jnp.stack(sg["dlbs"][0]), jnp.stack(sg["dlbs"][1])]),
                        jnp.stack(sg["g_hgrn_out"]), jnp.stack(sg["w_spatial"]), jnp.stack(sg["b_spatial"]),
                        jnp.stack(sg["g_spatial_v"]), jnp.stack(sg["b_grp_pool"]), jnp.stack(sg["scale_pool"]),
                        sg["g_norm_final"], loss_lanes[0]])
    all_small = allgather8(mine_small, "gather_small_grads")
    tot = _unpack(sum8(all_small, "sum_small_grads"), small_shapes)
    (_, dm_ctx_tot, g_mix, g_ffn, dlbs, g_hg, g_ws, g_bs, g_gv, g_bg, g_sc, g_fin, loss_v) = tot
    loss = loss_v[0]
    dm_lat_all = jnp.stack([_unpack(all_small[d], small_shapes[:1])[0] for d in range(8)])
    g_b_ada = jnp.sum(dm_lat_all, axis=0) + dm_ctx_tot
    _, lb_vjp = jax.vjp(_lower_bounds, sp["lb_logits"])
    g_lb_full = lb_vjp(dlbs)[0]
    grads = {"b_ada": g_b_ada, "g_norm_mix": g_mix, "g_norm_ffn": g_ffn,
             "lb_logits": lax.dynamic_slice_in_dim(g_lb_full, chip * 256, 256, axis=2),
             "g_hgrn_out": g_hg, "w_spatial": g_ws, "b_spatial": g_bs, "g_spatial_v": g_gv,
             "b_grp_pool": lax.dynamic_slice_in_dim(g_bg, chip * 128, 128, axis=2),
             "scale_pool": lax.dynamic_slice_in_dim(g_sc, chip * 512, 512, axis=1), "g_norm_final": g_fin}

    dm_rows = jnp.concatenate([dm_lat_all.transpose(1, 0, 2), dm_ctx_tot[:, None, :], jnp.zeros((4, 7, 6 * D), F32)],
                              axis=1)
    dm_loc = lax.dynamic_slice_in_dim(dm_rows, chip * 3072, 3072, axis=2)
    g_wa, d_wa, nm_wa, nv_wa, dc_part = ada_update(cs, dm_loc, w_ada, m_w_ada, v_w_ada, "ada_update")
    out["w_ada"] = [g_wa, d_wa, nm_wa, nv_wa]
    dc_all = allgather8(dc_part[8].reshape(16, 128), "gather_dc")
    dpre = dc_all[0] + dc_all[2] + dc_all[4] + dc_all[6]
    sig = jax.nn.sigmoid(c_ctx)
    grads["c_ctx"] = dpre.reshape(D) * (sig * (1.0 + c_ctx * (1.0 - sig)))

    names = ("c_ctx",) + SMALL
    shapes = [loc[n].shape for n in names]
    d_s, nm_s, nv_s = adamw_small(_pack([grads[n] for n in names]), _pack([loc[n] for n in names]),
                                  _pack([mom[n] for n in names]), _pack([var[n] for n in names]), "adamw_small")
    for n, dl, nm, nv in zip(names, _unpack(d_s, shapes), _unpack(nm_s, shapes), _unpack(nv_s, shapes)):
        out[n] = [grads[n], dl, nm, nv]

    finish(0, "mix", d_s)
    for n in BIG:
        out[n] = [t.reshape(loc[n].shape) for t in out[n]]

    return (loss, grad_x, *[out[n][0] for n in WEIGHTS], *[out[n][1] for n in WEIGHTS],
            *[out[n][2] for n in WEIGHTS], *[out[n][3] for n in WEIGHTS])
```

```python
import functools

import jax
import jax.numpy as jnp
from jax import lax
from jax.experimental import pallas as pl
from jax.experimental.pallas import tpu as pltpu

F32 = jnp.float32
BF16 = jnp.bfloat16
MESH = pl.DeviceIdType.MESH
ANY = pl.BlockSpec(memory_space=pl.ANY)
HIGHEST = lax.Precision.HIGHEST

D = 2048
DFF = 8192
CTX = 256
TR = 256
GRID_W = 64
EPS = 1e-6
LOG_FLOOR = 1e-30
NH = 8
DK = 128
SUB_FWD = 16
SUB_BWD = 32
B_CHUNK = 128
AW = NH * DK
POOL_WINDOWS = (2, 4, 8, 16)
NSH = 4
ADAM_LR, ADAM_B1, ADAM_B2, ADAM_EPS, ADAM_WD, ADAM_STEP = 0.001, 0.9, 0.999, 1e-08, 0.01, 10
MIB = 1024 * 1024


def _cp(vmem_mib):
    return pltpu.CompilerParams(vmem_limit_bytes=vmem_mib * MIB)


def _bf(v):
    return v.astype(BF16)


def _nn(a, b):
    return lax.dot_general(a, b, (((1,), (0,)), ((), ())), preferred_element_type=F32)


def _nt(a, b):
    return lax.dot_general(a, b, (((1,), (1,)), ((), ())), preferred_element_type=F32)


def _tn(a, b):
    return lax.dot_general(a, b, (((0,), (0,)), ((), ())), preferred_element_type=F32)


def _mesh_pos():
    return lax.axis_index("x"), lax.axis_index("y"), lax.axis_index("c")


HBM = pl.BlockSpec(memory_space=pltpu.HBM)
SEM = pl.BlockSpec(memory_space=pltpu.SEMAPHORE)
EFFECT = pltpu.SideEffectType.DATAFLOW_SIDE_EFFECTING


def _peer_copies(exchange, srcs, lands, send_sems, recv_sems):
    x, y, c = _mesh_pos()
    me = 2 * x + y
    pairs = []
    for t in range(len(srcs)):
        for j, (px, py) in enumerate([(1 - x, y), (x, 1 - y), (1 - x, 1 - y)]):
            peer = 2 * px + py
            src = srcs[t].at[peer] if exchange else srcs[t]
            out_slot, in_slot = (2 * me + c, 2 * peer + c) if exchange else (me, peer)

            def mk(slot, t=t, j=j, px=px, py=py, src=src):
                return pltpu.make_async_remote_copy(
                    src_ref=src, dst_ref=lands[t].at[slot], send_sem=send_sems.at[3 * t + j],
                    recv_sem=recv_sems.at[3 * t + j], device_id=(px, py, c), device_id_type=MESH)

            pairs.append((mk(out_slot), mk(in_slot)))
    return pairs


def ici_start(srcs, lands, after, exchange, name):
    n = len(srcs)

    def body(*refs):
        send_sems, recv_sems = refs[2 * n + 1], refs[2 * n + 2]
        for out_copy, _ in _peer_copies(exchange, refs[:n], refs[n:2 * n], send_sems, recv_sems):
            out_copy.start()
        refs[-1][...] = jnp.zeros_like(refs[-1])

    arrs = list(srcs) + list(lands)
    res = pl.pallas_call(
        body, name=name,
        out_shape=(pltpu.SemaphoreType.DMA((3 * n,)), pltpu.SemaphoreType.DMA((3 * n,)),
                   *[pltpu.HBM(a.shape, a.dtype) for a in arrs], jax.ShapeDtypeStruct((8, 128), F32)),
        in_specs=[HBM] * (2 * n) + [ANY],
        out_specs=(SEM, SEM, *[HBM] * (2 * n), pl.BlockSpec(memory_space=pltpu.VMEM)),
        input_output_aliases={t: 2 + t for t in range(2 * n)},
        compiler_params=pltpu.CompilerParams(has_side_effects=EFFECT),
    )(*[pltpu.with_memory_space_constraint(a, pltpu.HBM) for a in arrs], after)
    return res[0], res[1], list(res[2:2 + n]), list(res[2 + n:2 + 2 * n]), res[-1]


def ici_wait(started, after, exchange, name):
    send_sems, recv_sems, srcs, lands, _ = started
    n = len(srcs)

    def body(*refs):
        for out_copy, in_copy in _peer_copies(exchange, refs[:n], refs[n:2 * n], refs[2 * n], refs[2 * n + 1]):
            out_copy.wait_send()
            in_copy.wait_recv()

    arrs = list(srcs) + list(lands)
    res = pl.pallas_call(
        body, name=name,
        out_shape=tuple(pltpu.HBM(a.shape, a.dtype) for a in arrs),
        in_specs=[HBM] * (2 * n) + [SEM, SEM, ANY], out_specs=tuple([HBM] * (2 * n)),
        input_output_aliases={t: t for t in range(2 * n)},
        compiler_params=pltpu.CompilerParams(has_side_effects=EFFECT),
    )(*arrs, send_sems, recv_sems, after)
    return list(res[:n]), list(res[n:])


def sum_blocks(g, rb, name):
    cols = g.shape[-1]
    rows = g.size // (NSH * cols)
    tr = min(rows, 256)

    def body(g_ref, r1_ref, r2_ref, r3_ref, o_ref):
        acc = g_ref[...].astype(F32) + r1_ref[...].astype(F32) + r2_ref[...].astype(F32) + r3_ref[...].astype(F32)
        o_ref[...] = acc.astype(BF16)

    def mine(i):
        x, y, _ = _mesh_pos()
        return (2 * x + y, i, 0)

    def peer(fx, fy):
        def index(i):
            x, y, c = _mesh_pos()
            return (2 * (2 * (x ^ fx) + (y ^ fy)) + c, i, 0)
        return pl.BlockSpec((None, tr, cols), index)

    return pl.pallas_call(
        body, name=name, grid=(rows // tr,),
        in_specs=[pl.BlockSpec((None, tr, cols), mine), peer(1, 0), peer(0, 1), peer(1, 1)],
        out_specs=pl.BlockSpec((tr, cols), lambda i: (i, 0)),
        out_shape=jax.ShapeDtypeStruct((rows, cols), BF16), compiler_params=_cp(32),
    )(g.reshape(NSH, rows, cols), *[rb.reshape(8, rows, cols)] * 3)


def swap_sibling(ps, name):
    n = len(ps)

    def body(*refs):
        ins, outs, send_sems, recv_sems = refs[:n], refs[n:2 * n], refs[2 * n], refs[2 * n + 1]
        x, y, c = _mesh_pos()
        copies = [pltpu.make_async_remote_copy(
            src_ref=ins[t], dst_ref=outs[t], send_sem=send_sems.at[t], recv_sem=recv_sems.at[t],
            device_id=(x, y, 1 - c), device_id_type=MESH) for t in range(n)]
        for cp in copies:
            cp.start()
        for cp in copies:
            cp.wait_recv()
            cp.wait_send()

    return pl.pallas_call(
        body, name=name, out_shape=[jax.ShapeDtypeStruct(p.shape, p.dtype) for p in ps],
        in_specs=[ANY] * n, out_specs=[ANY] * n,
        scratch_shapes=[pltpu.SemaphoreType.DMA((n,)), pltpu.SemaphoreType.DMA((n,))],
    )(*ps)


def allgather8(v, name):
    m, n = v.shape

    def body(x_ref, out_ref, send_sems, recv_sems, local_sem):
        x, y, c = _mesh_pos()
        me, sibling = (x, y, c), (x, y, 1 - c)
        chips = [(1 - x, y), (x, 1 - y), (1 - x, 1 - y)]

        def rows(px, py, pc):
            return out_ref.at[4 * px + 2 * py + pc]

        def copy(k, block, to, src=None):
            return pltpu.make_async_remote_copy(
                src_ref=rows(*block) if src is None else src, dst_ref=rows(*block),
                send_sem=send_sems.at[k], recv_sem=recv_sems.at[k], device_id=to, device_id_type=MESH)

        mine = pltpu.make_async_copy(x_ref, rows(*me), local_sem)
        mine.start()
        first = [copy(0, me, sibling, src=x_ref)]
        first += [copy(1 + j, me, (*chip, c), src=x_ref) for j, chip in enumerate(chips)]
        for cp in first:
            cp.start()
        passed = [copy(4 + j, (*chip, c), sibling) for j, chip in enumerate(chips)]
        for j, chip in enumerate(chips):
            copy(1 + j, (*chip, c), me).wait_recv()
            passed[j].start()
        copy(0, sibling, me).wait_recv()
        for j, chip in enumerate(chips):
            copy(4 + j, (*chip, 1 - c), me).wait_recv()
        for cp in first + passed:
            cp.wait_send()
        mine.wait()

    return pl.pallas_call(
        body, name=name,
        out_shape=jax.ShapeDtypeStruct((8, m, n), v.dtype),
        in_specs=[pl.BlockSpec(memory_space=pltpu.VMEM)],
        out_specs=pl.BlockSpec(memory_space=pltpu.VMEM),
        scratch_shapes=[pltpu.SemaphoreType.DMA((7,)), pltpu.SemaphoreType.DMA((7,)), pltpu.SemaphoreType.DMA],
        compiler_params=_cp(40),
    )(v)


def _row_spec(width=D, off=0):
    return pl.BlockSpec((TR, width), lambda i, off=off: (i, off))


def _vec_spec(width=D):
    return pl.BlockSpec((1, width), lambda i: (0, 0))


def _mod_spec():
    return pl.BlockSpec((None, 6, D), lambda i: (jnp.minimum(i, 1), 0, 0))


def _pair_spec(width=D):
    return pl.BlockSpec((None, 1, width), lambda i: (jnp.minimum(i, 1), 0, 0))


def _accum(ref, val, first):
    @pl.when(first)
    def _():
        ref[...] = val

    @pl.when(jnp.logical_not(first))
    def _():
        ref[...] += val


def norm_mod(xs, g, mod, si, name):
    r = xs.shape[0]

    def body(x_ref, g_ref, m_ref, o_ref):
        x = x_ref[...]
        rstd = lax.rsqrt(jnp.mean(x * x, axis=-1, keepdims=True) + EPS)
        n = x * rstd * g_ref[...]
        o_ref[...] = (n * (1.0 + m_ref[si + 1:si + 2, :]) + m_ref[si:si + 1, :]).astype(BF16)

    return pl.pallas_call(
        body, name=name, grid=(r // TR,),
        in_specs=[_row_spec(), _vec_spec(), _mod_spec()], out_specs=_row_spec(),
        out_shape=jax.ShapeDtypeStruct((r, D), BF16), compiler_params=_cp(32),
    )(xs, g, mod)


def gate_in(dx, f, mod, gi, name):
    r = dx.shape[0]

    def body(dx_ref, f_ref, m_ref, o_ref, dg_ref):
        i = pl.program_id(0)
        dxv = dx_ref[...]
        o_ref[...] = (dxv * m_ref[gi:gi + 1, :]).astype(BF16)
        _accum(dg_ref, jnp.sum(dxv * f_ref[...].astype(F32), axis=0, keepdims=True), i <= 1)

    return pl.pallas_call(
        body, name=name, grid=(r // TR,),
        in_specs=[_row_spec(), _row_spec(), _mod_spec()], out_specs=[_row_spec(), _pair_spec()],
        out_shape=[jax.ShapeDtypeStruct((r, D), BF16), jax.ShapeDtypeStruct((2, 1, D), F32)],
        compiler_params=_cp(32),
    )(dx, f, mod)


def gate_in_pool(dx, ypre, mod, scale, gi, name):
    r = dx.shape[0]

    def body(dx_ref, y_ref, m_ref, s_ref, o_ref, dg_ref, ds_ref, db_ref):
        i = pl.program_id(0)
        dxv = dx_ref[...]
        yp = y_ref[...].astype(F32)
        sc = s_ref[...]
        dy = dxv * m_ref[gi:gi + 1, :]
        dyp = dy * sc
        o_ref[...] = dyp.astype(BF16)
        _accum(dg_ref, jnp.sum(dxv * (yp * sc), axis=0, keepdims=True), i <= 1)
        _accum(ds_ref, jnp.sum(dy * yp, axis=0, keepdims=True), i == 0)
        _accum(db_ref, jnp.sum(dyp, axis=0, keepdims=True), i == 0)

    return pl.pallas_call(
        body, name=name, grid=(r // TR,),
        in_specs=[_row_spec(), _row_spec(), _mod_spec(), _vec_spec()],
        out_specs=[_row_spec(), _pair_spec(), _vec_spec(), _vec_spec()],
        out_shape=[jax.ShapeDtypeStruct((r, D), BF16), jax.ShapeDtypeStruct((2, 1, D), F32),
                   jax.ShapeDtypeStruct((1, D), F32), jax.ShapeDtypeStruct((1, D), F32)],
        compiler_params=_cp(32),
    )(dx, ypre, mod, scale)


def normmod_bwd(dh, x, dxo, g, mod, si, name):
    r = x.shape[0]

    def body(dh_ref, x_ref, dxo_ref, g_ref, m_ref, dx_ref, dsh_ref, dsc_ref, dg_ref):
        i = pl.program_id(0)
        xv = x_ref[...]
        dhv = dh_ref[...]
        gv = g_ref[...]
        rstd = lax.rsqrt(jnp.mean(xv * xv, axis=-1, keepdims=True) + EPS)
        xhat = xv * rstd
        dn = dhv * (1.0 + m_ref[si + 1:si + 2, :])
        dxh = dn * gv
        dx_ref[...] = rstd * (dxh - xhat * jnp.mean(dxh * xhat, axis=-1, keepdims=True)) + dxo_ref[...]
        _accum(dsh_ref, jnp.sum(dhv, axis=0, keepdims=True), i <= 1)
        _accum(dsc_ref, jnp.sum(dhv * (xhat * gv), axis=0, keepdims=True), i <= 1)
        _accum(dg_ref, jnp.sum(dn * xhat, axis=0, keepdims=True), i == 0)

    return pl.pallas_call(
        body, name=name, grid=(r // TR,),
        in_specs=[_row_spec(), _row_spec(), _row_spec(), _vec_spec(), _mod_spec()],
        out_specs=[_row_spec(), _pair_spec(), _pair_spec(), _vec_spec()],
        out_shape=[jax.ShapeDtypeStruct((r, D), F32), jax.ShapeDtypeStruct((2, 1, D), F32),
                   jax.ShapeDtypeStruct((2, 1, D), F32), jax.ShapeDtypeStruct((1, D), F32)],
        compiler_params=_cp(48),
    )(dh, x, dxo, g, mod)


def final_loss(xs, g, target, name):
    r = xs.shape[0]

    def body(x_ref, g_ref, t_ref, dx_ref, loss_ref, dg_ref):
        i = pl.program_id(0)

        @pl.when(i == 0)
        def _():
            dx_ref[...] = jnp.zeros_like(dx_ref)
            loss_ref[...] = jnp.zeros_like(loss_ref)
            dg_ref[...] = jnp.zeros_like(dg_ref)

        @pl.when(i > 0)
        def _():
            xv = x_ref[...]
            gv = g_ref[...]
            rstd = lax.rsqrt(jnp.mean(xv * xv, axis=-1, keepdims=True) + EPS)
            xhat = xv * rstd
            err = xhat * gv - t_ref[...]
            part = 0.5 * jnp.sum(jnp.mean(err * err, axis=-1, keepdims=True), axis=0, keepdims=True)
            lane = lax.broadcasted_iota(jnp.int32, (1, 128), 1)
            loss_ref[...] += jnp.where(lane == 0, part, 0.0)
            dy = err * (1.0 / D)
            dg_ref[...] += jnp.sum(dy * xhat, axis=0, keepdims=True)
            dxh = dy * gv
            dx_ref[...] = rstd * (dxh - xhat * jnp.mean(dxh * xhat, axis=-1, keepdims=True))

    return pl.pallas_call(
        body, name=name, grid=(r // TR,),
        in_specs=[_row_spec(), _vec_spec(), pl.BlockSpec((TR, D), lambda i: (jnp.maximum(i - 1, 0), 0))],
        out_specs=[_row_spec(), pl.BlockSpec((1, 128), lambda i: (0, 0)), _vec_spec()],
        out_shape=[jax.ShapeDtypeStruct((r, D), F32), jax.ShapeDtypeStruct((1, 128), F32),
                   jax.ShapeDtypeStruct((1, D), F32)],
        compiler_params=_cp(32),
    )(xs, g, target)


def _mm(name, mode, a, b, grid, a_spec, b_spec, out_shapes, out_specs, epi, kaxis=None, acc=None,
        extras=(), merge_b=False, vmem=48):
    ne, no = len(extras), len(out_shapes)
    dot2 = {"nn": _nn, "nt": _nt, "tn": _tn}[mode]
    nk = grid[kaxis] if kaxis is not None else 1

    def dot(av, bv):
        return dot2(av, bv.reshape(-1, bv.shape[-1]) if merge_b else bv)

    def body(*refs):
        a_ref, b_ref = refs[0], refs[1]
        ex = refs[2:2 + ne]
        outs = refs[2 + ne:2 + ne + no]
        ids = [pl.program_id(ax) for ax in range(len(grid))]
        if kaxis is None:
            epi(dot(a_ref[...], b_ref[...]), ids, ex, outs)
        else:
            acc_ref = refs[-1]
            k = ids[kaxis]

            @pl.when(k == 0)
            def _():
                acc_ref[...] = jnp.zeros_like(acc_ref)

            acc_ref[...] += dot(a_ref[...], b_ref[...])

            @pl.when(k == nk - 1)
            def _():
                epi(acc_ref[...], ids, ex, outs)

    operands = [a, b] + [e[0] for e in extras]
    in_specs = [a_spec, b_spec] + [e[1] for e in extras]
    return pl.pallas_call(
        body, name=name, grid=grid, in_specs=in_specs, out_specs=out_specs, out_shape=out_shapes,
        scratch_shapes=[] if kaxis is None else [pltpu.VMEM(acc, F32)], compiler_params=_cp(vmem),
    )(*operands)


def _epi_store(acc, ids, ex, outs):
    outs[0][...] = acc.astype(outs[0].dtype)


def _epi_relu2(acc, ids, ex, outs):
    rl = jnp.maximum(acc, 0.0)
    outs[0][...] = (rl * rl).astype(BF16)


def _epi_2sqrt(acc, ids, ex, outs):
    outs[0][...] = (acc * (2.0 * jnp.sqrt(ex[0][...].astype(F32)))).astype(BF16)


def _gate_rows(ids, tm, shape, mod_ref, gi):
    rid = ids[0] * tm + lax.broadcasted_iota(jnp.int32, shape, 0)
    return jnp.where(rid < CTX, mod_ref[0, gi:gi + 1, :], mod_ref[1, gi:gi + 1, :])


def _epi_res(gi, tm):
    def epi(acc, ids, ex, outs):
        outs[0][...] = ex[0][...] + _gate_rows(ids, tm, acc.shape, ex[1], gi) * acc
        outs[1][...] = acc.astype(BF16)
    return epi


def _epi_pool(gi, tm):
    def epi(acc, ids, ex, outs):
        ypre = acc + ex[2][...]
        outs[0][...] = ex[0][...] + _gate_rows(ids, tm, acc.shape, ex[1], gi) * (ypre * ex[3][...])
        outs[1][...] = ypre.astype(BF16)
    return epi


def _tm(r):
    return 768 if r % 768 == 0 else TR


def _tm_wide(r):
    return 1408 if r % 1408 == 0 else _tm(r)


def mm_cols(a, w, l, name, epi=_epi_store, out_dtype=F32, per=2):
    r, k = a.shape
    c = w.shape[3]
    tm, tn = _tm_wide(r), c // per
    return _mm(name, "nn", a, w, (r // tm, NSH * per),
               pl.BlockSpec((tm, k), lambda i, j: (i, 0)),
               pl.BlockSpec((None, None, k, tn), lambda i, j: (j // per, l, 0, j % per)),
               [jax.ShapeDtypeStruct((r, NSH * c), out_dtype)], [pl.BlockSpec((tm, tn), lambda i, j: (i, j))],
               epi, vmem=56)[0]


def mm_rows_res(a, w, l, res, mod, gi, name):
    r = a.shape[0]
    kc = w.shape[2]
    tm, tn = (1056 if r % 1056 == 0 else _tm(r)), 1024
    return _mm(name, "nn", a, w, (r // tm, D // tn, NSH),
               pl.BlockSpec((tm, kc), lambda i, j, k: (i, k)),
               pl.BlockSpec((None, None, kc, tn), lambda i, j, k: (k, l, 0, j)),
               [jax.ShapeDtypeStruct((r, D), F32), jax.ShapeDtypeStruct((r, D), BF16)],
               [pl.BlockSpec((tm, tn), lambda i, j, k: (i, j))] * 2,
               _epi_res(gi, tm), kaxis=2, acc=(tm, tn),
               extras=[(res, pl.BlockSpec((tm, tn), lambda i, j, k: (i, j))),
                       (mod, pl.BlockSpec((2, 6, tn), lambda i, j, k: (0, 0, j)))], vmem=56)


def mm_rows(a, w, l, name):
    r = a.shape[0]
    kc = w.shape[2]
    tm, tn = _tm(r), 1024
    return _mm(name, "nn", a, w, (r // tm, D // tn, NSH),
               pl.BlockSpec((tm, kc), lambda i, j, k: (i, k)),
               pl.BlockSpec((None, None, kc, tn), lambda i, j, k: (k, l, 0, j)),
               [jax.ShapeDtypeStruct((r, D), F32)], [pl.BlockSpec((tm, tn), lambda i, j, k: (i, j))],
               _epi_store, kaxis=2, acc=(tm, tn), vmem=56)[0]


def mm_grp_res(z, w, o, bias, scale, res, mod, gi, name):
    r = z.shape[0]
    tm = _tm(r)
    return _mm(name, "nn", z, w, (r // tm, 4),
               pl.BlockSpec((tm, 512), lambda i, g: (i, g)), _grp_spec(o),
               [jax.ShapeDtypeStruct((r, D), F32), jax.ShapeDtypeStruct((r, D), BF16)],
               [pl.BlockSpec((tm, 512), lambda i, g: (i, g))] * 2,
               _epi_pool(gi, tm), merge_b=True,
               extras=[(res, pl.BlockSpec((tm, 512), lambda i, g: (i, g))),
                       (mod, pl.BlockSpec((2, 6, 512), lambda i, g: (0, 0, g))),
                       (bias, pl.BlockSpec((1, 512), lambda i, g: (0, g))),
                       (scale, pl.BlockSpec((1, 512), lambda i, g: (0, g)))], vmem=48)


def _grp_spec(o):
    return pl.BlockSpec((NSH, None, None, 128, 512), lambda i, g: (0, o, g, 0, 0))


def mm_t_rows(a, w, l, name, epi=_epi_store, out_dtype=F32, extras_of=None, per=1):
    r, n = a.shape
    kc = w.shape[2]
    tm, tn = _tm_wide(r), kc // per
    extras = []
    if extras_of is not None:
        extras = [(extras_of, pl.BlockSpec((tm, tn), lambda i, j: (i, j)))]
    return _mm(name, "nt", a, w, (r // tm, NSH * per),
               pl.BlockSpec((tm, n), lambda i, j: (i, 0)),
               pl.BlockSpec((None, None, tn, n), lambda i, j: (j // per, l, j % per, 0)),
               [jax.ShapeDtypeStruct((r, NSH * kc), out_dtype)], [pl.BlockSpec((tm, tn), lambda i, j: (i, j))],
               epi, extras=extras, vmem=56)[0]


def mm_t_cols(a, w, l, name):
    r = a.shape[0]
    k, c = w.shape[2], w.shape[3]
    tm, tn = _tm_wide(r), 1024
    return _mm(name, "nt", a, w, (r // tm, k // tn, NSH),
               pl.BlockSpec((tm, c), lambda i, j, s: (i, s)),
               pl.BlockSpec((None, None, tn, c), lambda i, j, s: (s, l, j, 0)),
               [jax.ShapeDtypeStruct((r, k), F32)], [pl.BlockSpec((tm, tn), lambda i, j, s: (i, j))],
               _epi_store, kaxis=2, acc=(tm, tn), vmem=56)[0]


def mm_t_grp(dy, w, o, name):
    r = dy.shape[0]
    tm = _tm(r)
    return _mm(name, "nt", dy, w, (r // tm, 4),
               pl.BlockSpec((tm, 512), lambda i, g: (i, g)), _grp_spec(o),
               [jax.ShapeDtypeStruct((r, D), F32)], [pl.BlockSpec((tm, 512), lambda i, g: (i, g))],
               _epi_store, merge_b=True, vmem=40)[0]


def grad_cols(a, b, name, ta=1024, per=2):
    r, k = a.shape
    c = b.shape[1] // NSH
    tk, tn = _tm(r), c // per
    return _mm(name, "tn", a, b, (NSH, k // ta, per, r // tk),
               pl.BlockSpec((tk, ta), lambda s, i, j, t: (t, i)),
               pl.BlockSpec((tk, tn), lambda s, i, j, t: (t, s * per + j)),
               [jax.ShapeDtypeStruct((NSH, 1, k, c), BF16)],
               [pl.BlockSpec((None, None, ta, tn), lambda s, i, j, t: (s, 0, i, j))],
               _epi_store, kaxis=3, acc=(ta, tn), vmem=56)[0]


def grad_rows(a, b, name, per=1, tn=1024):
    r = a.shape[0]
    kc, n = a.shape[1] // NSH, b.shape[1]
    tk, ta = _tm(r), kc // per
    return _mm(name, "tn", a, b, (NSH, per, n // tn, r // tk),
               pl.BlockSpec((tk, ta), lambda s, i, j, t: (t, s * per + i)),
               pl.BlockSpec((tk, tn), lambda s, i, j, t: (t, j)),
               [jax.ShapeDtypeStruct((NSH, 1, kc, n), BF16)],
               [pl.BlockSpec((None, None, ta, tn), lambda s, i, j, t: (s, 0, i, j))],
               _epi_store, kaxis=3, acc=(ta, tn), vmem=56)[0]


def grad_grp(z, dy, name):
    r = z.shape[0]
    tk = _tm(r)

    def epi(acc, ids, ex, outs):
        for s in range(NSH):
            outs[0][s] = acc[s * 128:(s + 1) * 128].astype(BF16)

    return _mm(name, "tn", z, dy, (4, r // tk),
               pl.BlockSpec((tk, 512), lambda g, t: (t, g)), pl.BlockSpec((tk, 512), lambda g, t: (t, g)),
               [jax.ShapeDtypeStruct((NSH, 1, 4, 128, 512), BF16)],
               [pl.BlockSpec((NSH, None, None, 128, 512), lambda g, t: (0, 0, g, 0, 0))],
               epi, kaxis=1, acc=(512, 512), vmem=40)[0]


def _scan_tile(reverse, nt):
    if reverse:
        return lambda p: jnp.where(p == 0, 0, nt - p)
    return lambda p: p


def _gates(f, lbv):
    sg = jax.nn.sigmoid(f)
    fg = lbv + (1.0 - lbv) * sg
    g = jnp.log(jnp.maximum(fg, LOG_FLOOR))
    kk = (1.0 - lbv) * jax.nn.sigmoid(-f)
    return sg, fg, g, kk


def _chunk_cumsum(g, reverse, sub):
    n = g.shape[0]
    rr = lax.broadcasted_iota(jnp.int32, (n, n), 0)
    cc = lax.broadcasted_iota(jnp.int32, (n, n), 1)
    inside = (rr // sub) == (cc // sub)
    tri = jnp.where(inside & ((cc >= rr) if reverse else (cc <= rr)), 1.0, 0.0).astype(F32)
    return jnp.dot(tri, g, precision=HIGHEST, preferred_element_type=F32)


def _decay(b, s, rows, reverse):
    return jnp.where((rows <= s) if reverse else (rows >= s), jnp.exp(b - b[s:s + 1]), 0.0)


def hgrn_fwd(p, lb_f, lb_b, name):
    SUB = SUB_FWD
    r = p.shape[0]
    nt = r // TR
    nsub = TR // SUB
    tiles = (_scan_tile(False, nt), _scan_tile(True, nt))

    def body(*refs):
        ins, outs, scr = refs[:8], refs[8:12], refs[12:]
        dirs = [(rev, ins[4 * d:4 * d + 4], outs[2 * d:2 * d + 2], scr[3 * d:3 * d + 3]) for d, rev in
                enumerate((False, True))]
        i = pl.program_id(0)
        for reverse, (q_ref, f_ref, v_ref, lb_ref), (o_ref, sin_ref), (st, k_s, b_s) in dirs:
            @pl.when(i == 0)
            def _(st=st):
                st[...] = jnp.zeros_like(st)

            sin_ref[...] = st[...]
            _, _, g, kk = _gates(f_ref[...], lb_ref[...])
            k_s[...] = kk
            b_s[...] = _chunk_cumsum(g, reverse, SUB)
        rows = lax.broadcasted_iota(jnp.int32, (SUB, DK), 0)

        def sub(jj, carry):
            for reverse, (q_ref, f_ref, v_ref, lb_ref), (o_ref, sin_ref), (st, k_s, b_s) in dirs:
                j = (nsub - 1 - jj) if reverse else jj
                rs = pl.ds(pl.multiple_of(j * SUB, SUB), SUB)
                for h in range(NH):
                    sl = slice(h * DK, (h + 1) * DK)
                    q, k, b, v = q_ref[rs, sl], k_s[rs, sl], b_s[rs, sl], v_ref[rs, sl]
                    btot = b[0:1] if reverse else b[SUB - 1:SUB]
                    o = _nt(_bf(q * jnp.exp(b)), _bf(st[h]))
                    for s in range(SUB):
                        col = jnp.sum(q * k[s:s + 1] * _decay(b, s, rows, reverse), axis=-1, keepdims=True)
                        o = o + col * v[s:s + 1]
                    o_ref[rs, sl] = o
                    st[h] = st[h] * jnp.exp(btot) + _tn(_bf(v), _bf(k * jnp.exp(btot - b)))
            return carry

        lax.fori_loop(0, nsub, sub, 0)

    def seg(d, col):
        return pl.BlockSpec((TR, AW), lambda i: (tiles[d](i), col))

    def outs_of(d):
        return [pl.BlockSpec((TR, AW), lambda i: (tiles[d](i), 0)),
                pl.BlockSpec((None, NH, DK, DK), lambda i: (tiles[d](i), 0, 0, 0))]

    shapes = [jax.ShapeDtypeStruct((r, AW), F32), jax.ShapeDtypeStruct((nt, NH, DK, DK), F32)]
    scratch = [pltpu.VMEM((NH, DK, DK), F32), pltpu.VMEM((TR, AW), F32), pltpu.VMEM((TR, AW), F32)]
    of, sf, ob, sb = pl.pallas_call(
        body, name=name, grid=(nt,),
        in_specs=[seg(0, 0), seg(0, 1), seg(0, 3), _vec_spec(AW), seg(1, 0), seg(1, 2), seg(1, 3), _vec_spec(AW)],
        out_specs=outs_of(0) + outs_of(1), out_shape=shapes * 2, scratch_shapes=scratch * 2,
        compiler_params=_cp(48),
    )(p, p, p, lb_f, p, p, p, lb_b)
    return (of, sf), (ob, sb)


def hgrn_bwd(p, do, sin, lb, reverse, name, add=None):
    SUB = SUB_BWD
    r = p.shape[0]
    nt = r // TR
    nsub = TR // SUB
    fcol = 2 if reverse else 1
    tile0 = _scan_tile(reverse, nt)
    tile = lambda i: tile0(nt - 1 - i)
    nadd = 0 if add is None else 2
    out_dt = F32 if add is None else BF16

    def body(*refs):
        q_ref, f_ref, v_ref, do_ref, sin_ref, lb_ref = refs[:6]
        adds = refs[6:6 + nadd]
        dq_ref, dv_ref, df_ref, dlb_ref = refs[6 + nadd:10 + nadd]
        dst, srun, ssub, k_s, b_s, sg_s, fg_s = refs[10 + nadd:]
        i = pl.program_id(0)

        @pl.when(i == 0)
        def _():
            dst[...] = jnp.zeros_like(dst)
            dlb_ref[...] = jnp.zeros_like(dlb_ref)

        lbv = lb_ref[...]
        sg, fg, g, kk = _gates(f_ref[...], lbv)
        k_s[...] = kk
        sg_s[...] = sg
        fg_s[...] = fg
        b_s[...] = _chunk_cumsum(g, reverse, SUB)
        srun[...] = sin_ref[...]
        rows = lax.broadcasted_iota(jnp.int32, (SUB, DK), 0)
        r16 = lax.broadcasted_iota(jnp.int32, (SUB, SUB), 0)
        c16 = lax.broadcasted_iota(jnp.int32, (SUB, SUB), 1)
        later = jnp.where((c16 <= r16) if reverse else (c16 >= r16), 1.0, 0.0).astype(F32)

        def recompute(jj, c):
            j = (nsub - 1 - jj) if reverse else jj
            rs = pl.ds(pl.multiple_of(j * SUB, SUB), SUB)
            for h in range(NH):
                sl = slice(h * DK, (h + 1) * DK)
                k, b, v = k_s[rs, sl], b_s[rs, sl], v_ref[rs, sl]
                btot = b[0:1] if reverse else b[SUB - 1:SUB]
                ssub[jj, h] = srun[h]
                srun[h] = srun[h] * jnp.exp(btot) + _tn(_bf(v), _bf(k * jnp.exp(btot - b)))
            return c

        lax.fori_loop(0, nsub, recompute, 0)
        for h in range(NH):
            ssub[nsub, h] = srun[h]

        def back(jj, c):
            pos = nsub - 1 - jj
            j = jj if reverse else pos
            rs = pl.ds(pl.multiple_of(j * SUB, SUB), SUB)
            for h in range(NH):
                sl = slice(h * DK, (h + 1) * DK)
                q, k, b, v, dov = q_ref[rs, sl], k_s[rs, sl], b_s[rs, sl], v_ref[rs, sl], do_ref[rs, sl]
                btot = b[0:1] if reverse else b[SUB - 1:SUB]
                s0 = ssub[pos, h]
                ds = dst[h]
                dg_next = jnp.sum(ds * ssub[pos + 1, h], axis=0, keepdims=True)
                eb = jnp.exp(b)
                ebt = jnp.exp(btot - b)
                ke = k * ebt
                dq = _nn(_bf(dov), _bf(s0)) * eb
                dk = _nn(_bf(v), _bf(ds)) * ebt
                dv = _nt(_bf(ke), _bf(ds))
                for s in range(SUB):
                    dec = _decay(b, s, rows, reverse)
                    dsc = jnp.sum(dov * v[s:s + 1], axis=-1, keepdims=True)
                    qd = q * dec
                    dq = dq + (dsc * dec) * k[s:s + 1]
                    dk_row = jnp.sum(dsc * qd, axis=0, keepdims=True)
                    sc = jnp.sum(qd * k[s:s + 1], axis=-1, keepdims=True)
                    dv_row = jnp.sum(sc * dov, axis=0, keepdims=True)
                    dk = dk + jnp.where(rows == s, dk_row, 0.0)
                    dv = dv + jnp.where(rows == s, dv_row, 0.0)
                dst[h] = ds * jnp.exp(btot) + _tn(_bf(dov), _bf(q * eb))
                dg = jnp.dot(later, q * dq - k * dk, precision=HIGHEST, preferred_element_type=F32) + dg_next
                sgv, fgv, lbh = sg_s[rs, sl], fg_s[rs, sl], lbv[:, sl]
                dfg = jnp.where(fgv > LOG_FLOOR, dg / fgv, 0.0)
                df_ref[rs, sl] = ((1.0 - lbh) * sgv * (1.0 - sgv) * (dfg - dk)).astype(BF16)
                dlb_ref[:, sl] += jnp.sum((dfg - dk) * (1.0 - sgv), axis=0, keepdims=True)
                if add is None:
                    dq_ref[rs, sl] = dq
                    dv_ref[rs, sl] = dv
                else:
                    dq_ref[rs, sl] = (dq + adds[0][rs, sl]).astype(BF16)
                    dv_ref[rs, sl] = (dv + adds[1][rs, sl]).astype(BF16)
            return c

        lax.fori_loop(0, nsub, back, 0)

    seg = lambda col: pl.BlockSpec((TR, AW), lambda i, col=col: (tile(i), col))
    plain = pl.BlockSpec((TR, AW), lambda i: (tile(i), 0))
    operands = [p, p, p, do, sin, lb]
    in_specs = [seg(0), seg(fcol), seg(3), plain,
                pl.BlockSpec((None, NH, DK, DK), lambda i: (tile(i), 0, 0, 0)), _vec_spec(AW)]
    if add is not None:
        operands += list(add)
        in_specs += [plain, plain]
    return pl.pallas_call(
        body, name=name, grid=(nt,), in_specs=in_specs,
        out_specs=[plain, plain, plain, _vec_spec(AW)],
        out_shape=[jax.ShapeDtypeStruct((r, AW), out_dt), jax.ShapeDtypeStruct((r, AW), out_dt),
                   jax.ShapeDtypeStruct((r, AW), BF16), jax.ShapeDtypeStruct((1, AW), F32)],
        scratch_shapes=[pltpu.VMEM((NH, DK, DK), F32), pltpu.VMEM((NH, DK, DK), F32),
                        pltpu.VMEM((nsub + 1, NH, DK, DK), F32)]
        + [pltpu.VMEM((TR, AW), F32)] * 4,
        compiler_params=_cp(56),
    )(*operands)


def _silu(v):
    return v * jax.nn.sigmoid(v)


def readout_fwd(of, ob, p, ng, name):
    r = of.shape[0]

    def body(of_ref, ob_ref, g_ref, ng_ref, y_ref):
        for h in range(NH):
            sl = slice(h * DK, (h + 1) * DK)
            o = of_ref[:, sl] + ob_ref[:, sl]
            on = o * lax.rsqrt(jnp.mean(o * o, axis=-1, keepdims=True) + EPS) * ng_ref[:, sl]
            y_ref[:, sl] = (on * _silu(g_ref[:, sl])).astype(BF16)

    return pl.pallas_call(
        body, name=name, grid=(r // TR,),
        in_specs=[_row_spec(AW), _row_spec(AW), _row_spec(AW, 4), _vec_spec(AW)], out_specs=_row_spec(AW),
        out_shape=jax.ShapeDtypeStruct((r, AW), BF16), compiler_params=_cp(32),
    )(of, ob, p, ng)


def readout_bwd(dy, of, ob, p, ng, name):
    r = of.shape[0]

    def body(dy_ref, of_ref, ob_ref, g_ref, ng_ref, do_ref, dg_ref, dn_ref):
        i = pl.program_id(0)

        @pl.when(i == 0)
        def _():
            dn_ref[...] = jnp.zeros_like(dn_ref)

        for h in range(NH):
            sl = slice(h * DK, (h + 1) * DK)
            o = of_ref[:, sl] + ob_ref[:, sl]
            rstd = lax.rsqrt(jnp.mean(o * o, axis=-1, keepdims=True) + EPS)
            oh = o * rstd
            gv = g_ref[:, sl]
            sig = jax.nn.sigmoid(gv)
            dyv = dy_ref[:, sl]
            don = dyv * (gv * sig)
            dg_ref[:, sl] = (dyv * (oh * ng_ref[:, sl]) * (sig * (1.0 + gv * (1.0 - sig)))).astype(BF16)
            dn_ref[:, sl] += jnp.sum(don * oh, axis=0, keepdims=True)
            doh = don * ng_ref[:, sl]
            do_ref[:, sl] = rstd * (doh - oh * jnp.mean(doh * oh, axis=-1, keepdims=True))

    return pl.pallas_call(
        body, name=name, grid=(r // TR,),
        in_specs=[_row_spec(AW), _row_spec(AW), _row_spec(AW), _row_spec(AW, 4), _vec_spec(AW)],
        out_specs=[_row_spec(AW), _row_spec(AW), _vec_spec(AW)],
        out_shape=[jax.ShapeDtypeStruct((r, AW), F32), jax.ShapeDtypeStruct((r, AW), BF16),
                   jax.ShapeDtypeStruct((1, AW), F32)],
        compiler_params=_cp(32),
    )(dy, of, ob, p, ng)


def _gelu(v):
    return 0.5 * v * (1.0 + lax.erf(v * 0.7071067811865476))


def _gelu_grad(v):
    return 0.5 * (1.0 + lax.erf(v * 0.7071067811865476)) + v * (0.3989422804014327 * jnp.exp(-0.5 * v * v))


def _cmlp_norm(vv, gn):
    vg = _gelu(vv)
    mu = jnp.mean(vg, axis=-1, keepdims=True)
    cen = vg - mu
    rstd = lax.rsqrt(jnp.mean(cen * cen, axis=-1, keepdims=True) + EPS)
    xhat = cen * rstd
    return xhat, rstd, xhat * gn


def chunkmlp_fwd(p, ws, bias, gn, name):
    r = p.shape[0]

    def body(u_ref, v_ref, ws_ref, b_ref, gn_ref, y_ref):
        for ci in range(TR // B_CHUNK):
            rs = slice(ci * B_CHUNK, (ci + 1) * B_CHUNK)
            for gidx in range(NH):
                sl = slice(gidx * DK, (gidx + 1) * DK)
                _, _, vn = _cmlp_norm(v_ref[rs, sl], gn_ref[:, sl])
                mixed = _nn(_bf(ws_ref[gidx]), _bf(vn)) + b_ref[gidx]
                y_ref[rs, sl] = (_gelu(u_ref[rs, sl]) * mixed).astype(BF16)

    return pl.pallas_call(
        body, name=name, grid=(r // TR,),
        in_specs=[_row_spec(AW, 5), _row_spec(AW, 6), pl.BlockSpec((NH, B_CHUNK, B_CHUNK), lambda i: (0, 0, 0)),
                  pl.BlockSpec((NH, B_CHUNK, 1), lambda i: (0, 0, 0)), _vec_spec(AW)],
        out_specs=_row_spec(AW), out_shape=jax.ShapeDtypeStruct((r, AW), BF16), compiler_params=_cp(32),
    )(p, p, ws, bias, gn)


def chunkmlp_bwd(dy, p, ws, bias, gn, name):
    r = p.shape[0]

    def body(dy_ref, u_ref, v_ref, ws_ref, b_ref, gn_ref, du_ref, dv_ref, dws_ref, db_ref, dgn_ref):
        i = pl.program_id(0)

        @pl.when(i == 0)
        def _():
            dws_ref[...] = jnp.zeros_like(dws_ref)
            db_ref[...] = jnp.zeros_like(db_ref)
            dgn_ref[...] = jnp.zeros_like(dgn_ref)

        for ci in range(TR // B_CHUNK):
            rs = slice(ci * B_CHUNK, (ci + 1) * B_CHUNK)
            for gidx in range(NH):
                sl = slice(gidx * DK, (gidx + 1) * DK)
                vv, uv, dyv, gnv = v_ref[rs, sl], u_ref[rs, sl], dy_ref[rs, sl], gn_ref[:, sl]
                xhat, rstd, vn = _cmlp_norm(vv, gnv)
                wg = _bf(ws_ref[gidx])
                mixed = _nn(wg, _bf(vn)) + b_ref[gidx]
                dmixed = dyv * _gelu(uv)
                du_ref[rs, sl] = (dyv * mixed * _gelu_grad(uv)).astype(BF16)
                dws_ref[gidx] += _nt(_bf(dmixed), _bf(vn))
                db_ref[gidx] += jnp.sum(dmixed, axis=-1, keepdims=True)
                dvn = _tn(wg, _bf(dmixed))
                dgn_ref[:, sl] += jnp.sum(dvn * xhat, axis=0, keepdims=True)
                dxh = dvn * gnv
                dvg = rstd * (dxh - jnp.mean(dxh, axis=-1, keepdims=True)
                              - xhat * jnp.mean(dxh * xhat, axis=-1, keepdims=True))
                dv_ref[rs, sl] = (dvg * _gelu_grad(vv)).astype(BF16)

    return pl.pallas_call(
        body, name=name, grid=(r // TR,),
        in_specs=[_row_spec(AW, 1), _row_spec(AW, 5), _row_spec(AW, 6),
                  pl.BlockSpec((NH, B_CHUNK, B_CHUNK), lambda i: (0, 0, 0)),
                  pl.BlockSpec((NH, B_CHUNK, 1), lambda i: (0, 0, 0)), _vec_spec(AW)],
        out_specs=[_row_spec(AW), _row_spec(AW), pl.BlockSpec((NH, B_CHUNK, B_CHUNK), lambda i: (0, 0, 0)),
                   pl.BlockSpec((NH, B_CHUNK, 1), lambda i: (0, 0, 0)), _vec_spec(AW)],
        out_shape=[jax.ShapeDtypeStruct((r, AW), BF16), jax.ShapeDtypeStruct((r, AW), BF16),
                   jax.ShapeDtypeStruct((NH, B_CHUNK, B_CHUNK), F32), jax.ShapeDtypeStruct((NH, B_CHUNK, 1), F32),
                   jax.ShapeDtypeStruct((1, AW), F32)],
        compiler_params=_cp(32),
    )(dy, p, p, ws, bias, gn)


def _win_count(pos, k, n):
    lo = jnp.maximum(pos - k // 2, 0)
    hi = jnp.minimum(pos - k // 2 + k, n)
    return (hi - lo).astype(F32)


POOL_CW = 256
POOL_PAD = (POOL_WINDOWS[-1] // 2) * GRID_W


def pool_op(p, transpose, name):
    r = p.shape[0]
    seq = r - CTX
    grows = seq // GRID_W
    nt = seq // TR

    def body(x_ref, o_ref, y_s):
        j = pl.program_id(0)

        @pl.when(j == 0)
        def _():
            y_s[:POOL_PAD, :] = jnp.zeros((POOL_PAD, POOL_CW), F32)
            y_s[POOL_PAD + seq:, :] = jnp.zeros((POOL_PAD, POOL_CW), F32)

        for gi, k in enumerate(POOL_WINDOWS):
            @pl.when(j // (512 // POOL_CW) == gi)
            def _(k=k):
                offs = list(range(-(k // 2) + 1, k // 2 + 1) if transpose else range(-(k // 2), k // 2))
                tt = lax.broadcasted_iota(jnp.int32, (TR, TR), 0)
                ss = lax.broadcasted_iota(jnp.int32, (TR, TR), 1)
                band = (ss - tt >= offs[0]) & (ss - tt <= offs[-1])
                b_ctx = jnp.where(band, 1.0, 0.0).astype(F32)
                b_grid = jnp.where(band & ((tt >> 6) == (ss >> 6)), 1.0, 0.0).astype(F32)
                trow = lax.broadcasted_iota(jnp.int32, (TR, POOL_CW), 0)

                def count(i):
                    t = i * TR + trow
                    return _win_count(t & (GRID_W - 1), k, GRID_W) * _win_count(t >> 6, k, grows)

                def col_pass(i, carry):
                    xt = x_ref[pl.ds(pl.multiple_of(CTX + i * TR, TR), TR), :]
                    if transpose:
                        xt = xt / count(i)
                    y_s[pl.ds(pl.multiple_of(POOL_PAD + i * TR, TR), TR), :] = jnp.dot(
                        b_grid, xt, precision=HIGHEST, preferred_element_type=F32)
                    return carry

                lax.fori_loop(0, nt, col_pass, 0)

                def row_pass(i, carry):
                    base = POOL_PAD + i * TR
                    acc = y_s[pl.ds(pl.multiple_of(base + offs[0] * GRID_W, GRID_W), TR), :]
                    for d in offs[1:]:
                        acc = acc + y_s[pl.ds(pl.multiple_of(base + d * GRID_W, GRID_W), TR), :]
                    rows = pl.ds(pl.multiple_of(CTX + i * TR, TR), TR)
                    if not transpose:
                        acc = acc / count(i)
                    o_ref[rows, :] = (acc - x_ref[rows, :]).astype(BF16)
                    return carry

                lax.fori_loop(0, nt, row_pass, 0)

                cx = x_ref[:CTX, :]
                cntc = _win_count(trow, k, CTX)
                accc = jnp.dot(b_ctx, cx / cntc if transpose else cx, precision=HIGHEST, preferred_element_type=F32)
                o_ref[:CTX, :] = ((accc if transpose else accc / cntc) - cx).astype(BF16)

    spec = pl.BlockSpec((r, POOL_CW), lambda j: (0, j))
    return pl.pallas_call(
        body, name=name, grid=(D // POOL_CW,), in_specs=[spec], out_specs=spec,
        out_shape=jax.ShapeDtypeStruct((r, D), BF16),
        scratch_shapes=[pltpu.VMEM((seq + 2 * POOL_PAD, POOL_CW), F32)], compiler_params=_cp(56),
    )(p)


def ada_mods(cs, w_ada, b_loc, name):
    nl, _, cl = w_ada.shape
    tn = 1024

    def body(c_ref, w_ref, b_ref, o_ref):
        o_ref[...] = _nn(_bf(_silu(c_ref[...])), _bf(w_ref[...])) + b_ref[...]

    return pl.pallas_call(
        body, name=name, grid=(nl, cl // tn),
        in_specs=[pl.BlockSpec((16, D), lambda l, j: (0, 0)), pl.BlockSpec((None, D, tn), lambda l, j: (l, 0, j)),
                  pl.BlockSpec((None, 1, tn), lambda l, j: (l, 0, j))],
        out_specs=pl.BlockSpec((None, 16, tn), lambda l, j: (l, 0, j)),
        out_shape=jax.ShapeDtypeStruct((nl, 16, cl), F32), compiler_params=_cp(40),
    )(cs, w_ada, b_loc)


def _adamw(w, g, m, v):
    m = ADAM_B1 * m + (1.0 - ADAM_B1) * g
    v = ADAM_B2 * v + (1.0 - ADAM_B2) * (g * g)
    m_hat = m / (1.0 - ADAM_B1 ** ADAM_STEP)
    v_hat = v / (1.0 - ADAM_B2 ** ADAM_STEP)
    delta = -ADAM_LR * (m_hat / (jnp.sqrt(v_hat) + ADAM_EPS) + ADAM_WD * w)
    return delta, m, v


def ada_update(cs, dm, w, m, v, name):
    nl, _, cl = w.shape
    ta, tn = 256, 1024

    def body(c_ref, dm_ref, w_ref, m_ref, v_ref, g_ref, d_ref, nm_ref, nv_ref, dc_ref):
        l, j = pl.program_id(1), pl.program_id(2)
        a = _bf(_silu(c_ref[...]))
        bmat = _bf(dm_ref[...])
        wv = w_ref[...]
        g = _tn(a, bmat)
        g_ref[...] = g
        d_ref[...], nm_ref[...], nv_ref[...] = _adamw(wv, g, m_ref[...], v_ref[...])
        _accum(dc_ref, _nt(bmat, _bf(wv)), (l == 0) & (j == 0))

    wspec = pl.BlockSpec((None, ta, tn), lambda i, l, j: (l, i, j))
    shp = jax.ShapeDtypeStruct(w.shape, F32)
    return pl.pallas_call(
        body, name=name, grid=(D // ta, nl, cl // tn),
        in_specs=[pl.BlockSpec((16, ta), lambda i, l, j: (0, i)),
                  pl.BlockSpec((None, 16, tn), lambda i, l, j: (l, 0, j)), wspec, wspec, wspec],
        out_specs=[wspec, wspec, wspec, wspec, pl.BlockSpec((16, ta), lambda i, l, j: (0, i))],
        out_shape=[shp, shp, shp, shp, jax.ShapeDtypeStruct((16, D), F32)], compiler_params=_cp(40),
    )(cs, dm, w, m, v)


def adamw_big(p, q, w, m, v, l, dsts, name):
    nl = w.shape[0]
    rows, cols = p.shape
    tr = min(rows, 256)
    w3, m3, v3 = (t.reshape(nl, rows, cols) for t in (w, m, v))
    nd = 0 if dsts is None else 4

    def body(p_ref, q_ref, w_ref, m_ref, v_ref, *rest):
        g_ref, d_ref, nm_ref, nv_ref = rest[nd:]
        g = p_ref[...].astype(F32) + q_ref[...].astype(F32)
        g_ref[...] = g
        d_ref[...], nm_ref[...], nv_ref[...] = _adamw(w_ref[...], g, m_ref[...], v_ref[...])

    part = pl.BlockSpec((tr, cols), lambda i: (i, 0))
    spec = pl.BlockSpec((None, tr, cols), lambda i: (l, i, 0))
    shp = jax.ShapeDtypeStruct((nl, rows, cols), F32)
    return pl.pallas_call(
        body, name=name, grid=(rows // tr,),
        in_specs=[part, part, spec, spec, spec] + [ANY] * nd,
        out_specs=[spec] * 4, out_shape=[shp] * 4,
        input_output_aliases={5 + t: t for t in range(nd)}, compiler_params=_cp(48),
    )(p, q, w3, m3, v3, *([] if dsts is None else dsts))


def sum8(g8, name):
    n = g8.shape[1]

    def body(g_ref, o_ref):
        acc = g_ref[0]
        for dev in range(1, 8):
            acc = acc + g_ref[dev]
        o_ref[...] = acc

    return pl.pallas_call(
        body, name=name, grid=(1,), in_specs=[pl.BlockSpec((8, n, 128), lambda i: (0, 0, 0))],
        out_specs=pl.BlockSpec((n, 128), lambda i: (0, 0)),
        out_shape=jax.ShapeDtypeStruct((n, 128), F32), compiler_params=_cp(48),
    )(g8)


def adamw_small(g, w, m, v, name):
    def body(g_ref, w_ref, m_ref, v_ref, d_ref, nm_ref, nv_ref):
        d_ref[...], nm_ref[...], nv_ref[...] = _adamw(w_ref[...], g_ref[...], m_ref[...], v_ref[...])

    spec = pl.BlockSpec(g.shape, lambda i: (0, 0))
    shp = jax.ShapeDtypeStruct(g.shape, F32)
    return pl.pallas_call(
        body, name=name, grid=(1,), in_specs=[spec] * 4, out_specs=[spec] * 3, out_shape=[shp] * 3,
        compiler_params=_cp(48),
    )(g, w, m, v)


def _pack(arrs):
    flat = jnp.concatenate([a.reshape(-1).astype(F32) for a in arrs])
    pad = (-flat.shape[0]) % 1024
    return jnp.pad(flat, (0, pad)).reshape(-1, 128)


def _unpack(packed, shapes):
    flat = packed.reshape(-1)
    out, off = [], 0
    for s in shapes:
        n = 1
        for d in s:
            n *= d
        out.append(flat[off:off + n].reshape(s))
        off += n
    return out


def _lower_bounds(lb_logits):
    pr = jax.nn.softmax(lb_logits.astype(F32), axis=1)
    return jnp.cumsum(pr, axis=1) - pr[:, :1]


class LayerWeights:
    def __init__(self, fetch):
        self.fetch, self.have, self.tokens = fetch, {}, []

    def get(self, key, after):
        if key not in self.have:
            new, token = self.fetch(key, after)
            self.have.update(new)
            if token is not None:
                self.tokens.append(token)
        return self.have[key]

    def tie(self, mod):
        for token in self.tokens:
            mod = mod + token[0, 0]
        self.tokens = []
        return mod


def device_step(xs, target, mods, sp, weights_for, grads_done):
    lbs = _lower_bounds(sp["lb_logits"])
    saved = []
    for layer in range(4):
        wl, md = weights_for(layer), mods[layer]
        w_in = wl.get("in" if layer % 2 == 0 else "pin", xs)
        md = wl.tie(md)
        s = {"xs": xs}
        h1 = norm_mod(xs, sp["g_norm_mix"][layer][None], md, 0, f"norm_mix{layer}")
        s["h1"] = h1
        if layer % 2 == 0:
            e = layer // 2
            p = mm_cols(h1, w_in, 0, f"in_proj{layer}", per=1)
            (of, sf), (ob, sb) = hgrn_fwd(p, lbs[0, e][None], lbs[1, e][None], f"scan{layer}")
            ya = readout_fwd(of, ob, p, sp["g_hgrn_out"][e][None], f"readout{layer}")
            yb = chunkmlp_fwd(p, sp["w_spatial"][e], sp["b_spatial"][e][:, :, None], sp["g_spatial_v"][e][None],
                              f"cmlp{layer}")
            ycat = jnp.concatenate([ya, yb], axis=1)
            w_out = wl.get("out", ycat)
            md = wl.tie(md)
            x1, f1 = mm_rows_res(ycat, w_out, 0, xs, md, 2, f"out_proj{layer}")
            s.update(p=p, of=of, ob=ob, sf=sf, sb=sb, ycat=ycat, f1=f1)
        else:
            o = layer // 2
            pp = mm_rows(h1, w_in, 0, f"pool_in{layer}")
            z = pool_op(pp, False, f"pool{layer}")
            x1, ypre = mm_grp_res(z, wl.get("grp", z), 0, sp["b_grp_pool"][o].reshape(1, D),
                                  sp["scale_pool"][o][None], xs, md, 2, f"pool_grp{layer}")
            s.update(z=z, ypre=ypre)
        h2 = norm_mod(x1, sp["g_norm_ffn"][layer][None], md, 3, f"norm_ffn{layer}")
        u = mm_cols(h2, wl.get("up", h2), 0, f"ffn_up{layer}", epi=_epi_relu2, out_dtype=BF16)
        x2, f2 = mm_rows_res(u, wl.get("down", u), 0, x1, md, 5, f"ffn_down{layer}")
        s.update(x1=x1, h2=h2, u=u, f2=f2, w=wl.have, md=md)
        saved.append(s)
        xs = x2

    dx, loss_lanes, dg_final = final_loss(xs, sp["g_norm_final"][None], target, "final_loss")

    token = jnp.zeros((8, 128), F32)
    dmods = [None] * 4
    sg = {"g_norm_final": dg_final[0], "g_norm_mix": [None] * 4, "g_norm_ffn": [None] * 4,
          "dlbs": [[None, None], [None, None]], "g_hgrn_out": [None] * 2, "w_spatial": [None] * 2,
          "b_spatial": [None] * 2, "g_spatial_v": [None] * 2, "b_grp_pool": [None] * 2, "scale_pool": [None] * 2}
    for layer in reversed(range(4)):
        s = saved[layer]
        wl, md = s["w"], s["md"] + token[0, 0]
        df2, dgt2 = gate_in(dx, s["f2"], md, 5, f"gate_ffn{layer}")
        da = mm_t_rows(df2, wl["down"], 0, f"ffn_down_t{layer}", epi=_epi_2sqrt, out_dtype=BF16,
                       extras_of=s["u"], per=2)
        g_down = grad_rows(s["u"], df2, f"g_ffn_down{layer}", per=2, tn=2048)
        dh2 = mm_t_cols(da, wl["up"], 0, f"ffn_up_t{layer}")
        g_up = grad_cols(s["h2"], da, f"g_ffn_up{layer}", per=1)
        dx1, dsh2, dsc2, dgf = normmod_bwd(dh2, s["x1"], dx, sp["g_norm_ffn"][layer][None], md, 3,
                                           f"norm_ffn_b{layer}")
        sg["g_norm_ffn"][layer] = dgf[0]
        md = md + grads_done(layer, "ffn", [g_up, g_down], dx1)[0, 0]
        if layer % 2 == 0:
            e = layer // 2
            df1, dgt1 = gate_in(dx1, s["f1"], md, 2, f"gate_mix{layer}")
            dycat = mm_t_rows(df1, wl["out"], 0, f"out_proj_t{layer}")
            g_b = grad_rows(s["ycat"], df1, f"g_out_proj{layer}", tn=2048)
            do, dpg, dng = readout_bwd(dycat, s["of"], s["ob"], s["p"], sp["g_hgrn_out"][e][None],
                                       f"readout_b{layer}")
            du, dv, dws, dbs, dgn = chunkmlp_bwd(dycat, s["p"], sp["w_spatial"][e], sp["b_spatial"][e][:, :, None],
                                                 sp["g_spatial_v"][e][None], f"cmlp_b{layer}")
            dq_f, di_f, dff, dlb_f = hgrn_bwd(s["p"], do, s["sf"], lbs[0, e][None], False, f"scan_f_b{layer}")
            dq, di, dfb, dlb_b = hgrn_bwd(s["p"], do, s["sb"], lbs[1, e][None], True, f"scan_b_b{layer}",
                                          add=(dq_f, di_f))
            dp = jnp.concatenate([dq, dff, dfb, di, dpg, du, dv], axis=1)
            dh1 = mm_t_cols(dp, wl["in"], 0, f"in_proj_t{layer}")
            g_a = grad_cols(s["h1"], dp, f"g_in_proj{layer}", per=1)
            sg["dlbs"][0][e], sg["dlbs"][1][e] = dlb_f[0], dlb_b[0]
            sg["g_hgrn_out"][e], sg["w_spatial"][e] = dng[0], dws
            sg["b_spatial"][e], sg["g_spatial_v"][e] = dbs[:, :, 0], dgn[0]
        else:
            o = layer // 2
            dyp, dgt1, dscale, dbias = gate_in_pool(dx1, s["ypre"], md, sp["scale_pool"][o][None], 2,
                                                    f"gate_mix{layer}")
            dz = mm_t_grp(dyp, wl["grp"], 0, f"pool_grp_t{layer}")
            g_b = grad_grp(s["z"], dyp, f"g_pool_grp{layer}")
            dpp = pool_op(dz, True, f"pool_t{layer}")
            dh1 = mm_t_rows(dpp, wl["pin"], 0, f"pool_in_t{layer}")
            g_a = grad_rows(s["h1"], dpp, f"g_pool_in{layer}", tn=2048)
            sg["b_grp_pool"][o], sg["scale_pool"][o] = dbias[0].reshape(4, 512), dscale[0]
        dx, dsh1, dsc1, dgm = normmod_bwd(dh1, s["xs"], dx1, sp["g_norm_mix"][layer][None], md, 0,
                                          f"norm_mix_b{layer}")
        sg["g_norm_mix"][layer] = dgm[0]
        dmods[layer] = jnp.concatenate([dsh1, dsc1, dgt1, dsh2, dsc2, dgt2], axis=1)
        token = grads_done(layer, "mix", [g_a, g_b], dx)
    return loss_lanes, dx, dmods, sg


BIG = ("w_in_even", "w_out_even", "w_in_pool", "w_grp_pool", "w_ffn_up", "w_ffn_down")
SMALL = ("b_ada", "g_norm_mix", "g_norm_ffn", "lb_logits", "g_hgrn_out", "w_spatial", "b_spatial", "g_spatial_v",
         "b_grp_pool", "scale_pool", "g_norm_final")
WEIGHTS = ("c_ctx", "w_ada", "b_ada", "g_norm_mix", "g_norm_ffn", "w_in_even", "w_out_even", "lb_logits",
           "g_hgrn_out", "w_spatial", "b_spatial", "g_spatial_v", "w_in_pool", "w_grp_pool", "b_grp_pool",
           "scale_pool", "w_ffn_up", "w_ffn_down", "g_norm_final")


def kernel(x, c, ctx, c_ctx, w_ada, b_ada, g_norm_mix, g_norm_ffn, w_in_even, w_out_even, lb_logits, g_hgrn_out, w_spatial, b_spatial, g_spatial_v, w_in_pool, w_grp_pool, b_grp_pool, scale_pool, w_ffn_up, w_ffn_down, g_norm_final, loss_target, m_c_ctx, m_w_ada, m_b_ada, m_g_norm_mix, m_g_norm_ffn, m_w_in_even, m_w_out_even, m_lb_logits, m_g_hgrn_out, m_w_spatial, m_b_spatial, m_g_spatial_v, m_w_in_pool, m_w_grp_pool, m_b_grp_pool, m_scale_pool, m_w_ffn_up, m_w_ffn_down, m_g_norm_final, v_c_ctx, v_w_ada, v_b_ada, v_g_norm_mix, v_g_norm_ffn, v_w_in_even, v_w_out_even, v_lb_logits, v_g_hgrn_out, v_w_spatial, v_b_spatial, v_g_spatial_v, v_w_in_pool, v_w_grp_pool, v_b_grp_pool, v_scale_pool, v_w_ffn_up, v_w_ffn_down, v_g_norm_final):
    loc = dict(c_ctx=c_ctx, w_ada=w_ada, b_ada=b_ada, g_norm_mix=g_norm_mix, g_norm_ffn=g_norm_ffn,
               w_in_even=w_in_even, w_out_even=w_out_even, lb_logits=lb_logits, g_hgrn_out=g_hgrn_out,
               w_spatial=w_spatial, b_spatial=b_spatial, g_spatial_v=g_spatial_v, w_in_pool=w_in_pool,
               w_grp_pool=w_grp_pool, b_grp_pool=b_grp_pool, scale_pool=scale_pool, w_ffn_up=w_ffn_up,
               w_ffn_down=w_ffn_down, g_norm_final=g_norm_final)
    mom = dict(c_ctx=m_c_ctx, w_ada=m_w_ada, b_ada=m_b_ada, g_norm_mix=m_g_norm_mix, g_norm_ffn=m_g_norm_ffn,
               w_in_even=m_w_in_even, w_out_even=m_w_out_even, lb_logits=m_lb_logits, g_hgrn_out=m_g_hgrn_out,
               w_spatial=m_w_spatial, b_spatial=m_b_spatial, g_spatial_v=m_g_spatial_v, w_in_pool=m_w_in_pool,
               w_grp_pool=m_w_grp_pool, b_grp_pool=m_b_grp_pool, scale_pool=m_scale_pool, w_ffn_up=m_w_ffn_up,
               w_ffn_down=m_w_ffn_down, g_norm_final=m_g_norm_final)
    var = dict(c_ctx=v_c_ctx, w_ada=v_w_ada, b_ada=v_b_ada, g_norm_mix=v_g_norm_mix, g_norm_ffn=v_g_norm_ffn,
               w_in_even=v_w_in_even, w_out_even=v_w_out_even, lb_logits=v_lb_logits, g_hgrn_out=v_g_hgrn_out,
               w_spatial=v_w_spatial, b_spatial=v_b_spatial, g_spatial_v=v_g_spatial_v, w_in_pool=v_w_in_pool,
               w_grp_pool=v_w_grp_pool, b_grp_pool=v_b_grp_pool, scale_pool=v_scale_pool, w_ffn_up=v_w_ffn_up,
               w_ffn_down=v_w_ffn_down, g_norm_final=v_g_norm_final)
    mx, my, mc = _mesh_pos()
    chip = 2 * mx + my
    dev = 2 * chip + mc

    def layer_tensors(layer):
        i = layer // 2
        pair = ("w_in_even", "w_out_even") if layer % 2 == 0 else ("w_in_pool", "w_grp_pool")
        return [(pair[0], i), (pair[1], i), ("w_ffn_up", layer), ("w_ffn_down", layer)]

    def layer_keys(layer):
        return ("in", "out", "up", "down") if layer % 2 == 0 else ("pin", "grp", "up", "down")

    def gather_parts(layer):
        return {"a": slice(0, 1), "b": slice(1, 4)} if layer == 0 else {"a": slice(0, 4)}

    def start_gather(layer, part, after):
        srcs = [loc[n][i][None].astype(BF16) for n, i in layer_tensors(layer)[gather_parts(layer)[part]]]
        lands = [lax.dynamic_update_slice(lax.empty((NSH,) + s.shape, BF16), s[None], (chip,) + (0,) * s.ndim)
                 for s in srcs]
        return ici_start(srcs, lands, after, False, f"gather_start{layer}{part}")

    hello = allgather8(_pack([c[0], lb_logits, b_grp_pool, scale_pool]), "gather_small")
    parts = [_unpack(hello[d], [(D,), (2, 2, 256), (2, 4, 128), (2, 512)]) for d in range(8)]
    cs = jnp.concatenate([jnp.stack([parts[d][0] for d in range(8)]), c_ctx[None], jnp.zeros((7, D), F32)])
    chips_of = [parts[2 * s] for s in range(NSH)]
    sp = dict(loc)
    sp["lb_logits"] = jnp.concatenate([q[1] for q in chips_of], axis=2)
    sp["b_grp_pool"] = jnp.concatenate([q[2] for q in chips_of], axis=2)
    sp["scale_pool"] = jnp.concatenate([q[3] for q in chips_of], axis=1)

    b_loc = lax.dynamic_slice_in_dim(b_ada, chip * 3072, 3072, axis=1)[:, None, :]
    mods_loc = ada_mods(cs, w_ada, b_loc, "ada_mods")
    mods_all = allgather8(mods_loc.reshape(-1, 128), "gather_mods").reshape(8, 4, 16, 3072)
    mods_full = jnp.concatenate([mods_all[2 * s] for s in range(NSH)], axis=2)
    mine = lax.dynamic_index_in_dim(mods_full, dev, axis=1, keepdims=False)
    mods = [jnp.stack([mods_full[l, 8].reshape(6, D), mine[l].reshape(6, D)]) for l in range(4)]

    first_a = start_gather(0, "a", mods_all)
    gathers = {(0, "a"): first_a, (0, "b"): start_gather(0, "b", first_a[4])}
    mods[0] = mods[0] + gathers[(0, "b")][4][0, 0]

    def weights_for(layer):
        def fetch(key, after):
            part = "b" if layer == 0 and key != "in" else "a"
            _, got = ici_wait(gathers[(layer, part)], after, False, f"gather_wait{layer}{part}")
            token = None
            if layer < 3 and part == list(gather_parts(layer))[-1]:
                gathers[(layer + 1, "a")] = start_gather(layer + 1, "a", got[0])
                token = gathers[(layer + 1, "a")][4]
            return dict(zip(layer_keys(layer)[gather_parts(layer)[part]], got)), token
        return LayerWeights(fetch)

    exchanges = {}

    def grads_done(layer, part, gs, after):
        lands = [lax.empty((8,) + g.shape[1:], BF16) for g in gs]
        exchanges[(layer, part)] = ici_start(gs, lands, after, True, f"exchange_start{layer}{part}")
        return exchanges[(layer, part)][4]

    xs = jnp.concatenate([ctx[0], x[0]], axis=0)
    loss_lanes, dxs, dmods, sg = device_step(xs, loss_target[0], mods, sp, weights_for, grads_done)
    grad_x = dxs[CTX:][None]

    out = {}

    def finish(layer, part, after):
        gs, rbs = ici_wait(exchanges[(layer, part)], after, True, f"exchange_wait{layer}{part}")
        names = layer_tensors(layer)[slice(2, 4) if part == "ffn" else slice(0, 2)]
        ps = [sum_blocks(g, rb, f"sum_{n}{i}") for (n, i), g, rb in zip(names, gs, rbs)]
        qs = swap_sibling(ps, f"swap{layer}{part}")
        for (n, i), p, q in zip(names, ps, qs):
            out[n] = adamw_big(p, q, loc[n], mom[n], var[n], i, out.get(n), f"adamw_{n}{i}")

    done = dmods[0] + exchanges[(0, "mix")][4][0, 0]
    dmods[0] = done
    for layer in (3, 2, 1):
        finish(layer, "ffn", done)
        finish(layer, "mix", done)
    finish(0, "ffn", done)

    dm_lat = jnp.stack([dmods[l][1].reshape(6 * D) for l in range(4)])
    dm_ctx = jnp.stack([dmods[l][0].reshape(6 * D) for l in range(4)])
    small_shapes = [(4, 6 * D), (4, 6 * D), (4, D), (4, D), (2, 2, AW), (2, AW), (2, NH, 128, 128), (2, NH, 128),
                    (2, AW), (2, 4, 512), (2, D), (D,), (128,)]
    mine_small = _pack([dm_lat, dm_ctx, jnp.stack(sg["g_norm_mix"]), jnp.stack(sg["g_norm_ffn"]),
                        jnp.stack([jnp.stack(sg["dlbs"][0]), jnp.stack(sg["dlbs"][1])]),
                        jnp.stack(sg["g_hgrn_out"]), jnp.stack(sg["w_spatial"]), jnp.stack(sg["b_spatial"]),
                        jnp.stack(sg["g_spatial_v"]), jnp.stack(sg["b_grp_pool"]), jnp.stack(sg["scale_pool"]),
                        sg["g_norm_final"], loss_lanes[0]])
    all_small = allgather8(mine_small, "gather_small_grads")
    tot = _unpack(sum8(all_small, "sum_small_grads"), small_shapes)
    (_, dm_ctx_tot, g_mix, g_ffn, dlbs, g_hg, g_ws, g_bs, g_gv, g_bg, g_sc, g_fin, loss_v) = tot
    loss = loss_v[0]
    dm_lat_all = jnp.stack([_unpack(all_small[d], small_shapes[:1])[0] for d in range(8)])
    g_b_ada = jnp.sum(dm_lat_all, axis=0) + dm_ctx_tot
    _, lb_vjp = jax.vjp(_lower_bounds, sp["lb_logits"])
    g_lb_full = lb_vjp(dlbs)[0]
    grads = {"b_ada": g_b_ada, "g_norm_mix": g_mix, "g_norm_ffn": g_ffn,
             "lb_logits": lax.dynamic_slice_in_dim(g_lb_full, chip * 256, 256, axis=2),
             "g_hgrn_out": g_hg, "w_spatial": g_ws, "b_spatial": g_bs, "g_spatial_v": g_gv,
             "b_grp_pool": lax.dynamic_slice_in_dim(g_bg, chip * 128, 128, axis=2),
             "scale_pool": lax.dynamic_slice_in_dim(g_sc, chip * 512, 512, axis=1), "g_norm_final": g_fin}

    dm_rows = jnp.concatenate([dm_lat_all.transpose(1, 0, 2), dm_ctx_tot[:, None, :], jnp.zeros((4, 7, 6 * D), F32)],
                              axis=1)
    dm_loc = lax.dynamic_slice_in_dim(dm_rows, chip * 3072, 3072, axis=2)
    g_wa, d_wa, nm_wa, nv_wa, dc_part = ada_update(cs, dm_loc, w_ada, m_w_ada, v_w_ada, "ada_update")
    out["w_ada"] = [g_wa, d_wa, nm_wa, nv_wa]
    dc_all = allgather8(dc_part[8].reshape(16, 128), "gather_dc")
    dpre = dc_all[0] + dc_all[2] + dc_all[4] + dc_all[6]
    sig = jax.nn.sigmoid(c_ctx)
    grads["c_ctx"] = dpre.reshape(D) * (sig * (1.0 + c_ctx * (1.0 - sig)))

    names = ("c_ctx",) + SMALL
    shapes = [loc[n].shape for n in names]
    d_s, nm_s, nv_s = adamw_small(_pack([grads[n] for n in names]), _pack([loc[n] for n in names]),
                                  _pack([mom[n] for n in names]), _pack([var[n] for n in names]), "adamw_small")
    for n, dl, nm, nv in zip(names, _unpack(d_s, shapes), _unpack(nm_s, shapes), _unpack(nv_s, shapes)):
        out[n] = [grads[n], dl, nm, nv]

    finish(0, "mix", d_s)
    for n in BIG:
        out[n] = [t.reshape(loc[n].shape) for t in out[n]]

    return (loss, grad_x, *[out[n][0] for n in WEIGHTS], *[out[n][1] for n in WEIGHTS],
            *[out[n][2] for n in WEIGHTS], *[out[n][3] for n in WEIGHTS])
```
